```python
import numpy as np
import jax
import jax.numpy as jnp
from jax import lax

D_MODEL = 4096
BATCH = 1
SEQ = 8192
DEPTH = 4

N_A_LAYERS = DEPTH // 2
N_B_LAYERS = DEPTH - N_A_LAYERS
N_DENSE_LAYERS = (DEPTH + 1) // 2
N_MOE_LAYERS = DEPTH // 2

A_HEAD_DIM = 128
A_HEADS = D_MODEL // A_HEAD_DIM
A_CHUNK = 64

B_HEAD_DIM = 64
B_Q_HEADS = D_MODEL // B_HEAD_DIM
B_KV_HEADS = 8
B_GROUP = B_Q_HEADS // B_KV_HEADS
B_KV_DIM = B_KV_HEADS * B_HEAD_DIM
WINDOW = 128
B_BLOCK = 128

REL_BUCKETS = 32
REL_MAX_DISTANCE = 128

D_FF_DENSE = 2 * D_MODEL
N_EXPERTS = 8
TOP_K = 2
D_FF_EXPERT = D_MODEL // 2

PLE_DIM = 256

DEEPNORM_ALPHA = (2.0 * DEPTH) ** 0.25
DEEPNORM_BETA = (8.0 * DEPTH) ** -0.25
LN_EPS = 1e-5
RMS_EPS = 1e-6

kernel_name = 'yoco_hgrn2_swa_sink_moe_trunk'


def layer_norm(x, g, b):
    xf = x.astype(jnp.float32)
    mu = jnp.mean(xf, -1, keepdims=True)
    var = jnp.mean(jnp.square(xf - mu), -1, keepdims=True)
    y = (xf - mu) * lax.rsqrt(var + LN_EPS) * g.astype(jnp.float32) + b.astype(jnp.float32)
    return y.astype(x.dtype)


def _hgrn2_chunk_step(state, chunk):
    q, k, v, log_f = chunk
    cum = jnp.cumsum(log_f, axis=2)
    c = q.shape[2]
    causal = jnp.tril(jnp.ones((c, c), dtype=bool))[None, None, :, :, None]
    diff = cum[:, :, :, None, :] - cum[:, :, None, :, :]
    decay = jnp.exp(jnp.where(causal, diff, -jnp.inf))
    scores = jnp.einsum('bhtk,bhsk,bhtsk->bhts', q, k, decay)
    o = (jnp.einsum('bhts,bhsv->bhtv', scores, v)
         + jnp.einsum('bhtk,bhkv->bhtv', q * jnp.exp(cum), state))
    last = cum[:, :, -1:, :]
    new_state = (jnp.exp(last[:, :, 0, :])[..., None] * state
                 + jnp.einsum('bhsk,bhsv->bhkv', k * jnp.exp(last - cum), v))
    return new_state, o


def hgrn2_mixer(x, w_in, lower_bound, out_g, w_o):
    b, s, _ = x.shape
    nc = s // A_CHUNK
    z = jnp.einsum('bsd,de->bse', x, w_in).astype(jnp.float32)
    q_raw, f_raw, i_val, g_out = jnp.split(z, 4, axis=-1)
    lb = lower_bound.astype(jnp.float32)
    log_f = jnp.logaddexp(jnp.log(lb), jnp.log1p(-lb) + jax.nn.log_sigmoid(f_raw))
    k = (1.0 - lb) * jax.nn.sigmoid(-f_raw)
    q = jax.nn.silu(q_raw)

    def to_chunks(t):
        return t.reshape(b, nc, A_CHUNK, A_HEADS, A_HEAD_DIM).transpose(1, 0, 3, 2, 4)

    state0 = jnp.zeros((b, A_HEADS, A_HEAD_DIM, A_HEAD_DIM), jnp.float32)
    _, o = lax.scan(_hgrn2_chunk_step, state0,
                    (to_chunks(q), to_chunks(k), to_chunks(i_val), to_chunks(log_f)))
    o = o.transpose(1, 0, 3, 2, 4).reshape(b, s, A_HEADS, A_HEAD_DIM)
    o = o * lax.rsqrt(jnp.mean(jnp.square(o), -1, keepdims=True) + RMS_EPS)
    o = o.reshape(b, s, D_MODEL) * out_g.astype(jnp.float32) * jax.nn.silu(g_out)
    return jnp.einsum('bsd,de->bse', o.astype(x.dtype), w_o)


def _rel_bucket_band():
    i = np.arange(B_BLOCK)[:, None]
    j = np.arange(2 * B_BLOCK)[None, :]
    dist = B_BLOCK + i - j
    in_window = (dist >= 0) & (dist < WINDOW)
    d = np.clip(dist, 0, None)
    max_exact = REL_BUCKETS // 2
    large = max_exact + (np.log(np.maximum(d, 1) / max_exact)
                         / np.log(REL_MAX_DISTANCE / max_exact)
                         * (REL_BUCKETS - max_exact)).astype(np.int32)
    large = np.minimum(large, REL_BUCKETS - 1)
    bucket = np.where(d < max_exact, d, large).astype(np.int32)
    return bucket, in_window


def swa_sink_mixer(x, k, v, w_q, sinks, rel_bias, w_o):
    b, s, _ = x.shape
    c = B_BLOCK
    nb = s // c
    q = jnp.einsum('bsd,de->bse', x, w_q).reshape(b, nb, c, B_KV_HEADS, B_GROUP, B_HEAD_DIM)

    def band(t):
        tp = jnp.pad(t, ((0, 0), (c, 0), (0, 0), (0, 0))).reshape(b, nb + 1, c, B_KV_HEADS, B_HEAD_DIM)
        return jnp.concatenate([tp[:, :-1], tp[:, 1:]], axis=2)

    kb, vb = band(k), band(v)
    logits = jnp.einsum('bnqhgd,bnshd->bnhgqs', q, kb).astype(jnp.float32) * (B_HEAD_DIM ** -0.5)
    bucket, in_window = _rel_bucket_band()
    bias = jnp.transpose(rel_bias.astype(jnp.float32)[bucket], (2, 0, 1))
    bias = bias.reshape(B_KV_HEADS, B_GROUP, c, 2 * c)
    key_valid = (jnp.arange(nb)[:, None] > 0) | (jnp.arange(2 * c)[None, :] >= c)
    mask = jnp.asarray(in_window)[None] & key_valid[:, None, :]
    logits = jnp.where(mask[None, :, None, None], logits + bias, -jnp.inf)
    sink = sinks.astype(jnp.float32).reshape(1, 1, B_KV_HEADS, B_GROUP, 1, 1)
    m = jnp.maximum(jnp.max(logits, -1, keepdims=True), sink)
    probs = jnp.exp(logits - m)
    denom = jnp.sum(probs, -1, keepdims=True) + jnp.exp(sink - m)
    probs = (probs / denom).astype(x.dtype)
    o = jnp.einsum('bnhgqs,bnshd->bnqhgd', probs, vb).reshape(b, s, D_MODEL)
    return jnp.einsum('bsd,de->bse', o, w_o)


def swiglu(x, w1, w3, w2):
    return jnp.einsum('bsf,fd->bsd', jax.nn.silu(jnp.einsum('bsd,df->bsf', x, w1))
                      * jnp.einsum('bsd,df->bsf', x, w3), w2)


def moe_swiglu(x, w_router, w1, w3, w2):
    logits = jnp.einsum('bsd,de->bse', x, w_router).astype(jnp.float32)
    top_val, top_idx = lax.top_k(logits, TOP_K)
    top_w = jax.nn.softmax(top_val, axis=-1)
    gates = jnp.sum(jax.nn.one_hot(top_idx, N_EXPERTS, dtype=jnp.float32) * top_w[..., None], axis=-2)
    h = jax.nn.silu(jnp.einsum('bsd,edf->ebsf', x, w1)) * jnp.einsum('bsd,edf->ebsf', x, w3)
    h = h * jnp.transpose(gates, (2, 0, 1))[..., None].astype(h.dtype)
    return jnp.einsum('ebsf,efd->bsd', h, w2)


def setup_inputs(seed: int = 0) -> dict:
    key = jax.random.key(seed)
    ks = jax.random.split(key, 26)
    d = D_MODEL
    beta = DEEPNORM_BETA

    def nrm(k, shape, scale):
        return jax.random.normal(k, shape, jnp.float32) * scale

    return {
        'x': nrm(ks[0], (BATCH, SEQ, d), 1.0),
        'p': nrm(ks[1], (DEPTH, BATCH, SEQ, PLE_DIM), 1.0),
        'ln1_g': 1.0 + nrm(ks[2], (DEPTH, d), 0.02),
        'ln1_b': nrm(ks[3], (DEPTH, d), 0.02),
        'ln2_g': 1.0 + nrm(ks[4], (DEPTH, d), 0.02),
        'ln2_b': nrm(ks[5], (DEPTH, d), 0.02),
        'a_w_in': nrm(ks[6], (N_A_LAYERS, d, 4 * d), d ** -0.5),
        'a_lb_logits': nrm(ks[7], (N_A_LAYERS + 1, d), 1.0),
        'a_out_g': 1.0 + nrm(ks[8], (N_A_LAYERS, d), 0.02),
        'a_w_o': nrm(ks[9], (N_A_LAYERS, d, d), beta * d ** -0.5),
        'kv_w': jnp.concatenate([nrm(ks[10], (d, B_KV_DIM), d ** -0.5),
                                 nrm(ks[11], (d, B_KV_DIM), beta * d ** -0.5)], axis=1),
        'b_w_q': nrm(ks[12], (N_B_LAYERS, d, d), d ** -0.5),
        'b_sinks': nrm(ks[13], (N_B_LAYERS, B_Q_HEADS), 1.0),
        'b_w_o': nrm(ks[14], (N_B_LAYERS, d, d), beta * d ** -0.5),
        'rel_bias': nrm(ks[15], (REL_BUCKETS, B_Q_HEADS), 0.5),
        'ffn_w1': nrm(ks[16], (N_DENSE_LAYERS, d, D_FF_DENSE), d ** -0.5),
        'ffn_w3': nrm(ks[17], (N_DENSE_LAYERS, d, D_FF_DENSE), d ** -0.5),
        'ffn_w2': nrm(ks[18], (N_DENSE_LAYERS, D_FF_DENSE, d), beta * D_FF_DENSE ** -0.5),
        'moe_router': nrm(ks[19], (N_MOE_LAYERS, d, N_EXPERTS), d ** -0.5),
        'moe_w1': nrm(ks[20], (N_MOE_LAYERS, N_EXPERTS, d, D_FF_EXPERT), d ** -0.5),
        'moe_w3': nrm(ks[21], (N_MOE_LAYERS, N_EXPERTS, d, D_FF_EXPERT), d ** -0.5),
        'moe_w2': nrm(ks[22], (N_MOE_LAYERS, N_EXPERTS, D_FF_EXPERT, d), beta * D_FF_EXPERT ** -0.5),
        'ple_w_proj': nrm(ks[23], (DEPTH, PLE_DIM, d), PLE_DIM ** -0.5),
        'ple_w_gate': nrm(ks[24], (DEPTH, d, d), d ** -0.5),
    }


def reference(x, p, ln1_g, ln1_b, ln2_g, ln2_b, a_w_in, a_lb_logits, a_out_g, a_w_o, kv_w,
              b_w_q, b_sinks, b_w_o, rel_bias, ffn_w1, ffn_w3, ffn_w2, moe_router, moe_w1,
              moe_w3, moe_w2, ple_w_proj, ple_w_gate):
    b, s, _ = x.shape
    lower_bounds = jnp.cumsum(jax.nn.softmax(a_lb_logits.astype(jnp.float32), axis=0), axis=0)
    k_shared = None
    v_shared = None
    for i in range(DEPTH):
        if i < N_A_LAYERS:
            h = hgrn2_mixer(x, a_w_in[i], lower_bounds[i], a_out_g[i], a_w_o[i])
        else:
            j = i - N_A_LAYERS
            h = swa_sink_mixer(x, k_shared, v_shared, b_w_q[j], b_sinks[j], rel_bias, b_w_o[j])
        x = layer_norm(DEEPNORM_ALPHA * x + h, ln1_g[i], ln1_b[i])
        if i % 2 == 0:
            f = swiglu(x, ffn_w1[i // 2], ffn_w3[i // 2], ffn_w2[i // 2])
        else:
            f = moe_swiglu(x, moe_router[i // 2], moe_w1[i // 2], moe_w3[i // 2], moe_w2[i // 2])
        x = layer_norm(DEEPNORM_ALPHA * x + f, ln2_g[i], ln2_b[i])
        x = x + (jax.nn.sigmoid(jnp.einsum('bsd,de->bse', x, ple_w_gate[i]))
                 * jnp.einsum('bsp,pd->bsd', p[i], ple_w_proj[i]))
        if i == N_A_LAYERS - 1:
            kv = jnp.einsum('bsd,de->bse', x, kv_w)
            k_shared = kv[..., :B_KV_DIM].reshape(b, s, B_KV_HEADS, B_HEAD_DIM)
            v_shared = kv[..., B_KV_DIM:].reshape(b, s, B_KV_HEADS, B_HEAD_DIM)
    return x
```

```python
import functools

import numpy as np
import jax
import jax.numpy as jnp
from jax import lax
from jax.experimental import pallas as pl
from jax.experimental.pallas import tpu as pltpu

F32 = jnp.float32
BF16 = jnp.bfloat16

V7X_LANES = 128
V7X_SCOPED_VMEM_BYTES = 60000 * 1024
COMPILER_SCRATCH_BYTES = 6 * 1024 * 1024

A_HEAD_DIM = 128
A_CHUNK = 64
A_SUB = 16
B_HEAD_DIM = 64
B_BLOCK = 128
WINDOW = 128
REL_BUCKETS = 32
REL_MAX_DISTANCE = 128
TOP_K = 2
LN_EPS = 1e-5
RMS_EPS = 1e-6


def _vmem_limit(block_bytes):
    return int(min(block_bytes + COMPILER_SCRATCH_BYTES, V7X_SCOPED_VMEM_BYTES))


def _nbytes(shape, dtype):
    return int(np.prod(shape)) * jnp.dtype(dtype).itemsize


def _silu(x):
    return x * jax.nn.sigmoid(x)


def _mm_body(*refs, nw, ne, no, nk, epilogue):
    x_ref = refs[0]
    w_refs = refs[1:1 + nw]
    extra_refs = refs[1 + nw:1 + nw + ne]
    out_refs = refs[1 + nw + ne:1 + nw + ne + no]
    acc_refs = refs[1 + nw + ne + no:]
    xv = x_ref[...]
    prods = [jnp.dot(xv, w_ref[...], preferred_element_type=F32) for w_ref in w_refs]
    if nk == 1:
        epilogue(prods, extra_refs, out_refs)
        return
    k = pl.program_id(2)

    @pl.when(k == 0)
    def _():
        for acc, d in zip(acc_refs, prods):
            acc[...] = d

    @pl.when(k > 0)
    def _():
        for acc, d in zip(acc_refs, prods):
            acc[...] += d

    @pl.when(k == nk - 1)
    def _():
        epilogue([acc[...] for acc in acc_refs], extra_refs, out_refs)


def _fused_matmul(x, ws, n_cols, extras, outs, epilogue, *, tm, tn, tk, name):
    m, kdim = x.shape
    tm, tn, tk = min(tm, m), min(tn, n_cols), min(tk, kdim)
    assert m % tm == 0 and n_cols % tn == 0 and kdim % tk == 0
    nk = kdim // tk
    grid = (m // tm, n_cols // tn, nk)
    in_specs = [pl.BlockSpec((tm, tk), lambda i, j, k: (i, k))]
    operands = [x]
    vmem = 2 * _nbytes((tm, tk), x.dtype)
    for w, off in ws:
        in_specs.append(pl.BlockSpec((tk, tn), lambda i, j, k, off=off: (k, j + off)))
        operands.append(w)
        vmem += 2 * _nbytes((tk, tn), w.dtype)
    for arr, bshape, imap in extras:
        in_specs.append(pl.BlockSpec(bshape, lambda i, j, k, imap=imap: imap(i, j)))
        operands.append(arr)
        vmem += 2 * _nbytes(bshape, arr.dtype)
    out_shapes, out_specs = [], []
    for shape, dtype, bshape, imap in outs:
        out_shapes.append(jax.ShapeDtypeStruct(shape, dtype))
        out_specs.append(pl.BlockSpec(bshape, lambda i, j, k, imap=imap: imap(i, j)))
        vmem += 2 * _nbytes(bshape, dtype)
    scratch = [pltpu.VMEM((tm, tn), F32) for _ in ws] if nk > 1 else []
    vmem += len(scratch) * _nbytes((tm, tn), F32)
    vmem += (1 + len(ws)) * _nbytes((tm, tn), F32)
    body = functools.partial(_mm_body, nw=len(ws), ne=len(extras), no=len(outs), nk=nk,
                             epilogue=epilogue)
    res = pl.pallas_call(
        body,
        grid=grid,
        in_specs=in_specs,
        out_specs=out_specs,
        out_shape=out_shapes,
        scratch_shapes=scratch,
        compiler_params=pltpu.CompilerParams(
            dimension_semantics=("parallel", "parallel", "arbitrary"),
            vmem_limit_bytes=_vmem_limit(vmem)),
        name=name,
    )(*operands)
    return res


def _tile(tm, tn):
    return (tm, tn), (lambda i, j: (i, j))


def _ln_rows(y, g, b):
    mu = jnp.mean(y, axis=-1, keepdims=True)
    yc = y - mu
    var = jnp.mean(yc * yc, axis=-1, keepdims=True)
    return yc * lax.rsqrt(var + LN_EPS) * g + b


def _ln_body(y_ref, g_ref, b_ref, o32_ref, o16_ref):
    out = _ln_rows(y_ref[...], g_ref[...], b_ref[...])
    o32_ref[...] = out
    o16_ref[...] = out.astype(BF16)


def _layer_norm(y, g, b, *, tr=256):
    s, d = y.shape
    tr = min(tr, s)
    assert s % tr == 0
    row = pl.BlockSpec((tr, d), lambda i: (i, 0))
    vec = pl.BlockSpec((1, d), lambda i: (0, 0))
    vmem = 2 * (2 * _nbytes((tr, d), F32) + _nbytes((tr, d), BF16)) + 2 * _nbytes((tr, d), F32)
    return pl.pallas_call(
        _ln_body,
        grid=(s // tr,),
        in_specs=[row, vec, vec],
        out_specs=[row, row],
        out_shape=[jax.ShapeDtypeStruct((s, d), F32), jax.ShapeDtypeStruct((s, d), BF16)],
        compiler_params=pltpu.CompilerParams(dimension_semantics=("parallel",),
                                             vmem_limit_bytes=_vmem_limit(vmem)),
        name="layer_norm",
    )(y, g.reshape(1, d), b.reshape(1, d))


def _lower_bound(lbl, layer):
    mx = jnp.max(lbl, axis=0, keepdims=True)
    e = jnp.exp(lbl - mx)
    return jnp.sum(e[:layer + 1], axis=0, keepdims=True) / jnp.sum(e, axis=0, keepdims=True)


def _hgrn_in_proj(xb, w_in, lb_logits, layer, *, tm=1024, tn=1024):
    s, d = xb.shape
    nseg = d // min(tn, d)
    blk, imap = _tile(min(tm, s), min(tn, d))

    def ep_silu(accs, extras, outs):
        outs[0][...] = _silu(accs[0]).astype(BF16)

    def ep_id(accs, extras, outs):
        outs[0][...] = accs[0].astype(BF16)

    def ep_forget(accs, extras, outs):
        lb = _lower_bound(extras[0][...], layer)
        f_raw = accs[0]
        log_sig = jnp.minimum(f_raw, 0.0) - jnp.log1p(jnp.exp(-jnp.abs(f_raw)))
        outs[0][...] = jnp.logaddexp(jnp.log(lb), jnp.log1p(-lb) + log_sig)
        outs[1][...] = ((1.0 - lb) * jax.nn.sigmoid(-f_raw)).astype(BF16)

    def call(seg, epilogue, extras, out_dtypes, name):
        outs = [((s, d), dt, blk, imap) for dt in out_dtypes]
        return _fused_matmul(xb, [(w_in, seg * nseg)], d, extras, outs, epilogue,
                             tm=tm, tn=tn, tk=d, name=name)

    nslot = lb_logits.shape[0]
    lb_extra = [(lb_logits, (nslot, blk[1]), lambda i, j: (0, j))]
    (q,) = call(0, ep_silu, [], [BF16], "hgrn_q")
    log_f, k = call(1, ep_forget, lb_extra, [F32, BF16], "hgrn_f")
    (v,) = call(2, ep_id, [], [BF16], "hgrn_v")
    (g,) = call(3, ep_silu, [], [BF16], "hgrn_g")
    return q, k, v, log_f, g


def _hgrn_body(q_ref, k_ref, v_ref, lf_ref, g_ref, og_ref, o_ref, st_ref, *, tc):
    @pl.when(pl.program_id(1) == 0)
    def _():
        st_ref[...] = jnp.zeros_like(st_ref)

    lf = lf_ref[...]
    row = lax.broadcasted_iota(jnp.int32, (tc, tc), 0)
    col = lax.broadcasted_iota(jnp.int32, (tc, tc), 1)
    tri = jnp.where((col <= row) & (col // A_CHUNK == row // A_CHUNK), 1.0, 0.0).astype(BF16)
    hi = lf.astype(BF16)
    r1 = lf - hi.astype(F32)
    mid = r1.astype(BF16)
    lo = (r1 - mid.astype(F32)).astype(BF16)
    cum = (jnp.dot(tri, hi, preferred_element_type=F32)
           + jnp.dot(tri, mid, preferred_element_type=F32)
           + jnp.dot(tri, lo, preferred_element_type=F32))

    out_g = og_ref[...]
    lane = lax.broadcasted_iota(jnp.int32, (A_SUB, A_HEAD_DIM), 1)
    srow = lax.broadcasted_iota(jnp.int32, (A_SUB, A_HEAD_DIM), 0)
    crow = lax.broadcasted_iota(jnp.int32, (A_CHUNK, A_HEAD_DIM), 0)
    nsub = A_CHUNK // A_SUB

    for ci in range(tc // A_CHUNK):
        rows = slice(ci * A_CHUNK, (ci + 1) * A_CHUNK)
        cumc = cum[rows]
        qc = q_ref[rows, :].astype(F32)
        kc = k_ref[rows, :].astype(F32)
        vc = v_ref[rows, :]
        last = cumc[A_CHUNK - 1:A_CHUNK]
        st = st_ref[...]
        qe = (qc * jnp.exp(cumc)).astype(BF16)
        o_inter = lax.dot_general(qe, st.astype(BF16), (((1,), (1,)), ((), ())),
                                  preferred_element_type=F32)
        kd = (kc * jnp.exp(last - cumc)).astype(BF16)
        upd = lax.dot_general(vc, kd, (((0,), (0,)), ((), ())), preferred_element_type=F32)
        st_ref[...] = st * jnp.exp(last) + upd

        p_rows = []
        for si in range(nsub):
            sr = slice(si * A_SUB, (si + 1) * A_SUB)
            cs, qs, ks = cumc[sr], qc[sr], kc[sr]
            p = jnp.zeros((A_SUB, A_HEAD_DIM), F32)
            if si > 0:
                ref = cumc[si * A_SUB - 1:si * A_SUB]
                q_t = (qs * jnp.exp(cs - ref)).astype(BF16)
                k_t = jnp.where(crow < si * A_SUB,
                                kc * jnp.exp(jnp.minimum(ref - cumc, 0.0)), 0.0).astype(BF16)
                off = lax.dot_general(q_t, k_t, (((1,), (1,)), ((), ())),
                                      preferred_element_type=F32)
                p = jnp.concatenate(
                    [off, jnp.zeros((A_SUB, A_HEAD_DIM - A_CHUNK), F32)], axis=1)
            for sj in range(A_SUB):
                decay = jnp.exp(jnp.minimum(cs - cs[sj:sj + 1], 0.0))
                colv = jnp.sum(qs * (ks[sj:sj + 1] * decay), axis=1, keepdims=True)
                p = jnp.where((lane == si * A_SUB + sj) & (srow >= sj), colv, p)
            p_rows.append(p[:, :A_CHUNK])
        scores = jnp.concatenate(p_rows, axis=0).astype(BF16)
        o = o_inter + jnp.dot(scores, vc, preferred_element_type=F32)
        o = o * lax.rsqrt(jnp.mean(o * o, axis=-1, keepdims=True) + RMS_EPS)
        o_ref[rows, :] = (o * out_g * g_ref[rows, :].astype(F32)).astype(BF16)


def _hgrn_core(q, k, v, log_f, g, out_g, *, tc=256):
    s, d = q.shape
    tc = min(tc, s)
    assert s % tc == 0 and tc % A_CHUNK == 0 and d % A_HEAD_DIM == 0
    blk = pl.BlockSpec((tc, A_HEAD_DIM), lambda h, c: (c, h))
    vmem = 2 * (4 * _nbytes((tc, A_HEAD_DIM), BF16) + 2 * _nbytes((tc, A_HEAD_DIM), F32)) \
        + 8 * _nbytes((tc, tc), F32)
    return pl.pallas_call(
        functools.partial(_hgrn_body, tc=tc),
        grid=(d // A_HEAD_DIM, s // tc),
        in_specs=[blk, blk, blk, blk, blk, pl.BlockSpec((1, A_HEAD_DIM), lambda h, c: (0, h))],
        out_specs=blk,
        out_shape=jax.ShapeDtypeStruct((s, d), BF16),
        scratch_shapes=[pltpu.VMEM((A_HEAD_DIM, A_HEAD_DIM), F32)],
        compiler_params=pltpu.CompilerParams(dimension_semantics=("parallel", "arbitrary"),
                                             vmem_limit_bytes=_vmem_limit(vmem)),
        name="hgrn_core",
    )(q, k, v, log_f, g, out_g.reshape(1, d))


def _residual_matmul(hb, w, resid, alpha, *, name, tm=1024, tn=1024, tk=4096):
    s, d = resid.shape
    blk, imap = _tile(min(tm, s), min(tn, d))

    def ep(accs, extras, outs):
        outs[0][...] = alpha * extras[0][...] + accs[0]

    (y,) = _fused_matmul(hb, [(w, 0)], d, [(resid, blk, imap)], [((s, d), F32, blk, imap)], ep,
                         tm=tm, tn=tn, tk=tk, name=name)
    return y


def _rel_bucket_band():
    i = np.arange(B_BLOCK)[:, None]
    j = np.arange(2 * B_BLOCK)[None, :]
    d = np.clip(B_BLOCK + i - j, 0, None)
    max_exact = REL_BUCKETS // 2
    large = max_exact + (np.log(np.maximum(d, 1) / max_exact)
                         / np.log(REL_MAX_DISTANCE / max_exact)
                         * (REL_BUCKETS - max_exact)).astype(np.int32)
    large = np.minimum(large, REL_BUCKETS - 1)
    return np.where(d < max_exact, d, large).astype(np.int32)


def _bias_body(rb_ref, bucket_ref, o_ref, *, heads_per_step):
    h0 = pl.program_id(0) * heads_per_step
    bucket = bucket_ref[...]
    for hh in range(heads_per_step):
        acc = jnp.zeros(bucket.shape, F32)
        for b in range(REL_BUCKETS):
            acc = jnp.where(bucket == b, rb_ref[b, h0 + hh], acc)
        o_ref[hh] = acc


def _rel_bias_band(rel_bias, *, heads_per_step=8):
    nh = rel_bias.shape[1]
    bucket = jnp.asarray(_rel_bucket_band())
    return pl.pallas_call(
        functools.partial(_bias_body, heads_per_step=heads_per_step),
        grid=(nh // heads_per_step,),
        in_specs=[pl.BlockSpec(memory_space=pltpu.SMEM),
                  pl.BlockSpec((B_BLOCK, 2 * B_BLOCK), lambda i: (0, 0))],
        out_specs=pl.BlockSpec((heads_per_step, B_BLOCK, 2 * B_BLOCK), lambda i: (i, 0, 0)),
        out_shape=jax.ShapeDtypeStruct((nh, B_BLOCK, 2 * B_BLOCK), F32),
        compiler_params=pltpu.CompilerParams(dimension_semantics=("parallel",)),
        name="rel_bias_band",
    )(rel_bias, bucket)


def _head_split_matmul(xb, w, n_cols, *, name, tm=1024, tn=1024):
    s, kdim = xb.shape
    tm, tn = min(tm, s), min(tn, n_cols)
    hpt = tn // B_HEAD_DIM

    def ep(accs, extras, outs):
        for c in range(hpt):
            outs[0][c] = accs[0][:, c * B_HEAD_DIM:(c + 1) * B_HEAD_DIM].astype(BF16)

    outs = [((n_cols // B_HEAD_DIM, s, B_HEAD_DIM), BF16, (hpt, tm, B_HEAD_DIM),
             lambda i, j: (j, i, 0))]
    (out,) = _fused_matmul(xb, [(w, 0)], n_cols, [], outs, ep, tm=tm, tn=tn, tk=kdim, name=name)
    return out


def _attn_body(sink_ref, q_ref, kp_ref, kc_ref, vp_ref, vc_ref, bias_ref, o_ref, *, group):
    h = pl.program_id(0)
    n = pl.program_id(1)
    c = B_BLOCK
    q = q_ref[...].reshape(group * c, B_HEAD_DIM)
    kb = jnp.concatenate([kp_ref[0], kc_ref[0]], axis=0)
    vb = jnp.concatenate([vp_ref[0], vc_ref[0]], axis=0)
    logits = lax.dot_general(q, kb, (((1,), (1,)), ((), ())), preferred_element_type=F32)
    logits = logits * (B_HEAD_DIM ** -0.5)
    qi = lax.broadcasted_iota(jnp.int32, (c, 2 * c), 0)
    kj = lax.broadcasted_iota(jnp.int32, (c, 2 * c), 1)
    dist = c + qi - kj
    mask = (dist >= 0) & (dist < WINDOW) & ((n > 0) | (kj >= c))
    probs = []
    for gi in range(group):
        lg = jnp.where(mask, logits[gi * c:(gi + 1) * c] + bias_ref[gi], -jnp.inf)
        sink = sink_ref[h * group + gi]
        m = jnp.maximum(jnp.max(lg, axis=-1, keepdims=True), sink)
        e = jnp.exp(lg - m)
        denom = jnp.sum(e, axis=-1, keepdims=True) + jnp.exp(sink - m)
        probs.append((e / denom).astype(BF16))
    pv = jnp.dot(jnp.concatenate(probs, axis=0), vb, preferred_element_type=F32)
    o_ref[...] = jnp.concatenate([pv[gi * c:(gi + 1) * c] for gi in range(group)],
                                 axis=1).astype(BF16)


def _swa_core(qh, kvh, sinks, bias, n_kv):
    nq, s, hd = qh.shape
    group = nq // n_kv
    nb = s // B_BLOCK
    prev = lambda n: jnp.maximum(n - 1, 0)
    kv_blk = (1, B_BLOCK, hd)
    in_specs = [
        pl.BlockSpec(memory_space=pltpu.SMEM),
        pl.BlockSpec((group, B_BLOCK, hd), lambda h, n: (h, n, 0)),
        pl.BlockSpec(kv_blk, lambda h, n: (h, prev(n), 0)),
        pl.BlockSpec(kv_blk, lambda h, n: (h, n, 0)),
        pl.BlockSpec(kv_blk, lambda h, n: (h + n_kv, prev(n), 0)),
        pl.BlockSpec(kv_blk, lambda h, n: (h + n_kv, n, 0)),
        pl.BlockSpec((group, B_BLOCK, 2 * B_BLOCK), lambda h, n: (h, 0, 0)),
    ]
    vmem = 2 * (_nbytes((group, B_BLOCK, 2 * B_BLOCK), F32) + 6 * _nbytes((group, B_BLOCK, V7X_LANES), BF16)) \
        + 6 * _nbytes((group * B_BLOCK, 2 * B_BLOCK), F32)
    return pl.pallas_call(
        functools.partial(_attn_body, group=group),
        grid=(n_kv, nb),
        in_specs=in_specs,
        out_specs=pl.BlockSpec((B_BLOCK, group * hd), lambda h, n: (n, h)),
        out_shape=jax.ShapeDtypeStruct((s, nq * hd), BF16),
        compiler_params=pltpu.CompilerParams(dimension_semantics=("parallel", "parallel"),
                                             vmem_limit_bytes=_vmem_limit(vmem)),
        name="swa_core",
    )(sinks, qh, kvh, kvh, kvh, kvh, bias)


def _swiglu_up(xb, w1, w3, *, tm=1024, tn=512):
    s, d = xb.shape
    f = w1.shape[1]
    blk, imap = _tile(min(tm, s), min(tn, f))

    def ep(accs, extras, outs):
        outs[0][...] = (_silu(accs[0]) * accs[1]).astype(BF16)

    (h,) = _fused_matmul(xb, [(w1, 0), (w3, 0)], f, [], [((s, f), BF16, blk, imap)], ep,
                         tm=tm, tn=tn, tk=d, name="swiglu_up")
    return h


def _router_body(x_ref, w_ref, gates_ref, idx_ref):
    logits = jnp.dot(x_ref[...], w_ref[...], preferred_element_type=F32,
                     precision=lax.Precision.HIGHEST)
    ne = logits.shape[1]
    eid = lax.broadcasted_iota(jnp.int32, logits.shape, 1)
    m1 = jnp.max(logits, axis=-1, keepdims=True)
    i1 = jnp.min(jnp.where(logits == m1, eid, ne), axis=-1, keepdims=True)
    rest = jnp.where(eid == i1, -jnp.inf, logits)
    m2 = jnp.max(rest, axis=-1, keepdims=True)
    i2 = jnp.min(jnp.where(rest == m2, eid, ne), axis=-1, keepdims=True)
    e2 = jnp.exp(m2 - m1)
    w_top1 = 1.0 / (1.0 + e2)
    w_top2 = e2 / (1.0 + e2)
    gates_ref[...] = jnp.where(eid == i1, w_top1, 0.0) + jnp.where(eid == i2, w_top2, 0.0)
    idx_ref[...] = jnp.where(eid == 0, i1, i2)[:, :TOP_K]


def _router(x, w_router, *, tr=512):
    s, d = x.shape
    ne = w_router.shape[1]
    tr = min(tr, s)
    vmem = 2 * _nbytes((tr, d), F32) * 4 + 2 * _nbytes((d, V7X_LANES), F32)
    return pl.pallas_call(
        _router_body,
        grid=(s // tr,),
        in_specs=[pl.BlockSpec((tr, d), lambda i: (i, 0)), pl.BlockSpec((d, ne), lambda i: (0, 0))],
        out_specs=[pl.BlockSpec((tr, ne), lambda i: (i, 0)), pl.BlockSpec((tr, TOP_K), lambda i: (i, 0))],
        out_shape=[jax.ShapeDtypeStruct((s, ne), F32), jax.ShapeDtypeStruct((s, TOP_K), jnp.int32)],
        compiler_params=pltpu.CompilerParams(dimension_semantics=("parallel",),
                                             vmem_limit_bytes=_vmem_limit(vmem)),
        name="moe_router",
    )(x, w_router)


def _routing_tables(gates, idx, tm):
    s, ne = gates.shape
    n_assign = s * TOP_K
    ntiles = n_assign // tm + ne
    e_flat = idx.reshape(n_assign)
    onehot = (e_flat[:, None] == jnp.arange(ne, dtype=jnp.int32)[None, :]).astype(jnp.int32)
    rank = jnp.take_along_axis(jnp.cumsum(onehot, axis=0) - onehot, e_flat[:, None], axis=1)[:, 0]
    counts = jnp.sum(onehot, axis=0)
    padded = ((counts + tm - 1) // tm) * tm
    ends = jnp.cumsum(padded)
    pos = (ends - padded)[e_flat] + rank
    token = jnp.arange(n_assign, dtype=jnp.int32) // TOP_K
    gate = jnp.take_along_axis(gates, idx, axis=1).reshape(n_assign)
    row_token = jnp.zeros((ntiles * tm,), jnp.int32).at[pos].set(token)
    row_gate = jnp.zeros((ntiles * tm,), F32).at[pos].set(gate)
    tile_start = jnp.arange(ntiles, dtype=jnp.int32) * tm
    tile_expert = jnp.minimum(jnp.searchsorted(ends, tile_start, side="right"), ne - 1).astype(jnp.int32)
    tile_valid = (tile_start < ends[-1]).astype(jnp.int32)
    return row_token, row_gate, pos.reshape(s, TOP_K).astype(jnp.int32), tile_expert, tile_valid, ntiles


def _gather_body(tok_ref, x_hbm, o_ref, buf, sem, *, tm, ntiles):
    i = pl.program_id(0)

    def row_copy(tile_tok_ref, slot, r):
        return pltpu.make_async_copy(x_hbm.at[pl.ds(tile_tok_ref[0, 0, r], 1)],
                                     buf.at[slot, pl.ds(r, 1)], sem.at[slot])

    def start_tile(slot):
        def body(r, carry):
            row_copy(tok_ref, slot, r).start()
            return carry
        lax.fori_loop(0, tm, body, 0)

    slot = i % 2
    start_tile(slot)

    def wait_body(r, carry):
        row_copy(tok_ref, slot, r).wait()
        return carry
    lax.fori_loop(0, tm, wait_body, 0)
    o_ref[...] = buf[slot].astype(BF16)


def _moe_gather(x, row_token, ntiles, tm):
    s, d = x.shape
    vmem = 2 * _nbytes((tm, d), F32) + 2 * _nbytes((tm, d), BF16) + _nbytes((tm, d), F32)
    return pl.pallas_call(
        functools.partial(_gather_body, tm=tm, ntiles=ntiles),
        grid=(ntiles,),
        in_specs=[pl.BlockSpec((1, 1, tm), lambda i: (i, 0, 0), memory_space=pltpu.SMEM),
                  pl.BlockSpec(memory_space=pl.ANY)],
        out_specs=pl.BlockSpec((tm, d), lambda i: (i, 0)),
        out_shape=jax.ShapeDtypeStruct((ntiles * tm, d), BF16),
        scratch_shapes=[pltpu.VMEM((2, tm, d), F32), pltpu.SemaphoreType.DMA((2,))],
        compiler_params=pltpu.CompilerParams(dimension_semantics=("arbitrary",),
                                             vmem_limit_bytes=_vmem_limit(vmem)),
        name="moe_gather",
    )(row_token.reshape(ntiles, 1, tm), x)


def _moe_up_body(te_ref, tv_ref, x_ref, w1_ref, w3_ref, gate_ref, o_ref):
    i = pl.program_id(1)

    @pl.when(tv_ref[i] > 0)
    def _():
        xv = x_ref[...]
        a = jnp.dot(xv, w1_ref[...], preferred_element_type=F32)
        b = jnp.dot(xv, w3_ref[...], preferred_element_type=F32)
        o_ref[...] = (_silu(a) * b * gate_ref[...]).astype(BF16)

    @pl.when(tv_ref[i] == 0)
    def _():
        o_ref[...] = jnp.zeros_like(o_ref)


def _moe_up(xg, w1, w3, row_gate, tile_expert, tile_valid, tm, *, tf=1024):
    rows, d = xg.shape
    f = w1.shape[2]
    tf = min(tf, f)
    ntiles = rows // tm
    wspec = pl.BlockSpec((None, d, tf), lambda j, i, te, tv: (te[i], 0, j))
    vmem = 2 * (_nbytes((tm, d), BF16) + 2 * _nbytes((d, tf), BF16) + _nbytes((tm, tf), BF16)) \
        + 4 * _nbytes((tm, tf), F32)
    grid_spec = pltpu.PrefetchScalarGridSpec(
        num_scalar_prefetch=2,
        grid=(f // tf, ntiles),
        in_specs=[pl.BlockSpec((tm, d), lambda j, i, te, tv: (i, 0)), wspec, wspec,
                  pl.BlockSpec((tm, 1), lambda j, i, te, tv: (i, 0))],
        out_specs=pl.BlockSpec((tm, tf), lambda j, i, te, tv: (i, j)),
    )
    return pl.pallas_call(
        _moe_up_body,
        grid_spec=grid_spec,
        out_shape=jax.ShapeDtypeStruct((rows, f), BF16),
        compiler_params=pltpu.CompilerParams(dimension_semantics=("parallel", "arbitrary"),
                                             vmem_limit_bytes=_vmem_limit(vmem)),
        name="moe_up",
    )(tile_expert, tile_valid, xg, w1, w3, row_gate.reshape(rows, 1))


def _moe_down_body(te_ref, tv_ref, h_ref, w_ref, o_ref):
    i = pl.program_id(1)

    @pl.when(tv_ref[i] > 0)
    def _():
        o_ref[...] = jnp.dot(h_ref[...], w_ref[...], preferred_element_type=F32)

    @pl.when(tv_ref[i] == 0)
    def _():
        o_ref[...] = jnp.zeros_like(o_ref)


def _moe_down(h, w2, tile_expert, tile_valid, tm, *, tn=1024):
    rows, f = h.shape
    d = w2.shape[2]
    tn = min(tn, d)
    ntiles = rows // tm
    vmem = 2 * (_nbytes((tm, f), BF16) + _nbytes((f, tn), BF16) + _nbytes((tm, tn), F32)) \
        + 2 * _nbytes((tm, tn), F32)
    grid_spec = pltpu.PrefetchScalarGridSpec(
        num_scalar_prefetch=2,
        grid=(d // tn, ntiles),
        in_specs=[pl.BlockSpec((tm, f), lambda j, i, te, tv: (i, 0)),
                  pl.BlockSpec((None, f, tn), lambda j, i, te, tv: (te[i], 0, j))],
        out_specs=pl.BlockSpec((tm, tn), lambda j, i, te, tv: (i, j)),
    )
    return pl.pallas_call(
        _moe_down_body,
        grid_spec=grid_spec,
        out_shape=jax.ShapeDtypeStruct((rows, d), F32),
        compiler_params=pltpu.CompilerParams(dimension_semantics=("parallel", "arbitrary"),
                                             vmem_limit_bytes=_vmem_limit(vmem)),
        name="moe_down",
    )(tile_expert, tile_valid, h, w2)


def _combine_body(pos_ref, y_hbm, x_ref, g_ref, b_ref, o32_ref, o16_ref, buf, sem, *, tr, alpha):
    def row_copy(r, kk):
        return pltpu.make_async_copy(y_hbm.at[pl.ds(pos_ref[0, kk, r], 1)],
                                     buf.at[kk, pl.ds(r, 1)], sem.at[kk])

    def start_body(r, carry):
        for kk in range(TOP_K):
            row_copy(r, kk).start()
        return carry
    lax.fori_loop(0, tr, start_body, 0)

    def wait_body(r, carry):
        for kk in range(TOP_K):
            row_copy(r, kk).wait()
        return carry
    lax.fori_loop(0, tr, wait_body, 0)

    f = buf[0]
    for kk in range(1, TOP_K):
        f = f + buf[kk]
    out = _ln_rows(alpha * x_ref[...] + f, g_ref[...], b_ref[...])
    o32_ref[...] = out
    o16_ref[...] = out.astype(BF16)


def _moe_combine_ln(y, pos, x, g, b, alpha, *, tr=256):
    s, d = x.shape
    tr = min(tr, s)
    nt = s // tr
    pos_t = pos.reshape(nt, tr, TOP_K).transpose(0, 2, 1)
    row = pl.BlockSpec((tr, d), lambda i: (i, 0))
    vec = pl.BlockSpec((1, d), lambda i: (0, 0))
    vmem = TOP_K * _nbytes((tr, d), F32) + 2 * (2 * _nbytes((tr, d), F32) + _nbytes((tr, d), BF16)) \
        + 3 * _nbytes((tr, d), F32)
    return pl.pallas_call(
        functools.partial(_combine_body, tr=tr, alpha=alpha),
        grid=(nt,),
        in_specs=[pl.BlockSpec((1, TOP_K, tr), lambda i: (i, 0, 0), memory_space=pltpu.SMEM),
                  pl.BlockSpec(memory_space=pl.ANY), row, vec, vec],
        out_specs=[row, row],
        out_shape=[jax.ShapeDtypeStruct((s, d), F32), jax.ShapeDtypeStruct((s, d), BF16)],
        scratch_shapes=[pltpu.VMEM((TOP_K, tr, d), F32), pltpu.SemaphoreType.DMA((TOP_K,))],
        compiler_params=pltpu.CompilerParams(dimension_semantics=("arbitrary",),
                                             vmem_limit_bytes=_vmem_limit(vmem)),
        name="moe_combine_ln",
    )(pos_t, y, x, g.reshape(1, d), b.reshape(1, d))


def _ple(xb, x, pb, w_gate, w_proj, *, tm=1024, tn=512):
    s, d = x.shape
    pdim = pb.shape[1]
    blk, imap = _tile(min(tm, s), min(tn, d))

    def ep(accs, extras, outs):
        proj = jnp.dot(extras[1][...], extras[2][...], preferred_element_type=F32)
        out = extras[0][...] + jax.nn.sigmoid(accs[0]) * proj
        outs[0][...] = out
        outs[1][...] = out.astype(BF16)

    extras = [(x, blk, imap),
              (pb, (blk[0], pdim), lambda i, j: (i, 0)),
              (w_proj, (pdim, blk[1]), lambda i, j: (0, j))]
    outs = [((s, d), F32, blk, imap), ((s, d), BF16, blk, imap)]
    return _fused_matmul(xb, [(w_gate, 0)], d, extras, outs, ep, tm=tm, tn=tn, tk=d, name="ple")


MOE_TILE_ROWS = 256


def kernel(x, p, ln1_g, ln1_b, ln2_g, ln2_b, a_w_in, a_lb_logits, a_out_g, a_w_o, kv_w, b_w_q, b_sinks, b_w_o, rel_bias, ffn_w1, ffn_w3, ffn_w2, moe_router, moe_w1, moe_w3, moe_w2, ple_w_proj, ple_w_gate):
    batch, s, d = x.shape
    depth = p.shape[0]
    n_a = a_w_in.shape[0]
    kv_dim = kv_w.shape[1] // 2
    n_kv = kv_dim // B_HEAD_DIM
    alpha = (2.0 * depth) ** 0.25
    moe_tm = min(MOE_TILE_ROWS, s)
    bias_band = _rel_bias_band(rel_bias.astype(F32))
    lb_logits = a_lb_logits.astype(F32)

    outs = []
    for bi in range(batch):
        xf = x[bi].astype(F32)
        xb = xf.astype(BF16)
        kvh = None
        for i in range(depth):
            if i < n_a:
                q, k, v, log_f, g = _hgrn_in_proj(xb, a_w_in[i].astype(BF16), lb_logits, i)
                hb = _hgrn_core(q, k, v, log_f, g, a_out_g[i].astype(F32))
                w_o = a_w_o[i].astype(BF16)
            else:
                j = i - n_a
                qh = _head_split_matmul(xb, b_w_q[j].astype(BF16), d, name="swa_q")
                hb = _swa_core(qh, kvh, b_sinks[j].astype(F32), bias_band, n_kv)
                w_o = b_w_o[j].astype(BF16)
            y = _residual_matmul(hb, w_o, xf, alpha, name="mixer_out")
            xf, xb = _layer_norm(y, ln1_g[i].astype(F32), ln1_b[i].astype(F32))
            if i % 2 == 0:
                li = i // 2
                hmid = _swiglu_up(xb, ffn_w1[li].astype(BF16), ffn_w3[li].astype(BF16))
                y = _residual_matmul(hmid, ffn_w2[li].astype(BF16), xf, alpha, name="ffn_down",
                                     tk=2048)
                xf, xb = _layer_norm(y, ln2_g[i].astype(F32), ln2_b[i].astype(F32))
            else:
                li = i // 2
                gates, idx = _router(xf, moe_router[li].astype(F32))
                row_token, row_gate, pos, tile_expert, tile_valid, ntiles = _routing_tables(
                    gates, idx, moe_tm)
                xg = _moe_gather(xf, row_token, ntiles, moe_tm)
                hmid = _moe_up(xg, moe_w1[li].astype(BF16), moe_w3[li].astype(BF16), row_gate,
                               tile_expert, tile_valid, moe_tm)
                yg = _moe_down(hmid, moe_w2[li].astype(BF16), tile_expert, tile_valid, moe_tm)
                xf, xb = _moe_combine_ln(yg, pos, xf, ln2_g[i].astype(F32), ln2_b[i].astype(F32),
                                         alpha)
            xf, xb = _ple(xb, xf, p[i, bi].astype(BF16), ple_w_gate[i].astype(BF16),
                          ple_w_proj[i].astype(BF16))
            if i == n_a - 1:
                kvh = _head_split_matmul(xb, kv_w.astype(BF16), 2 * kv_dim, name="kv_proj")
        outs.append(xf)
    return jnp.stack(outs, axis=0).astype(x.dtype)
```

```python
import functools

import numpy as np
import jax
import jax.numpy as jnp
from jax import lax
from jax.experimental import pallas as pl
from jax.experimental.pallas import tpu as pltpu

F32 = jnp.float32
BF16 = jnp.bfloat16

V7X_LANES = 128
V7X_SCOPED_VMEM_BYTES = 60000 * 1024
COMPILER_SCRATCH_BYTES = 6 * 1024 * 1024

A_HEAD_DIM = 128
A_CHUNK = 64
A_SUB = 16
A_CUMSUM_ROWS = 256
B_HEAD_DIM = 64
B_BLOCK = 128
WINDOW = 128
REL_BUCKETS = 32
REL_MAX_DISTANCE = 128
TOP_K = 2
LN_EPS = 1e-5
RMS_EPS = 1e-6


def _vmem_limit(block_bytes):
    return int(min(block_bytes + COMPILER_SCRATCH_BYTES, V7X_SCOPED_VMEM_BYTES))


def _nbytes(shape, dtype):
    return int(np.prod([n for n in shape if n is not None])) * jnp.dtype(dtype).itemsize


def _silu(x):
    return x * jax.nn.sigmoid(x)


def _mm_body(*refs, nw, ne, no, nk, weight_stationary, epilogue):
    x_ref = refs[0]
    w_refs = refs[1:1 + nw]
    extra_refs = refs[1 + nw:1 + nw + ne]
    out_refs = refs[1 + nw + ne:1 + nw + ne + no]
    scratch = refs[1 + nw + ne + no:]
    xv = x_ref[...]
    if weight_stationary:
        @pl.when(pl.program_id(1) == 0)
        def _():
            for w_ref, wb in zip(w_refs, scratch):
                wb[...] = w_ref[...].astype(BF16)

        prods = [jnp.dot(xv, wb[...], preferred_element_type=F32) for wb in scratch]
        epilogue(prods, extra_refs, out_refs)
        return
    prods = [jnp.dot(xv, w_ref[...].astype(BF16), preferred_element_type=F32) for w_ref in w_refs]
    if nk == 1:
        epilogue(prods, extra_refs, out_refs)
        return
    k = pl.program_id(2)

    @pl.when(k == 0)
    def _():
        for acc, d in zip(scratch, prods):
            acc[...] = d

    @pl.when(k > 0)
    def _():
        for acc, d in zip(scratch, prods):
            acc[...] += d

    @pl.when(k == nk - 1)
    def _():
        epilogue([acc[...] for acc in scratch], extra_refs, out_refs)


def _fused_matmul(x, ws, n_cols, extras, outs, epilogue, *, tm, tn, tk, name,
                  weight_stationary=True):
    m, kdim = x.shape
    tm, tn, tk = min(tm, m), min(tn, n_cols), min(tk, kdim)
    assert m % tm == 0 and n_cols % tn == 0 and kdim % tk == 0
    nk = kdim // tk
    if weight_stationary:
        assert nk == 1
        grid = (n_cols // tn, m // tm)
        ijk = lambda a, b: (b, a, 0)
        semantics = ("parallel", "arbitrary")
    else:
        grid = (m // tm, n_cols // tn, nk)
        ijk = lambda a, b, c: (a, b, c)
        semantics = ("parallel", "parallel", "arbitrary")

    def on_ij(imap):
        return lambda *g: imap(*ijk(*g)[:2])

    in_specs = [pl.BlockSpec((tm, tk), lambda *g: (ijk(*g)[0], ijk(*g)[2]))]
    operands = [x]
    vmem = 2 * _nbytes((tm, tk), x.dtype)
    for w, layer, off in ws:
        if layer is None:
            spec = pl.BlockSpec((tk, tn), lambda *g, off=off: (ijk(*g)[2], ijk(*g)[1] + off))
        else:
            spec = pl.BlockSpec((None, tk, tn), lambda *g, off=off, layer=layer:
                                (layer, ijk(*g)[2], ijk(*g)[1] + off))
        in_specs.append(spec)
        operands.append(w)
        vmem += 2 * _nbytes((tk, tn), w.dtype) + _nbytes((tk, tn), BF16)
    for arr, bshape, imap in extras:
        in_specs.append(pl.BlockSpec(bshape, on_ij(imap)))
        operands.append(arr)
        vmem += 2 * _nbytes(bshape, arr.dtype)
    out_shapes, out_specs = [], []
    for shape, dtype, bshape, imap in outs:
        out_shapes.append(jax.ShapeDtypeStruct(shape, dtype))
        out_specs.append(pl.BlockSpec(bshape, on_ij(imap)))
        vmem += 2 * _nbytes(bshape, dtype)
    if weight_stationary:
        scratch = [pltpu.VMEM((tk, tn), BF16) for _ in ws]
    else:
        scratch = [pltpu.VMEM((tm, tn), F32) for _ in ws] if nk > 1 else []
        vmem += len(scratch) * _nbytes((tm, tn), F32)
    vmem += (1 + len(ws)) * _nbytes((tm, tn), F32)
    body = functools.partial(_mm_body, nw=len(ws), ne=len(extras), no=len(outs), nk=nk,
                             weight_stationary=weight_stationary, epilogue=epilogue)
    return pl.pallas_call(
        body,
        grid=grid,
        in_specs=in_specs,
        out_specs=out_specs,
        out_shape=out_shapes,
        scratch_shapes=scratch,
        compiler_params=pltpu.CompilerParams(dimension_semantics=semantics,
                                             vmem_limit_bytes=_vmem_limit(vmem)),
        name=name,
    )(*operands)


def _tile(tm, tn):
    return (tm, tn), (lambda i, j: (i, j))


def _ln_rows(y, g, b):
    mu = jnp.mean(y, axis=-1, keepdims=True)
    yc = y - mu
    var = jnp.mean(yc * yc, axis=-1, keepdims=True)
    return yc * lax.rsqrt(var + LN_EPS) * g + b


def _ln_body(y_ref, g_ref, b_ref, o32_ref, o16_ref):
    out = _ln_rows(y_ref[...], g_ref[...], b_ref[...])
    o32_ref[...] = out
    o16_ref[...] = out.astype(BF16)


def _layer_norm(y, g, b, *, tr=256):
    s, d = y.shape
    tr = min(tr, s)
    assert s % tr == 0
    row = pl.BlockSpec((tr, d), lambda i: (i, 0))
    vec = pl.BlockSpec((1, d), lambda i: (0, 0))
    vmem = 2 * (2 * _nbytes((tr, d), F32) + _nbytes((tr, d), BF16)) + 2 * _nbytes((tr, d), F32)
    return pl.pallas_call(
        _ln_body,
        grid=(s // tr,),
        in_specs=[row, vec, vec],
        out_specs=[row, row],
        out_shape=[jax.ShapeDtypeStruct((s, d), F32), jax.ShapeDtypeStruct((s, d), BF16)],
        compiler_params=pltpu.CompilerParams(dimension_semantics=("parallel",),
                                             vmem_limit_bytes=_vmem_limit(vmem)),
        name="layer_norm",
    )(y, g.reshape(1, d), b.reshape(1, d))


def _lower_bound(lbl, layer):
    mx = jnp.max(lbl, axis=0, keepdims=True)
    e = jnp.exp(lbl - mx)
    return jnp.sum(e[:layer + 1], axis=0, keepdims=True) / jnp.sum(e, axis=0, keepdims=True)


def _hgrn_in_proj(xb, w_in, lb_logits, layer, *, tm=1024, tn=512):
    s, d = xb.shape
    nseg = d // min(tn, d)
    blk, imap = _tile(min(tm, s), min(tn, d))

    def ep_silu(accs, extras, outs):
        outs[0][...] = _silu(accs[0]).astype(BF16)

    def ep_id(accs, extras, outs):
        outs[0][...] = accs[0].astype(BF16)

    def ep_forget(accs, extras, outs):
        lb = _lower_bound(extras[0][...], layer)
        f_raw = accs[0]
        e = jnp.exp(-jnp.abs(f_raw))
        log_sig = jnp.minimum(f_raw, 0.0) - jnp.log1p(e)
        sig_neg = jnp.where(f_raw >= 0.0, e, 1.0) / (1.0 + e)
        a = jnp.log(lb)
        b = jnp.log1p(-lb) + log_sig
        outs[0][...] = jnp.maximum(a, b) + jnp.log1p(jnp.exp(-jnp.abs(a - b)))
        outs[1][...] = ((1.0 - lb) * sig_neg).astype(BF16)

    def call(seg, epilogue, extras, out_dtypes, name):
        outs = [((s, d), dt, blk, imap) for dt in out_dtypes]
        return _fused_matmul(xb, [(w_in, layer, seg * nseg)], d, extras, outs, epilogue,
                             tm=tm, tn=tn, tk=d, name=name)

    nslot = lb_logits.shape[0]
    lb_extra = [(lb_logits, (nslot, blk[1]), lambda i, j: (0, j))]
    (q,) = call(0, ep_silu, [], [BF16], "hgrn_q")
    log_f, k = call(1, ep_forget, lb_extra, [F32, BF16], "hgrn_f")
    (v,) = call(2, ep_id, [], [BF16], "hgrn_v")
    (g,) = call(3, ep_silu, [], [BF16], "hgrn_g")
    return q, k, v, log_f, g


def _hgrn_body(q_ref, k_ref, v_ref, lf_ref, g_ref, og_ref, o_ref, st_ref, *, tc):
    @pl.when(pl.program_id(1) == 0)
    def _():
        st_ref[...] = jnp.zeros_like(st_ref)

    tg = min(tc, A_CUMSUM_ROWS)
    row = lax.broadcasted_iota(jnp.int32, (tg, tg), 0)
    col = lax.broadcasted_iota(jnp.int32, (tg, tg), 1)
    tri = jnp.where((col <= row) & (col // A_CHUNK == row // A_CHUNK), 1.0, 0.0).astype(BF16)
    cums = []
    for gi in range(tc // tg):
        lf = lf_ref[gi * tg:(gi + 1) * tg, :]
        hi = lf.astype(BF16)
        r1 = lf - hi.astype(F32)
        mid = r1.astype(BF16)
        lo = (r1 - mid.astype(F32)).astype(BF16)
        cums.append(jnp.dot(tri, hi, preferred_element_type=F32)
                    + jnp.dot(tri, mid, preferred_element_type=F32)
                    + jnp.dot(tri, lo, preferred_element_type=F32))
    cum = jnp.concatenate(cums, axis=0)

    out_g = og_ref[...]
    lane = lax.broadcasted_iota(jnp.int32, (A_SUB, A_HEAD_DIM), 1)
    srow = lax.broadcasted_iota(jnp.int32, (A_SUB, A_HEAD_DIM), 0)
    crow = lax.broadcasted_iota(jnp.int32, (A_CHUNK, A_HEAD_DIM), 0)
    nsub = A_CHUNK // A_SUB

    for ci in range(tc // A_CHUNK):
        rows = slice(ci * A_CHUNK, (ci + 1) * A_CHUNK)
        cumc = cum[rows]
        qc = q_ref[rows, :].astype(F32)
        kc = k_ref[rows, :].astype(F32)
        vc = v_ref[rows, :]
        last = cumc[A_CHUNK - 1:A_CHUNK]
        st = st_ref[...]
        qe = (qc * jnp.exp(cumc)).astype(BF16)
        o_inter = lax.dot_general(qe, st.astype(BF16), (((1,), (1,)), ((), ())),
                                  preferred_element_type=F32)
        kd = (kc * jnp.exp(last - cumc)).astype(BF16)
        upd = lax.dot_general(vc, kd, (((0,), (0,)), ((), ())), preferred_element_type=F32)
        st_ref[...] = st * jnp.exp(last) + upd

        p_rows = []
        for si in range(nsub):
            sr = slice(si * A_SUB, (si + 1) * A_SUB)
            cs, qs, ks = cumc[sr], qc[sr], kc[sr]
            p = jnp.zeros((A_SUB, A_HEAD_DIM), F32)
            if si > 0:
                ref = cumc[si * A_SUB - 1:si * A_SUB]
                q_t = (qs * jnp.exp(cs - ref)).astype(BF16)
                k_t = jnp.where(crow < si * A_SUB,
                                kc * jnp.exp(jnp.minimum(ref - cumc, 0.0)), 0.0).astype(BF16)
                off = lax.dot_general(q_t, k_t, (((1,), (1,)), ((), ())),
                                      preferred_element_type=F32)
                p = jnp.concatenate(
                    [off, jnp.zeros((A_SUB, A_HEAD_DIM - A_CHUNK), F32)], axis=1)
            for sj in range(A_SUB):
                decay = jnp.exp(jnp.minimum(cs - cs[sj:sj + 1], 0.0))
                colv = jnp.sum(qs * (ks[sj:sj + 1] * decay), axis=1, keepdims=True)
                p = jnp.where((lane == si * A_SUB + sj) & (srow >= sj), colv, p)
            p_rows.append(p[:, :A_CHUNK])
        scores = jnp.concatenate(p_rows, axis=0).astype(BF16)
        o = o_inter + jnp.dot(scores, vc, preferred_element_type=F32)
        o = o * lax.rsqrt(jnp.mean(o * o, axis=-1, keepdims=True) + RMS_EPS)
        o_ref[rows, :] = (o * out_g * g_ref[rows, :].astype(F32)).astype(BF16)


def _hgrn_core(q, k, v, log_f, g, out_g, *, tc=512):
    s, d = q.shape
    tc = min(tc, s)
    assert s % tc == 0 and tc % A_CHUNK == 0 and d % A_HEAD_DIM == 0
    blk = pl.BlockSpec((tc, A_HEAD_DIM), lambda h, c: (c, h))
    vmem = 2 * (4 * _nbytes((tc, A_HEAD_DIM), BF16) + 2 * _nbytes((tc, A_HEAD_DIM), F32)) \
        + 8 * _nbytes((A_CUMSUM_ROWS, A_CUMSUM_ROWS), F32) + 8 * _nbytes((tc, A_HEAD_DIM), F32)
    return pl.pallas_call(
        functools.partial(_hgrn_body, tc=tc),
        grid=(d // A_HEAD_DIM, s // tc),
        in_specs=[blk, blk, blk, blk, blk, pl.BlockSpec((1, A_HEAD_DIM), lambda h, c: (0, h))],
        out_specs=blk,
        out_shape=jax.ShapeDtypeStruct((s, d), BF16),
        scratch_shapes=[pltpu.VMEM((A_HEAD_DIM, A_HEAD_DIM), F32)],
        compiler_params=pltpu.CompilerParams(dimension_semantics=("parallel", "arbitrary"),
                                             vmem_limit_bytes=_vmem_limit(vmem)),
        name="hgrn_core",
    )(q, k, v, log_f, g, out_g.reshape(1, d))


def _residual_matmul(hb, w, layer, resid, alpha, *, name, tm=1024, tn=512, tk=None):
    s, d = resid.shape
    kdim = hb.shape[1]
    blk, imap = _tile(min(tm, s), min(tn, d))

    def ep(accs, extras, outs):
        outs[0][...] = alpha * extras[0][...] + accs[0]

    (y,) = _fused_matmul(hb, [(w, layer, 0)], d, [(resid, blk, imap)],
                         [((s, d), F32, blk, imap)], ep, tm=tm, tn=tn, tk=tk or kdim, name=name,
                         weight_stationary=tk is None)
    return y


def _rel_bucket_band():
    i = np.arange(B_BLOCK)[:, None]
    j = np.arange(2 * B_BLOCK)[None, :]
    d = np.clip(B_BLOCK + i - j, 0, None)
    max_exact = REL_BUCKETS // 2
    large = max_exact + (np.log(np.maximum(d, 1) / max_exact)
                         / np.log(REL_MAX_DISTANCE / max_exact)
                         * (REL_BUCKETS - max_exact)).astype(np.int32)
    large = np.minimum(large, REL_BUCKETS - 1)
    return np.where(d < max_exact, d, large).astype(np.int32)


def _bias_body(rb_ref, bucket_ref, o_ref, *, heads_per_step):
    h0 = pl.program_id(0) * heads_per_step
    bucket = bucket_ref[...]
    for hh in range(heads_per_step):
        acc = jnp.zeros(bucket.shape, F32)
        for b in range(REL_BUCKETS):
            acc = jnp.where(bucket == b, rb_ref[b, h0 + hh], acc)
        o_ref[hh] = acc


def _rel_bias_band(rel_bias, *, heads_per_step=8):
    nh = rel_bias.shape[1]
    bucket = jnp.asarray(_rel_bucket_band())
    return pl.pallas_call(
        functools.partial(_bias_body, heads_per_step=heads_per_step),
        grid=(nh // heads_per_step,),
        in_specs=[pl.BlockSpec(memory_space=pltpu.SMEM),
                  pl.BlockSpec((B_BLOCK, 2 * B_BLOCK), lambda i: (0, 0))],
        out_specs=pl.BlockSpec((heads_per_step, B_BLOCK, 2 * B_BLOCK), lambda i: (i, 0, 0)),
        out_shape=jax.ShapeDtypeStruct((nh, B_BLOCK, 2 * B_BLOCK), F32),
        compiler_params=pltpu.CompilerParams(dimension_semantics=("parallel",)),
        name="rel_bias_band",
    )(rel_bias, bucket)


def _head_split_matmul(xb, w, layer, n_cols, *, name, tm=1024, tn=512):
    s, kdim = xb.shape
    tm, tn = min(tm, s), min(tn, n_cols)
    hpt = tn // B_HEAD_DIM

    def ep(accs, extras, outs):
        for c in range(hpt):
            outs[0][c] = accs[0][:, c * B_HEAD_DIM:(c + 1) * B_HEAD_DIM].astype(BF16)

    outs = [((n_cols // B_HEAD_DIM, s, B_HEAD_DIM), BF16, (hpt, tm, B_HEAD_DIM),
             lambda i, j: (j, i, 0))]
    (out,) = _fused_matmul(xb, [(w, layer, 0)], n_cols, [], outs, ep, tm=tm, tn=tn, tk=kdim,
                           name=name)
    return out


def _attn_body(sink_ref, q_ref, kp_ref, kc_ref, vp_ref, vc_ref, bias_ref, o_ref, *, group, qb):
    h = pl.program_id(0)
    m_blk = pl.program_id(1)
    c = B_BLOCK
    k_all = jnp.concatenate([kp_ref[0], kc_ref[0]], axis=0)
    v_all = jnp.concatenate([vp_ref[0], vc_ref[0]], axis=0)
    qi = lax.broadcasted_iota(jnp.int32, (c, 2 * c), 0)
    kj = lax.broadcasted_iota(jnp.int32, (c, 2 * c), 1)
    dist = c + qi - kj
    in_window = (dist >= 0) & (dist < WINDOW)
    for b in range(qb):
        q = q_ref[:, b * c:(b + 1) * c, :].reshape(group * c, B_HEAD_DIM)
        kb = k_all[b * c:(b + 2) * c]
        vb = v_all[b * c:(b + 2) * c]
        logits = lax.dot_general(q, kb, (((1,), (1,)), ((), ())), preferred_element_type=F32)
        logits = logits * (B_HEAD_DIM ** -0.5)
        mask = in_window & ((m_blk * qb + b > 0) | (kj >= c)) if b == 0 else in_window
        probs = []
        for gi in range(group):
            lg = jnp.where(mask, logits[gi * c:(gi + 1) * c] + bias_ref[gi], -jnp.inf)
            sink = sink_ref[h * group + gi]
            m = jnp.maximum(jnp.max(lg, axis=-1, keepdims=True), sink)
            e = jnp.exp(lg - m)
            denom = jnp.sum(e, axis=-1, keepdims=True) + jnp.exp(sink - m)
            probs.append((e / denom).astype(BF16))
        pv = jnp.dot(jnp.concatenate(probs, axis=0), vb, preferred_element_type=F32)
        o_ref[b * c:(b + 1) * c, :] = jnp.concatenate(
            [pv[gi * c:(gi + 1) * c] for gi in range(group)], axis=1).astype(BF16)


def _swa_core(qh, kvh, sinks, bias, n_kv, *, qb=4):
    nq, s, hd = qh.shape
    group = nq // n_kv
    nb = s // B_BLOCK
    qb = min(qb, nb)
    assert nb % qb == 0
    prev = lambda m: jnp.maximum(m * qb - 1, 0)
    prev_blk = (1, B_BLOCK, hd)
    cur_blk = (1, qb * B_BLOCK, hd)
    in_specs = [
        pl.BlockSpec(memory_space=pltpu.SMEM),
        pl.BlockSpec((group, qb * B_BLOCK, hd), lambda h, m: (h, m, 0)),
        pl.BlockSpec(prev_blk, lambda h, m: (h, prev(m), 0)),
        pl.BlockSpec(cur_blk, lambda h, m: (h, m, 0)),
        pl.BlockSpec(prev_blk, lambda h, m: (h + n_kv, prev(m), 0)),
        pl.BlockSpec(cur_blk, lambda h, m: (h + n_kv, m, 0)),
        pl.BlockSpec((group, B_BLOCK, 2 * B_BLOCK), lambda h, m: (h, 0, 0)),
    ]
    vmem = 2 * (_nbytes((group, B_BLOCK, 2 * B_BLOCK), F32)
                + (group + 3) * _nbytes((qb * B_BLOCK, V7X_LANES), BF16)
                + _nbytes((qb * B_BLOCK, group * hd), BF16)) \
        + 6 * qb * _nbytes((group * B_BLOCK, 2 * B_BLOCK), F32)
    return pl.pallas_call(
        functools.partial(_attn_body, group=group, qb=qb),
        grid=(n_kv, nb // qb),
        in_specs=in_specs,
        out_specs=pl.BlockSpec((qb * B_BLOCK, group * hd), lambda h, m: (m, h)),
        out_shape=jax.ShapeDtypeStruct((s, nq * hd), BF16),
        compiler_params=pltpu.CompilerParams(dimension_semantics=("parallel", "parallel"),
                                             vmem_limit_bytes=_vmem_limit(vmem)),
        name="swa_core",
    )(sinks, qh, kvh, kvh, kvh, kvh, bias)


def _swiglu_up(xb, w1, w3, layer, *, tm=1024, tn=256):
    s, d = xb.shape
    f = w1.shape[2]
    blk, imap = _tile(min(tm, s), min(tn, f))

    def ep(accs, extras, outs):
        outs[0][...] = (_silu(accs[0]) * accs[1]).astype(BF16)

    (h,) = _fused_matmul(xb, [(w1, layer, 0), (w3, layer, 0)], f, [],
                         [((s, f), BF16, blk, imap)], ep, tm=tm, tn=tn, tk=d, name="swiglu_up")
    return h


def _router_body(x_ref, w_ref, gates_ref, idx_ref):
    logits = jnp.dot(x_ref[...], w_ref[...], preferred_element_type=F32,
                     precision=lax.Precision.HIGHEST)
    ne = logits.shape[1]
    eid = lax.broadcasted_iota(jnp.int32, logits.shape, 1)
    m1 = jnp.max(logits, axis=-1, keepdims=True)
    i1 = jnp.min(jnp.where(logits == m1, eid, ne), axis=-1, keepdims=True)
    rest = jnp.where(eid == i1, -jnp.inf, logits)
    m2 = jnp.max(rest, axis=-1, keepdims=True)
    i2 = jnp.min(jnp.where(rest == m2, eid, ne), axis=-1, keepdims=True)
    e2 = jnp.exp(m2 - m1)
    w_top1 = 1.0 / (1.0 + e2)
    w_top2 = e2 / (1.0 + e2)
    gates_ref[...] = jnp.where(eid == i1, w_top1, 0.0) + jnp.where(eid == i2, w_top2, 0.0)
    idx_ref[...] = jnp.where(eid == 0, i1, i2)[:, :TOP_K]


def _router(x, w_router, *, tr=512):
    s, d = x.shape
    ne = w_router.shape[1]
    tr = min(tr, s)
    vmem = 2 * _nbytes((tr, d), F32) * 4 + 2 * _nbytes((d, V7X_LANES), F32)
    return pl.pallas_call(
        _router_body,
        grid=(s // tr,),
        in_specs=[pl.BlockSpec((tr, d), lambda i: (i, 0)), pl.BlockSpec((d, ne), lambda i: (0, 0))],
        out_specs=[pl.BlockSpec((tr, ne), lambda i: (i, 0)), pl.BlockSpec((tr, TOP_K), lambda i: (i, 0))],
        out_shape=[jax.ShapeDtypeStruct((s, ne), F32), jax.ShapeDtypeStruct((s, TOP_K), jnp.int32)],
        compiler_params=pltpu.CompilerParams(dimension_semantics=("parallel",),
                                             vmem_limit_bytes=_vmem_limit(vmem)),
        name="moe_router",
    )(x, w_router)


def _routing_tables(gates, idx, tm):
    s, ne = gates.shape
    n_assign = s * TOP_K
    ntiles = n_assign // tm + ne
    e_flat = idx.reshape(n_assign)
    onehot = (e_flat[:, None] == jnp.arange(ne, dtype=jnp.int32)[None, :]).astype(jnp.int32)
    rank = jnp.take_along_axis(jnp.cumsum(onehot, axis=0) - onehot, e_flat[:, None], axis=1)[:, 0]
    counts = jnp.sum(onehot, axis=0)
    padded = ((counts + tm - 1) // tm) * tm
    ends = jnp.cumsum(padded)
    pos = (ends - padded)[e_flat] + rank
    token = jnp.arange(n_assign, dtype=jnp.int32) // TOP_K
    gate = jnp.take_along_axis(gates, idx, axis=1).reshape(n_assign)
    row_token = jnp.zeros((ntiles * tm,), jnp.int32).at[pos].set(token)
    row_gate = jnp.zeros((ntiles * tm,), F32).at[pos].set(gate)
    tile_start = jnp.arange(ntiles, dtype=jnp.int32) * tm
    tile_expert = jnp.minimum(jnp.searchsorted(ends, tile_start, side="right"), ne - 1).astype(jnp.int32)
    tile_valid = (tile_start < ends[-1]).astype(jnp.int32)
    return row_token, row_gate, pos.reshape(s, TOP_K).astype(jnp.int32), tile_expert, tile_valid, ntiles


GATHER_ISSUE_UNROLL = 8


def _gather_body(tok_ref, tok_next_ref, x_hbm, o_ref, buf, sem, *, tm, ntiles):
    i = pl.program_id(0)
    slot = i % 2

    def row_copy(tile_tok_ref, slot_, r):
        return pltpu.make_async_copy(x_hbm.at[pl.ds(tile_tok_ref[0, 0, r], 1)],
                                     buf.at[slot_, pl.ds(r, 1)], sem.at[slot_])

    def start_tile(tile_tok_ref, slot_):
        def body(r, carry):
            row_copy(tile_tok_ref, slot_, r).start()
            return carry
        lax.fori_loop(0, tm, body, 0, unroll=GATHER_ISSUE_UNROLL)

    @pl.when(i == 0)
    def _():
        start_tile(tok_ref, slot)

    @pl.when(i + 1 < ntiles)
    def _():
        start_tile(tok_next_ref, 1 - slot)

    def wait_body(r, carry):
        row_copy(tok_ref, slot, r).wait()
        return carry
    lax.fori_loop(0, tm, wait_body, 0, unroll=GATHER_ISSUE_UNROLL)
    o_ref[...] = buf[slot].astype(BF16)


def _moe_gather(x, row_token, ntiles, tm):
    s, d = x.shape
    vmem = 2 * _nbytes((tm, d), F32) + 2 * _nbytes((tm, d), BF16) + _nbytes((tm, d), F32)
    tok = row_token.reshape(ntiles, 1, tm)
    return pl.pallas_call(
        functools.partial(_gather_body, tm=tm, ntiles=ntiles),
        grid=(ntiles,),
        in_specs=[pl.BlockSpec((1, 1, tm), lambda i: (i, 0, 0), memory_space=pltpu.SMEM),
                  pl.BlockSpec((1, 1, tm), lambda i: (jnp.minimum(i + 1, ntiles - 1), 0, 0),
                               memory_space=pltpu.SMEM),
                  pl.BlockSpec(memory_space=pl.ANY)],
        out_specs=pl.BlockSpec((tm, d), lambda i: (i, 0)),
        out_shape=jax.ShapeDtypeStruct((ntiles * tm, d), BF16),
        scratch_shapes=[pltpu.VMEM((2, tm, d), F32), pltpu.SemaphoreType.DMA((2,))],
        compiler_params=pltpu.CompilerParams(dimension_semantics=("arbitrary",),
                                             vmem_limit_bytes=_vmem_limit(vmem)),
        name="moe_gather",
    )(tok, tok, x)


def _expert_changed(te_ref, i):
    return (i == 0) | (te_ref[i] != te_ref[jnp.maximum(i - 1, 0)])


def _moe_up_body(te_ref, tv_ref, x_ref, w1_ref, w3_ref, gate_ref, o_ref, wb1, wb3):
    i = pl.program_id(1)

    @pl.when(_expert_changed(te_ref, i))
    def _():
        wb1[...] = w1_ref[...].astype(BF16)
        wb3[...] = w3_ref[...].astype(BF16)

    @pl.when(tv_ref[i] > 0)
    def _():
        xv = x_ref[...]
        a = jnp.dot(xv, wb1[...], preferred_element_type=F32)
        b = jnp.dot(xv, wb3[...], preferred_element_type=F32)
        o_ref[...] = (_silu(a) * b * gate_ref[...]).astype(BF16)

    @pl.when(tv_ref[i] == 0)
    def _():
        o_ref[...] = jnp.zeros_like(o_ref)


def _moe_up(xg, w1, w3, layer, row_gate, tile_expert, tile_valid, tm, *, tf=512):
    rows, d = xg.shape
    f = w1.shape[3]
    tf = min(tf, f)
    ntiles = rows // tm
    wspec = pl.BlockSpec((None, None, d, tf), lambda j, i, te, tv: (layer, te[i], 0, j))
    vmem = 2 * (_nbytes((tm, d), BF16) + 2 * _nbytes((d, tf), F32) + _nbytes((tm, tf), BF16)) \
        + 2 * _nbytes((d, tf), BF16) + 4 * _nbytes((tm, tf), F32)
    grid_spec = pltpu.PrefetchScalarGridSpec(
        num_scalar_prefetch=2,
        grid=(f // tf, ntiles),
        in_specs=[pl.BlockSpec((tm, d), lambda j, i, te, tv: (i, 0)), wspec, wspec,
                  pl.BlockSpec((tm, 1), lambda j, i, te, tv: (i, 0))],
        out_specs=pl.BlockSpec((tm, tf), lambda j, i, te, tv: (i, j)),
        scratch_shapes=[pltpu.VMEM((d, tf), BF16), pltpu.VMEM((d, tf), BF16)],
    )
    return pl.pallas_call(
        _moe_up_body,
        grid_spec=grid_spec,
        out_shape=jax.ShapeDtypeStruct((rows, f), BF16),
        compiler_params=pltpu.CompilerParams(dimension_semantics=("parallel", "arbitrary"),
                                             vmem_limit_bytes=_vmem_limit(vmem)),
        name="moe_up",
    )(tile_expert, tile_valid, xg, w1, w3, row_gate.reshape(rows, 1))


def _moe_down_body(te_ref, tv_ref, h_ref, w_ref, o_ref, wb):
    i = pl.program_id(1)

    @pl.when(_expert_changed(te_ref, i))
    def _():
        wb[...] = w_ref[...].astype(BF16)

    @pl.when(tv_ref[i] > 0)
    def _():
        o_ref[...] = jnp.dot(h_ref[...], wb[...], preferred_element_type=F32)

    @pl.when(tv_ref[i] == 0)
    def _():
        o_ref[...] = jnp.zeros_like(o_ref)


def _moe_down(h, w2, layer, tile_expert, tile_valid, tm, *, tn=2048):
    rows, f = h.shape
    d = w2.shape[3]
    tn = min(tn, d)
    ntiles = rows // tm
    vmem = 2 * (_nbytes((tm, f), BF16) + _nbytes((f, tn), F32) + _nbytes((tm, tn), F32)) \
        + _nbytes((f, tn), BF16) + 2 * _nbytes((tm, tn), F32)
    grid_spec = pltpu.PrefetchScalarGridSpec(
        num_scalar_prefetch=2,
        grid=(d // tn, ntiles),
        in_specs=[pl.BlockSpec((tm, f), lambda j, i, te, tv: (i, 0)),
                  pl.BlockSpec((None, None, f, tn), lambda j, i, te, tv: (layer, te[i], 0, j))],
        out_specs=pl.BlockSpec((tm, tn), lambda j, i, te, tv: (i, j)),
        scratch_shapes=[pltpu.VMEM((f, tn), BF16)],
    )
    return pl.pallas_call(
        _moe_down_body,
        grid_spec=grid_spec,
        out_shape=jax.ShapeDtypeStruct((rows, d), F32),
        compiler_params=pltpu.CompilerParams(dimension_semantics=("parallel", "arbitrary"),
                                             vmem_limit_bytes=_vmem_limit(vmem)),
        name="moe_down",
    )(tile_expert, tile_valid, h, w2)


def _combine_body(pos_ref, y_hbm, x_ref, g_ref, b_ref, o32_ref, o16_ref, buf, sem, *, tr, alpha):
    def row_copy(r, kk):
        return pltpu.make_async_copy(y_hbm.at[pl.ds(pos_ref[0, kk, r], 1)],
                                     buf.at[kk, pl.ds(r, 1)], sem.at[kk])

    def start_body(r, carry):
        for kk in range(TOP_K):
            row_copy(r, kk).start()
        return carry
    lax.fori_loop(0, tr, start_body, 0)

    def wait_body(r, carry):
        for kk in range(TOP_K):
            row_copy(r, kk).wait()
        return carry
    lax.fori_loop(0, tr, wait_body, 0)

    f = buf[0]
    for kk in range(1, TOP_K):
        f = f + buf[kk]
    out = _ln_rows(alpha * x_ref[...] + f, g_ref[...], b_ref[...])
    o32_ref[...] = out
    o16_ref[...] = out.astype(BF16)


def _moe_combine_ln(y, pos, x, g, b, alpha, *, tr=256):
    s, d = x.shape
    tr = min(tr, s)
    nt = s // tr
    pos_t = pos.reshape(nt, tr, TOP_K).transpose(0, 2, 1)
    row = pl.BlockSpec((tr, d), lambda i: (i, 0))
    vec = pl.BlockSpec((1, d), lambda i: (0, 0))
    vmem = TOP_K * _nbytes((tr, d), F32) + 2 * (2 * _nbytes((tr, d), F32) + _nbytes((tr, d), BF16)) \
        + 3 * _nbytes((tr, d), F32)
    return pl.pallas_call(
        functools.partial(_combine_body, tr=tr, alpha=alpha),
        grid=(nt,),
        in_specs=[pl.BlockSpec((1, TOP_K, tr), lambda i: (i, 0, 0), memory_space=pltpu.SMEM),
                  pl.BlockSpec(memory_space=pl.ANY), row, vec, vec],
        out_specs=[row, row],
        out_shape=[jax.ShapeDtypeStruct((s, d), F32), jax.ShapeDtypeStruct((s, d), BF16)],
        scratch_shapes=[pltpu.VMEM((TOP_K, tr, d), F32), pltpu.SemaphoreType.DMA((TOP_K,))],
        compiler_params=pltpu.CompilerParams(dimension_semantics=("arbitrary",),
                                             vmem_limit_bytes=_vmem_limit(vmem)),
        name="moe_combine_ln",
    )(pos_t, y, x, g.reshape(1, d), b.reshape(1, d))


def _ple(xb, x, p, w_gate, w_proj, layer, bi, *, tm=1024, tn=512):
    s, d = x.shape
    pdim = p.shape[-1]
    blk, imap = _tile(min(tm, s), min(tn, d))

    def ep(accs, extras, outs):
        proj = jnp.dot(extras[1][...].astype(BF16), extras[2][...].astype(BF16),
                       preferred_element_type=F32)
        out = extras[0][...] + jax.nn.sigmoid(accs[0]) * proj
        outs[0][...] = out
        outs[1][...] = out.astype(BF16)

    extras = [(x, blk, imap),
              (p, (None, None, blk[0], pdim), lambda i, j: (layer, bi, i, 0)),
              (w_proj, (None, pdim, blk[1]), lambda i, j: (layer, 0, j))]
    outs = [((s, d), F32, blk, imap), ((s, d), BF16, blk, imap)]
    return _fused_matmul(xb, [(w_gate, layer, 0)], d, extras, outs, ep, tm=tm, tn=tn, tk=d,
                         name="ple")


MOE_TILE_ROWS = 256


def kernel(x, p, ln1_g, ln1_b, ln2_g, ln2_b, a_w_in, a_lb_logits, a_out_g, a_w_o, kv_w, b_w_q, b_sinks, b_w_o, rel_bias, ffn_w1, ffn_w3, ffn_w2, moe_router, moe_w1, moe_w3, moe_w2, ple_w_proj, ple_w_gate):
    batch, s, d = x.shape
    depth = p.shape[0]
    n_a = a_w_in.shape[0]
    kv_dim = kv_w.shape[1] // 2
    n_kv = kv_dim // B_HEAD_DIM
    alpha = (2.0 * depth) ** 0.25
    moe_tm = min(MOE_TILE_ROWS, s)
    bias_band = _rel_bias_band(rel_bias.astype(F32))
    lb_logits = a_lb_logits.astype(F32)

    outs = []
    for bi in range(batch):
        xf = x[bi].astype(F32)
        xb = xf.astype(BF16)
        kvh = None
        for i in range(depth):
            if i < n_a:
                q, k, v, log_f, g = _hgrn_in_proj(xb, a_w_in, lb_logits, i)
                hb = _hgrn_core(q, k, v, log_f, g, a_out_g[i].astype(F32))
                y = _residual_matmul(hb, a_w_o, i, xf, alpha, name="mixer_out")
            else:
                j = i - n_a
                qh = _head_split_matmul(xb, b_w_q, j, d, name="swa_q")
                hb = _swa_core(qh, kvh, b_sinks[j].astype(F32), bias_band, n_kv)
                y = _residual_matmul(hb, b_w_o, j, xf, alpha, name="mixer_out")
            xf, xb = _layer_norm(y, ln1_g[i].astype(F32), ln1_b[i].astype(F32))
            li = i // 2
            if i % 2 == 0:
                hmid = _swiglu_up(xb, ffn_w1, ffn_w3, li)
                y = _residual_matmul(hmid, ffn_w2, li, xf, alpha, name="ffn_down", tn=1024,
                                     tk=2048)
                xf, xb = _layer_norm(y, ln2_g[i].astype(F32), ln2_b[i].astype(F32))
            else:
                gates, idx = _router(xf, moe_router[li].astype(F32))
                row_token, row_gate, pos, tile_expert, tile_valid, ntiles = _routing_tables(
                    gates, idx, moe_tm)
                xg = _moe_gather(xf, row_token, ntiles, moe_tm)
                hmid = _moe_up(xg, moe_w1, moe_w3, li, row_gate, tile_expert, tile_valid, moe_tm)
                yg = _moe_down(hmid, moe_w2, li, tile_expert, tile_valid, moe_tm)
                xf, xb = _moe_combine_ln(yg, pos, xf, ln2_g[i].astype(F32), ln2_b[i].astype(F32),
                                         alpha)
            xf, xb = _ple(xb, xf, p, ple_w_gate, ple_w_proj, i, bi)
            if i == n_a - 1:
                kvh = _head_split_matmul(xb, kv_w, None, 2 * kv_dim, name="kv_proj")
        outs.append(xf)
    return jnp.stack(outs, axis=0).astype(x.dtype)
```

```python
import functools

import numpy as np
import jax
import jax.numpy as jnp
from jax import lax
from jax.experimental import pallas as pl
from jax.experimental.pallas import tpu as pltpu

F32 = jnp.float32
BF16 = jnp.bfloat16

V7X_LANES = 128
V7X_SCOPED_VMEM_BYTES = 60000 * 1024
COMPILER_SCRATCH_BYTES = 6 * 1024 * 1024

MM_ROWS = 2048
MM_COLS = 256
MM_SUB_ROWS = 512

A_HEAD_DIM = 128
A_CHUNK = 64
A_SUB = 16
A_CUMSUM_ROWS = 256
B_HEAD_DIM = 64
B_BLOCK = 128
WINDOW = 128
REL_BUCKETS = 32
REL_MAX_DISTANCE = 128
TOP_K = 2
LN_EPS = 1e-5
RMS_EPS = 1e-6
LOG2E = 1.4426950408889634


def _vmem_limit(block_bytes):
    return int(min(block_bytes + COMPILER_SCRATCH_BYTES, V7X_SCOPED_VMEM_BYTES))


def _nbytes(shape, dtype):
    return int(np.prod([n for n in shape if n is not None])) * jnp.dtype(dtype).itemsize


def _silu(x):
    return x * jax.nn.sigmoid(x)


def _mm_body(*refs, nw, ne, no, epilogue):
    x_ref = refs[0]
    w_refs = refs[1:1 + nw]
    extra_refs = refs[1 + nw:1 + nw + ne]
    out_refs = refs[1 + nw + ne:1 + nw + ne + no]
    wbs = [w_ref[...].astype(BF16) for w_ref in w_refs]
    tm = x_ref.shape[0]
    sub = min(tm, MM_SUB_ROWS)
    for r0 in range(0, tm, sub):
        rows = slice(r0, r0 + sub)
        xv = x_ref[rows, :]
        prods = [jnp.dot(xv, wb, preferred_element_type=F32) for wb in wbs]
        epilogue(prods, extra_refs, out_refs, rows)


def _fused_matmul(x, ws, n_cols, extras, outs, epilogue, *, tm, tn, name, tk=None, k_block=0,
                  x_buffers=2):
    m, kdim = x.shape
    tk = tk or kdim
    tm, tn = min(tm, m), min(tn, n_cols)
    assert m % tm == 0 and n_cols % tn == 0 and kdim % tk == 0
    x_mode = {} if x_buffers == 2 else {"pipeline_mode": pl.Buffered(x_buffers)}
    in_specs = [pl.BlockSpec((tm, tk), lambda i, j: (i, k_block), **x_mode)]
    operands = [x]
    vmem = x_buffers * _nbytes((tm, tk), x.dtype)
    for w, layer, off in ws:
        if layer is None:
            spec = pl.BlockSpec((tk, tn), lambda i, j, off=off: (k_block, j + off))
        else:
            spec = pl.BlockSpec((None, tk, tn), lambda i, j, off=off, layer=layer:
                                (layer, k_block, j + off))
        in_specs.append(spec)
        operands.append(w)
        vmem += 2 * _nbytes((tk, tn), w.dtype) + _nbytes((tk, tn), BF16)
    for arr, bshape, imap in extras:
        in_specs.append(pl.BlockSpec(bshape, imap))
        operands.append(arr)
        vmem += 2 * _nbytes(bshape, arr.dtype)
    out_shapes, out_specs = [], []
    for shape, dtype, bshape, imap in outs:
        out_shapes.append(jax.ShapeDtypeStruct(shape, dtype))
        out_specs.append(pl.BlockSpec(bshape, imap))
        vmem += 2 * _nbytes(bshape, dtype)
    vmem += (2 + len(ws)) * _nbytes((tm, tn), F32)
    body = functools.partial(_mm_body, nw=len(ws), ne=len(extras), no=len(outs), epilogue=epilogue)
    return pl.pallas_call(
        body,
        grid=(m // tm, n_cols // tn),
        in_specs=in_specs,
        out_specs=out_specs,
        out_shape=out_shapes,
        compiler_params=pltpu.CompilerParams(dimension_semantics=("parallel", "parallel"),
                                             vmem_limit_bytes=_vmem_limit(vmem)),
        name=name,
    )(*operands)


def _tile(tm, tn):
    return (tm, tn), (lambda i, j: (i, j))


def _ln_rows(y, g, b):
    mu = jnp.mean(y, axis=-1, keepdims=True)
    yc = y - mu
    var = jnp.mean(yc * yc, axis=-1, keepdims=True)
    return yc * lax.rsqrt(var + LN_EPS) * g + b


def _route_top2(x, w_router):
    logits = jnp.dot(x, w_router, preferred_element_type=F32, precision=lax.Precision.HIGHEST)
    ne = logits.shape[1]
    eid = lax.broadcasted_iota(jnp.int32, logits.shape, 1)
    m1 = jnp.max(logits, axis=-1, keepdims=True)
    i1 = jnp.min(jnp.where(logits == m1, eid, ne), axis=-1, keepdims=True)
    rest = jnp.where(eid == i1, -jnp.inf, logits)
    m2 = jnp.max(rest, axis=-1, keepdims=True)
    i2 = jnp.min(jnp.where(rest == m2, eid, ne), axis=-1, keepdims=True)
    e2 = jnp.exp(m2 - m1)
    w_top1 = 1.0 / (1.0 + e2)
    w_top2 = e2 / (1.0 + e2)
    gates = jnp.where(eid == i1, w_top1, 0.0) + jnp.where(eid == i2, w_top2, 0.0)
    return gates, jnp.where(eid == 0, i1, i2)[:, :TOP_K]


def _ln_body(*refs, route):
    if route:
        y_ref, g_ref, b_ref, wr_ref, o32_ref, o16_ref, gates_ref, idx_ref = refs
    else:
        y_ref, g_ref, b_ref, o32_ref, o16_ref = refs
    out = _ln_rows(y_ref[...], g_ref[...], b_ref[...])
    o32_ref[...] = out
    o16_ref[...] = out.astype(BF16)
    if route:
        gates_ref[...], idx_ref[...] = _route_top2(out, wr_ref[...])


def _layer_norm(y, g, b, w_router=None, *, tr=256):
    s, d = y.shape
    tr = min(tr, s)
    assert s % tr == 0
    row = pl.BlockSpec((tr, d), lambda i: (i, 0))
    vec = pl.BlockSpec((1, d), lambda i: (0, 0))
    in_specs, operands = [row, vec, vec], [y, g.reshape(1, d), b.reshape(1, d)]
    out_specs = [row, row]
    out_shape = [jax.ShapeDtypeStruct((s, d), F32), jax.ShapeDtypeStruct((s, d), BF16)]
    vmem = 2 * (2 * _nbytes((tr, d), F32) + _nbytes((tr, d), BF16)) + 2 * _nbytes((tr, d), F32)
    if w_router is not None:
        ne = w_router.shape[1]
        in_specs.append(pl.BlockSpec((d, ne), lambda i: (0, 0)))
        operands.append(w_router)
        out_specs += [pl.BlockSpec((tr, ne), lambda i: (i, 0)),
                      pl.BlockSpec((tr, TOP_K), lambda i: (i, 0))]
        out_shape += [jax.ShapeDtypeStruct((s, ne), F32), jax.ShapeDtypeStruct((s, TOP_K), jnp.int32)]
        vmem += 2 * _nbytes((d, V7X_LANES), F32) + 4 * _nbytes((tr, d), F32)
    return pl.pallas_call(
        functools.partial(_ln_body, route=w_router is not None),
        grid=(s // tr,),
        in_specs=in_specs,
        out_specs=out_specs,
        out_shape=out_shape,
        compiler_params=pltpu.CompilerParams(dimension_semantics=("parallel",),
                                             vmem_limit_bytes=_vmem_limit(vmem)),
        name="layer_norm_route" if w_router is not None else "layer_norm",
    )(*operands)


def _lower_bound(lbl, layer):
    mx = jnp.max(lbl, axis=0, keepdims=True)
    e = jnp.exp(lbl - mx)
    return jnp.sum(e[:layer + 1], axis=0, keepdims=True) / jnp.sum(e, axis=0, keepdims=True)


def _hgrn_in_proj(xb, w_in, lb_logits, layer, *, tm=MM_ROWS, tn=MM_COLS):
    s, d = xb.shape
    nseg = d // min(tn, d)
    blk, imap = _tile(min(tm, s), min(tn, d))

    def ep_silu(accs, extras, outs, rows):
        outs[0][rows, :] = _silu(accs[0]).astype(BF16)

    def ep_id(accs, extras, outs, rows):
        outs[0][rows, :] = accs[0].astype(BF16)

    def ep_forget(accs, extras, outs, rows):
        lb = _lower_bound(extras[0][...], layer)
        f_raw = accs[0]
        e = jnp.exp(-jnp.abs(f_raw))
        log_sig = jnp.minimum(f_raw, 0.0) - jnp.log(1.0 + e)
        sig_neg = jnp.where(f_raw >= 0.0, e, 1.0) / (1.0 + e)
        a = jnp.log(lb)
        b = jnp.log1p(-lb) + log_sig
        outs[0][rows, :] = jnp.maximum(a, b) + jnp.log(1.0 + jnp.exp(-jnp.abs(a - b)))
        outs[1][rows, :] = ((1.0 - lb) * sig_neg).astype(BF16)

    def call(seg, epilogue, extras, out_dtypes, name, x_buffers=2):
        outs = [((s, d), dt, blk, imap) for dt in out_dtypes]
        return _fused_matmul(xb, [(w_in, layer, seg * nseg)], d, extras, outs, epilogue,
                             tm=tm, tn=tn, name=name, x_buffers=x_buffers)

    nslot = lb_logits.shape[0]
    lb_extra = [(lb_logits, (nslot, blk[1]), lambda i, j: (0, j))]
    (q,) = call(0, ep_silu, [], [BF16], "hgrn_q")
    log_f, k = call(1, ep_forget, lb_extra, [F32, BF16], "hgrn_f", x_buffers=1)
    (v,) = call(2, ep_id, [], [BF16], "hgrn_v")
    (g,) = call(3, ep_silu, [], [BF16], "hgrn_g")
    return q, k, v, log_f, g


def _hgrn_body(q_ref, k_ref, v_ref, lf_ref, g_ref, og_ref, o_ref, st_ref, *, tc):
    @pl.when(pl.program_id(1) == 0)
    def _():
        st_ref[...] = jnp.zeros_like(st_ref)

    tg = min(tc, A_CUMSUM_ROWS)
    row = lax.broadcasted_iota(jnp.int32, (tg, tg), 0)
    col = lax.broadcasted_iota(jnp.int32, (tg, tg), 1)
    tri = jnp.where((col <= row) & (col // A_CHUNK == row // A_CHUNK), 1.0, 0.0).astype(BF16)
    cums = []
    for gi in range(tc // tg):
        lf = lf_ref[gi * tg:(gi + 1) * tg, :]
        hi = lf.astype(BF16)
        r1 = lf - hi.astype(F32)
        mid = r1.astype(BF16)
        lo = (r1 - mid.astype(F32)).astype(BF16)
        cums.append(jnp.dot(tri, hi, preferred_element_type=F32)
                    + jnp.dot(tri, mid, preferred_element_type=F32)
                    + jnp.dot(tri, lo, preferred_element_type=F32))
    cum = jnp.concatenate(cums, axis=0)

    out_g = og_ref[...]
    half = A_SUB // 2
    lane = lax.broadcasted_iota(jnp.int32, (half, A_HEAD_DIM), 1)
    hrow = lax.broadcasted_iota(jnp.int32, (half, A_HEAD_DIM), 0)
    nsub = A_CHUNK // A_SUB

    for ci in range(tc // A_CHUNK):
        rows = slice(ci * A_CHUNK, (ci + 1) * A_CHUNK)
        c2 = cum[rows] * LOG2E
        qc = q_ref[rows, :].astype(F32)
        kc = k_ref[rows, :].astype(F32)
        vc = v_ref[rows, :]
        last = c2[A_CHUNK - 1:A_CHUNK]
        st = st_ref[...]
        qe = (qc * jnp.exp2(c2)).astype(BF16)
        o_inter = lax.dot_general(qe, st.astype(BF16), (((1,), (1,)), ((), ())),
                                  preferred_element_type=F32)
        kd = (kc * jnp.exp2(last - c2)).astype(BF16)
        upd = lax.dot_general(vc, kd, (((0,), (0,)), ((), ())), preferred_element_type=F32)
        st_ref[...] = st * jnp.exp2(last) + upd

        p_rows = []
        k_blocks, prev_ref = [], None
        for si in range(nsub):
            lo_r = si * A_SUB
            cs, qs, ks = c2[lo_r:lo_r + A_SUB], qc[lo_r:lo_r + A_SUB], kc[lo_r:lo_r + A_SUB]
            off = None
            if si > 0:
                ref = c2[lo_r - 1:lo_r]
                if prev_ref is not None:
                    rebase = jnp.exp2(ref - prev_ref)
                    k_blocks = [kb * rebase for kb in k_blocks]
                k_blocks.append(kc[lo_r - A_SUB:lo_r] * jnp.exp2(ref - c2[lo_r - A_SUB:lo_r]))
                prev_ref = ref
                k_t = jnp.concatenate(
                    k_blocks + [jnp.zeros((A_CHUNK - lo_r, A_HEAD_DIM), F32)], axis=0)
                q_t = (qs * jnp.exp2(cs - ref)).astype(BF16)
                off = lax.dot_general(q_t, k_t.astype(BF16), (((1,), (1,)), ((), ())),
                                      preferred_element_type=F32)
            c_h = (cs[:half], cs[half:])
            q_h = (qs[:half], qs[half:])
            p_h = [jnp.zeros((half, A_HEAD_DIM), F32), jnp.zeros((half, A_HEAD_DIM), F32)]
            for sj in range(A_SUB):
                for hh in range(sj // half, 2):
                    decay = jnp.exp2(c_h[hh] - cs[sj:sj + 1])
                    colv = jnp.sum(q_h[hh] * (ks[sj:sj + 1] * decay), axis=1, keepdims=True)
                    sel = lane == lo_r + sj
                    if sj // half == hh:
                        sel = sel & (hrow >= sj - hh * half)
                    p_h[hh] = jnp.where(sel, colv, p_h[hh])
            for hh in range(2):
                ph = p_h[hh][:, :A_CHUNK]
                p_rows.append(ph if off is None else ph + off[hh * half:(hh + 1) * half])
        scores = jnp.concatenate(p_rows, axis=0).astype(BF16)
        o = o_inter + jnp.dot(scores, vc, preferred_element_type=F32)
        o = o * lax.rsqrt(jnp.mean(o * o, axis=-1, keepdims=True) + RMS_EPS)
        o_ref[rows, :] = (o * out_g * g_ref[rows, :].astype(F32)).astype(BF16)


def _hgrn_core(q, k, v, log_f, g, out_g, *, tc=512):
    s, d = q.shape
    tc = min(tc, s)
    assert s % tc == 0 and tc % A_CHUNK == 0 and d % A_HEAD_DIM == 0
    blk = pl.BlockSpec((tc, A_HEAD_DIM), lambda h, c: (c, h))
    vmem = 2 * (4 * _nbytes((tc, A_HEAD_DIM), BF16) + 2 * _nbytes((tc, A_HEAD_DIM), F32)) \
        + 8 * _nbytes((A_CUMSUM_ROWS, A_CUMSUM_ROWS), F32) + 8 * _nbytes((tc, A_HEAD_DIM), F32)
    return pl.pallas_call(
        functools.partial(_hgrn_body, tc=tc),
        grid=(d // A_HEAD_DIM, s // tc),
        in_specs=[blk, blk, blk, blk, blk, pl.BlockSpec((1, A_HEAD_DIM), lambda h, c: (0, h))],
        out_specs=blk,
        out_shape=jax.ShapeDtypeStruct((s, d), BF16),
        scratch_shapes=[pltpu.VMEM((A_HEAD_DIM, A_HEAD_DIM), F32)],
        compiler_params=pltpu.CompilerParams(dimension_semantics=("parallel", "arbitrary"),
                                             vmem_limit_bytes=_vmem_limit(vmem)),
        name="hgrn_core",
    )(q, k, v, log_f, g, out_g.reshape(1, d))


def _residual_matmul(hb, w, layer, resid, alpha, *, name, tm=MM_ROWS, tn=MM_COLS):
    s, d = resid.shape
    kdim = hb.shape[1]
    tk = min(kdim, d)
    blk, imap = _tile(min(tm, s), min(tn, d))
    y = resid
    for kb in range(kdim // tk):
        scale = alpha if kb == 0 else 1.0

        def ep(accs, extras, outs, rows, scale=scale):
            outs[0][rows, :] = scale * extras[0][rows, :] + accs[0]

        (y,) = _fused_matmul(hb, [(w, layer, 0)], d, [(y, blk, imap)], [((s, d), F32, blk, imap)],
                             ep, tm=tm, tn=tn, tk=tk, k_block=kb, name=name, x_buffers=1)
    return y


def _rel_bucket_band():
    i = np.arange(B_BLOCK)[:, None]
    j = np.arange(2 * B_BLOCK)[None, :]
    d = np.clip(B_BLOCK + i - j, 0, None)
    max_exact = REL_BUCKETS // 2
    large = max_exact + (np.log(np.maximum(d, 1) / max_exact)
                         / np.log(REL_MAX_DISTANCE / max_exact)
                         * (REL_BUCKETS - max_exact)).astype(np.int32)
    large = np.minimum(large, REL_BUCKETS - 1)
    return np.where(d < max_exact, d, large).astype(np.int32)


def _bias_body(rb_ref, bucket_ref, o_ref, *, heads_per_step):
    h0 = pl.program_id(0) * heads_per_step
    bucket = bucket_ref[...]
    for hh in range(heads_per_step):
        acc = jnp.zeros(bucket.shape, F32)
        for b in range(REL_BUCKETS):
            acc = jnp.where(bucket == b, rb_ref[b, h0 + hh], acc)
        o_ref[hh] = acc


def _rel_bias_band(rel_bias, *, heads_per_step=8):
    nh = rel_bias.shape[1]
    bucket = jnp.asarray(_rel_bucket_band())
    return pl.pallas_call(
        functools.partial(_bias_body, heads_per_step=heads_per_step),
        grid=(nh // heads_per_step,),
        in_specs=[pl.BlockSpec(memory_space=pltpu.SMEM),
                  pl.BlockSpec((B_BLOCK, 2 * B_BLOCK), lambda i: (0, 0))],
        out_specs=pl.BlockSpec((heads_per_step, B_BLOCK, 2 * B_BLOCK), lambda i: (i, 0, 0)),
        out_shape=jax.ShapeDtypeStruct((nh, B_BLOCK, 2 * B_BLOCK), F32),
        compiler_params=pltpu.CompilerParams(dimension_semantics=("parallel",)),
        name="rel_bias_band",
    )(rel_bias, bucket)


def _head_split_matmul(xb, w, layer, n_cols, *, name, tm=MM_ROWS, tn=MM_COLS):
    s, kdim = xb.shape
    tm, tn = min(tm, s), min(tn, n_cols)
    hpt = tn // B_HEAD_DIM

    def ep(accs, extras, outs, rows):
        for c in range(hpt):
            outs[0][c, rows, :] = accs[0][:, c * B_HEAD_DIM:(c + 1) * B_HEAD_DIM].astype(BF16)

    outs = [((n_cols // B_HEAD_DIM, s, B_HEAD_DIM), BF16, (hpt, tm, B_HEAD_DIM),
             lambda i, j: (j, i, 0))]
    (out,) = _fused_matmul(xb, [(w, layer, 0)], n_cols, [], outs, ep, tm=tm, tn=tn, name=name)
    return out


def _attn_body(sink_ref, q_ref, kp_ref, kc_ref, vp_ref, vc_ref, bias_ref, o_ref, *, group, qb):
    h = pl.program_id(0)
    m_blk = pl.program_id(1)
    c = B_BLOCK
    k_all = jnp.concatenate([kp_ref[0], kc_ref[0]], axis=0)
    v_all = jnp.concatenate([vp_ref[0], vc_ref[0]], axis=0)
    qi = lax.broadcasted_iota(jnp.int32, (c, 2 * c), 0)
    kj = lax.broadcasted_iota(jnp.int32, (c, 2 * c), 1)
    dist = c + qi - kj
    in_window = (dist >= 0) & (dist < WINDOW)
    for b in range(qb):
        q = q_ref[:, b * c:(b + 1) * c, :].reshape(group * c, B_HEAD_DIM)
        kb = k_all[b * c:(b + 2) * c]
        vb = v_all[b * c:(b + 2) * c]
        logits = lax.dot_general(q, kb, (((1,), (1,)), ((), ())), preferred_element_type=F32)
        logits = logits * (B_HEAD_DIM ** -0.5)
        mask = in_window & ((m_blk * qb + b > 0) | (kj >= c)) if b == 0 else in_window
        probs = []
        for gi in range(group):
            lg = jnp.where(mask, logits[gi * c:(gi + 1) * c] + bias_ref[gi], -jnp.inf)
            sink = sink_ref[h * group + gi]
            m = jnp.maximum(jnp.max(lg, axis=-1, keepdims=True), sink)
            e = jnp.exp(lg - m)
            denom = jnp.sum(e, axis=-1, keepdims=True) + jnp.exp(sink - m)
            probs.append((e / denom).astype(BF16))
        pv = jnp.dot(jnp.concatenate(probs, axis=0), vb, preferred_element_type=F32)
        o_ref[b * c:(b + 1) * c, :] = jnp.concatenate(
            [pv[gi * c:(gi + 1) * c] for gi in range(group)], axis=1).astype(BF16)


def _swa_core(qh, kvh, sinks, bias, n_kv, *, qb=4):
    nq, s, hd = qh.shape
    group = nq // n_kv
    nb = s // B_BLOCK
    qb = min(qb, nb)
    assert nb % qb == 0
    prev = lambda m: jnp.maximum(m * qb - 1, 0)
    prev_blk = (1, B_BLOCK, hd)
    cur_blk = (1, qb * B_BLOCK, hd)
    in_specs = [
        pl.BlockSpec(memory_space=pltpu.SMEM),
        pl.BlockSpec((group, qb * B_BLOCK, hd), lambda h, m: (h, m, 0)),
        pl.BlockSpec(prev_blk, lambda h, m: (h, prev(m), 0)),
        pl.BlockSpec(cur_blk, lambda h, m: (h, m, 0)),
        pl.BlockSpec(prev_blk, lambda h, m: (h + n_kv, prev(m), 0)),
        pl.BlockSpec(cur_blk, lambda h, m: (h + n_kv, m, 0)),
        pl.BlockSpec((group, B_BLOCK, 2 * B_BLOCK), lambda h, m: (h, 0, 0)),
    ]
    vmem = 2 * (_nbytes((group, B_BLOCK, 2 * B_BLOCK), F32)
                + (group + 3) * _nbytes((qb * B_BLOCK, V7X_LANES), BF16)
                + _nbytes((qb * B_BLOCK, group * hd), BF16)) \
        + 6 * qb * _nbytes((group * B_BLOCK, 2 * B_BLOCK), F32)
    return pl.pallas_call(
        functools.partial(_attn_body, group=group, qb=qb),
        grid=(n_kv, nb // qb),
        in_specs=in_specs,
        out_specs=pl.BlockSpec((qb * B_BLOCK, group * hd), lambda h, m: (m, h)),
        out_shape=jax.ShapeDtypeStruct((s, nq * hd), BF16),
        compiler_params=pltpu.CompilerParams(dimension_semantics=("parallel", "parallel"),
                                             vmem_limit_bytes=_vmem_limit(vmem)),
        name="swa_core",
    )(sinks, qh, kvh, kvh, kvh, kvh, bias)


def _swiglu_up(xb, w1, w3, layer, *, tm=MM_ROWS, tn=MM_COLS):
    s, d = xb.shape
    f = w1.shape[2]
    blk, imap = _tile(min(tm, s), min(tn, f))

    def ep(accs, extras, outs, rows):
        outs[0][rows, :] = (_silu(accs[0]) * accs[1]).astype(BF16)

    (h,) = _fused_matmul(xb, [(w1, layer, 0), (w3, layer, 0)], f, [],
                         [((s, f), BF16, blk, imap)], ep, tm=tm, tn=tn, name="swiglu_up",
                         x_buffers=1)
    return h


def _routing_tables(gates, idx, tm):
    s, ne = gates.shape
    n_assign = s * TOP_K
    ntiles = n_assign // tm + ne
    e_flat = idx.reshape(n_assign)
    onehot = (e_flat[:, None] == jnp.arange(ne, dtype=jnp.int32)[None, :]).astype(jnp.int32)
    rank = jnp.take_along_axis(jnp.cumsum(onehot, axis=0) - onehot, e_flat[:, None], axis=1)[:, 0]
    counts = jnp.sum(onehot, axis=0)
    padded = ((counts + tm - 1) // tm) * tm
    ends = jnp.cumsum(padded)
    pos = (ends - padded)[e_flat] + rank
    token = jnp.arange(n_assign, dtype=jnp.int32) // TOP_K
    gate = jnp.take_along_axis(gates, idx, axis=1).reshape(n_assign)
    row_token = jnp.zeros((ntiles * tm,), jnp.int32).at[pos].set(token)
    row_gate = jnp.zeros((ntiles * tm,), F32).at[pos].set(gate)
    tile_start = jnp.arange(ntiles, dtype=jnp.int32) * tm
    tile_expert = jnp.minimum(jnp.searchsorted(ends, tile_start, side="right"), ne - 1).astype(jnp.int32)
    tile_valid = (tile_start < ends[-1]).astype(jnp.int32)
    return row_token, row_gate, pos.reshape(s, TOP_K).astype(jnp.int32), tile_expert, tile_valid, ntiles


GATHER_ISSUE_UNROLL = 8


def _gather_body(tv_ref, tok_ref, tok_next_ref, x_hbm, o_ref, buf, sem, *, tm, ntiles):
    i = pl.program_id(0)
    slot = i % 2

    def row_copy(tile_tok_ref, slot_, r):
        return pltpu.make_async_copy(x_hbm.at[pl.ds(tile_tok_ref[0, 0, r], 1)],
                                     buf.at[slot_, pl.ds(r, 1)], sem.at[slot_])

    def start_tile(tile_tok_ref, slot_):
        def body(r, carry):
            row_copy(tile_tok_ref, slot_, r).start()
            return carry
        lax.fori_loop(0, tm, body, 0, unroll=GATHER_ISSUE_UNROLL)

    @pl.when((i == 0) & (tv_ref[0] > 0))
    def _():
        start_tile(tok_ref, slot)

    @pl.when((i + 1 < ntiles) & (tv_ref[jnp.minimum(i + 1, ntiles - 1)] > 0))
    def _():
        start_tile(tok_next_ref, 1 - slot)

    @pl.when(tv_ref[i] > 0)
    def _():
        def wait_body(r, carry):
            row_copy(tok_ref, slot, r).wait()
            return carry
        lax.fori_loop(0, tm, wait_body, 0, unroll=GATHER_ISSUE_UNROLL)
        o_ref[...] = buf[slot].astype(BF16)

    @pl.when(tv_ref[i] == 0)
    def _():
        o_ref[...] = jnp.zeros_like(o_ref)


def _moe_gather(x, row_token, tile_valid, ntiles, tm):
    s, d = x.shape
    vmem = 2 * _nbytes((tm, d), F32) + 2 * _nbytes((tm, d), BF16) + _nbytes((tm, d), F32)
    tok = row_token.reshape(ntiles, 1, tm)
    grid_spec = pltpu.PrefetchScalarGridSpec(
        num_scalar_prefetch=1,
        grid=(ntiles,),
        in_specs=[pl.BlockSpec((1, 1, tm), lambda i, tv: (i, 0, 0), memory_space=pltpu.SMEM),
                  pl.BlockSpec((1, 1, tm), lambda i, tv: (jnp.minimum(i + 1, ntiles - 1), 0, 0),
                               memory_space=pltpu.SMEM),
                  pl.BlockSpec(memory_space=pl.ANY)],
        out_specs=pl.BlockSpec((tm, d), lambda i, tv: (i, 0)),
        scratch_shapes=[pltpu.VMEM((2, tm, d), F32), pltpu.SemaphoreType.DMA((2,))],
    )
    return pl.pallas_call(
        functools.partial(_gather_body, tm=tm, ntiles=ntiles),
        grid_spec=grid_spec,
        out_shape=jax.ShapeDtypeStruct((ntiles * tm, d), BF16),
        compiler_params=pltpu.CompilerParams(dimension_semantics=("arbitrary",),
                                             vmem_limit_bytes=_vmem_limit(vmem)),
        name="moe_gather",
    )(tile_valid, tok, tok, x)


def _expert_changed(te_ref, i):
    return (i == 0) | (te_ref[i] != te_ref[jnp.maximum(i - 1, 0)])


def _moe_up_body(te_ref, tv_ref, x_ref, w1_ref, w3_ref, gate_ref, o_ref, wb1, wb3):
    i = pl.program_id(1)

    @pl.when(_expert_changed(te_ref, i))
    def _():
        wb1[...] = w1_ref[...].astype(BF16)
        wb3[...] = w3_ref[...].astype(BF16)

    @pl.when(tv_ref[i] > 0)
    def _():
        xv = x_ref[...]
        a = jnp.dot(xv, wb1[...], preferred_element_type=F32)
        b = jnp.dot(xv, wb3[...], preferred_element_type=F32)
        o_ref[...] = (_silu(a) * b * gate_ref[...]).astype(BF16)

    @pl.when(tv_ref[i] == 0)
    def _():
        o_ref[...] = jnp.zeros_like(o_ref)


def _moe_up(xg, w1, w3, layer, row_gate, tile_expert, tile_valid, tm, *, tf=512):
    rows, d = xg.shape
    f = w1.shape[3]
    tf = min(tf, f)
    ntiles = rows // tm
    wspec = pl.BlockSpec((None, None, d, tf), lambda j, i, te, tv: (layer, te[i], 0, j))
    vmem = 2 * (_nbytes((tm, d), BF16) + 2 * _nbytes((d, tf), F32) + _nbytes((tm, tf), BF16)) \
        + 2 * _nbytes((d, tf), BF16) + 4 * _nbytes((tm, tf), F32)
    grid_spec = pltpu.PrefetchScalarGridSpec(
        num_scalar_prefetch=2,
        grid=(f // tf, ntiles),
        in_specs=[pl.BlockSpec((tm, d), lambda j, i, te, tv: (i, 0)), wspec, wspec,
                  pl.BlockSpec((tm, 1), lambda j, i, te, tv: (i, 0))],
        out_specs=pl.BlockSpec((tm, tf), lambda j, i, te, tv: (i, j)),
        scratch_shapes=[pltpu.VMEM((d, tf), BF16), pltpu.VMEM((d, tf), BF16)],
    )
    return pl.pallas_call(
        _moe_up_body,
        grid_spec=grid_spec,
        out_shape=jax.ShapeDtypeStruct((rows, f), BF16),
        compiler_params=pltpu.CompilerParams(dimension_semantics=("parallel", "arbitrary"),
                                             vmem_limit_bytes=_vmem_limit(vmem)),
        name="moe_up",
    )(tile_expert, tile_valid, xg, w1, w3, row_gate.reshape(rows, 1))


def _moe_down_body(te_ref, tv_ref, h_ref, w_ref, o_ref, wb):
    i = pl.program_id(1)

    @pl.when(_expert_changed(te_ref, i))
    def _():
        wb[...] = w_ref[...].astype(BF16)

    @pl.when(tv_ref[i] > 0)
    def _():
        o_ref[...] = jnp.dot(h_ref[...], wb[...], preferred_element_type=F32)

    @pl.when(tv_ref[i] == 0)
    def _():
        o_ref[...] = jnp.zeros_like(o_ref)


def _moe_down(h, w2, layer, tile_expert, tile_valid, tm, *, tn=2048):
    rows, f = h.shape
    d = w2.shape[3]
    tn = min(tn, d)
    ntiles = rows // tm
    vmem = 2 * (_nbytes((tm, f), BF16) + _nbytes((f, tn), F32) + _nbytes((tm, tn), F32)) \
        + _nbytes((f, tn), BF16) + 2 * _nbytes((tm, tn), F32)
    grid_spec = pltpu.PrefetchScalarGridSpec(
        num_scalar_prefetch=2,
        grid=(d // tn, ntiles),
        in_specs=[pl.BlockSpec((tm, f), lambda j, i, te, tv: (i, 0)),
                  pl.BlockSpec((None, None, f, tn), lambda j, i, te, tv: (layer, te[i], 0, j))],
        out_specs=pl.BlockSpec((tm, tn), lambda j, i, te, tv: (i, j)),
        scratch_shapes=[pltpu.VMEM((f, tn), BF16)],
    )
    return pl.pallas_call(
        _moe_down_body,
        grid_spec=grid_spec,
        out_shape=jax.ShapeDtypeStruct((rows, d), F32),
        compiler_params=pltpu.CompilerParams(dimension_semantics=("parallel", "arbitrary"),
                                             vmem_limit_bytes=_vmem_limit(vmem)),
        name="moe_down",
    )(tile_expert, tile_valid, h, w2)


def _combine_body(pos_ref, pos_next_ref, y_hbm, x_ref, g_ref, b_ref, o32_ref, o16_ref, buf, sem,
                  *, tr, nt, alpha):
    i = pl.program_id(0)
    slot = i % 2

    def row_copy(tile_pos_ref, slot_, r, kk):
        return pltpu.make_async_copy(y_hbm.at[pl.ds(tile_pos_ref[0, kk, r], 1)],
                                     buf.at[slot_, kk, pl.ds(r, 1)], sem.at[slot_])

    def start_tile(tile_pos_ref, slot_):
        def body(r, carry):
            for kk in range(TOP_K):
                row_copy(tile_pos_ref, slot_, r, kk).start()
            return carry
        lax.fori_loop(0, tr, body, 0, unroll=GATHER_ISSUE_UNROLL)

    @pl.when(i == 0)
    def _():
        start_tile(pos_ref, slot)

    @pl.when(i + 1 < nt)
    def _():
        start_tile(pos_next_ref, 1 - slot)

    def wait_body(r, carry):
        for kk in range(TOP_K):
            row_copy(pos_ref, slot, r, kk).wait()
        return carry
    lax.fori_loop(0, tr, wait_body, 0, unroll=GATHER_ISSUE_UNROLL)

    f = buf[slot, 0]
    for kk in range(1, TOP_K):
        f = f + buf[slot, kk]
    out = _ln_rows(alpha * x_ref[...] + f, g_ref[...], b_ref[...])
    o32_ref[...] = out
    o16_ref[...] = out.astype(BF16)


def _moe_combine_ln(y, pos, x, g, b, alpha, *, tr=256):
    s, d = x.shape
    tr = min(tr, s)
    nt = s // tr
    pos_t = pos.reshape(nt, tr, TOP_K).transpose(0, 2, 1)
    row = pl.BlockSpec((tr, d), lambda i: (i, 0))
    vec = pl.BlockSpec((1, d), lambda i: (0, 0))
    vmem = 2 * TOP_K * _nbytes((tr, d), F32) \
        + 2 * (2 * _nbytes((tr, d), F32) + _nbytes((tr, d), BF16)) + 3 * _nbytes((tr, d), F32)
    pos_blk = (1, TOP_K, tr)
    return pl.pallas_call(
        functools.partial(_combine_body, tr=tr, nt=nt, alpha=alpha),
        grid=(nt,),
        in_specs=[pl.BlockSpec(pos_blk, lambda i: (i, 0, 0), memory_space=pltpu.SMEM),
                  pl.BlockSpec(pos_blk, lambda i: (jnp.minimum(i + 1, nt - 1), 0, 0),
                               memory_space=pltpu.SMEM),
                  pl.BlockSpec(memory_space=pl.ANY), row, vec, vec],
        out_specs=[row, row],
        out_shape=[jax.ShapeDtypeStruct((s, d), F32), jax.ShapeDtypeStruct((s, d), BF16)],
        scratch_shapes=[pltpu.VMEM((2, TOP_K, tr, d), F32), pltpu.SemaphoreType.DMA((2,))],
        compiler_params=pltpu.CompilerParams(dimension_semantics=("arbitrary",),
                                             vmem_limit_bytes=_vmem_limit(vmem)),
        name="moe_combine_ln",
    )(pos_t, pos_t, y, x, g.reshape(1, d), b.reshape(1, d))


def _ple(xb, x, p, w_gate, w_proj, layer, bi, *, tm=MM_ROWS, tn=MM_COLS):
    s, d = x.shape
    pdim = p.shape[-1]
    blk, imap = _tile(min(tm, s), min(tn, d))

    def ep(accs, extras, outs, rows):
        proj = jnp.dot(extras[1][rows, :].astype(BF16), extras[2][...].astype(BF16),
                       preferred_element_type=F32)
        out = extras[0][rows, :] + jax.nn.sigmoid(accs[0]) * proj
        outs[0][rows, :] = out
        outs[1][rows, :] = out.astype(BF16)

    extras = [(x, blk, imap),
              (p, (None, None, blk[0], pdim), lambda i, j: (layer, bi, i, 0)),
              (w_proj, (None, pdim, blk[1]), lambda i, j: (layer, 0, j))]
    outs = [((s, d), F32, blk, imap), ((s, d), BF16, blk, imap)]
    return _fused_matmul(xb, [(w_gate, layer, 0)], d, extras, outs, ep, tm=tm, tn=tn, name="ple",
                         x_buffers=1)


MOE_TILE_ROWS = 256


def kernel(x, p, ln1_g, ln1_b, ln2_g, ln2_b, a_w_in, a_lb_logits, a_out_g, a_w_o, kv_w, b_w_q, b_sinks, b_w_o, rel_bias, ffn_w1, ffn_w3, ffn_w2, moe_router, moe_w1, moe_w3, moe_w2, ple_w_proj, ple_w_gate):
    batch, s, d = x.shape
    depth = p.shape[0]
    n_a = a_w_in.shape[0]
    kv_dim = kv_w.shape[1] // 2
    n_kv = kv_dim // B_HEAD_DIM
    alpha = (2.0 * depth) ** 0.25
    moe_tm = min(MOE_TILE_ROWS, s)
    bias_band = _rel_bias_band(rel_bias.astype(F32))
    lb_logits = a_lb_logits.astype(F32)

    outs = []
    for bi in range(batch):
        xf = x[bi].astype(F32)
        xb = xf.astype(BF16)
        kvh = None
        for i in range(depth):
            if i < n_a:
                q, k, v, log_f, g = _hgrn_in_proj(xb, a_w_in, lb_logits, i)
                hb = _hgrn_core(q, k, v, log_f, g, a_out_g[i].astype(F32))
                y = _residual_matmul(hb, a_w_o, i, xf, alpha, name="mixer_out")
            else:
                j = i - n_a
                qh = _head_split_matmul(xb, b_w_q, j, d, name="swa_q")
                hb = _swa_core(qh, kvh, b_sinks[j].astype(F32), bias_band, n_kv)
                y = _residual_matmul(hb, b_w_o, j, xf, alpha, name="mixer_out")
            li = i // 2
            if i % 2 == 0:
                xf, xb = _layer_norm(y, ln1_g[i].astype(F32), ln1_b[i].astype(F32))
                hmid = _swiglu_up(xb, ffn_w1, ffn_w3, li)
                y = _residual_matmul(hmid, ffn_w2, li, xf, alpha, name="ffn_down")
                xf, xb = _layer_norm(y, ln2_g[i].astype(F32), ln2_b[i].astype(F32))
            else:
                xf, xb, gates, idx = _layer_norm(y, ln1_g[i].astype(F32), ln1_b[i].astype(F32),
                                                 moe_router[li].astype(F32))
                row_token, row_gate, pos, tile_expert, tile_valid, ntiles = _routing_tables(
                    gates, idx, moe_tm)
                xg = _moe_gather(xf, row_token, tile_valid, ntiles, moe_tm)
                hmid = _moe_up(xg, moe_w1, moe_w3, li, row_gate, tile_expert, tile_valid, moe_tm)
                yg = _moe_down(hmid, moe_w2, li, tile_expert, tile_valid, moe_tm)
                xf, xb = _moe_combine_ln(yg, pos, xf, ln2_g[i].astype(F32), ln2_b[i].astype(F32),
                                         alpha)
            xf, xb = _ple(xb, xf, p, ple_w_gate, ple_w_proj, i, bi)
            if i == n_a - 1:
                kvh = _head_split_matmul(xb, kv_w, None, 2 * kv_dim, name="kv_proj")
        outs.append(xf)
    return jnp.stack(outs, axis=0).astype(x.dtype)
```

```python
import functools

import numpy as np
import jax
import jax.numpy as jnp
from jax import lax
from jax.experimental import pallas as pl
from jax.experimental.pallas import tpu as pltpu

F32 = jnp.float32
BF16 = jnp.bfloat16

V7X_LANES = 128
V7X_SCOPED_VMEM_BYTES = 60000 * 1024
COMPILER_SCRATCH_BYTES = 6 * 1024 * 1024

MM_ROWS = 2048
MM_COLS = 256
MM_SUB_ROWS = 512

A_HEAD_DIM = 128
A_CHUNK = 64
A_SUB = 16
A_CUMSUM_ROWS = 256
A_SAFE_LOG2_DECAY = 100.0
B_HEAD_DIM = 64
B_LOGIT_SCALE = B_HEAD_DIM ** -0.5
assert B_LOGIT_SCALE == 2.0 ** -3
B_BLOCK = 128
WINDOW = 128
REL_BUCKETS = 32
REL_MAX_DISTANCE = 128
TOP_K = 2
LN_EPS = 1e-5
RMS_EPS = 1e-6
LOG2E = 1.4426950408889634


def _vmem_limit(block_bytes):
    return int(min(block_bytes + COMPILER_SCRATCH_BYTES, V7X_SCOPED_VMEM_BYTES))


def _nbytes(shape, dtype):
    return int(np.prod([n for n in shape if n is not None])) * jnp.dtype(dtype).itemsize


def _silu(x):
    return x * jax.nn.sigmoid(x)


def _mm_body(*refs, nw, ne, no, epilogue):
    x_ref = refs[0]
    w_refs = refs[1:1 + nw]
    extra_refs = refs[1 + nw:1 + nw + ne]
    out_refs = refs[1 + nw + ne:1 + nw + ne + no]
    wbs = [w_ref[...].astype(BF16) for w_ref in w_refs]
    tm = x_ref.shape[0]
    sub = min(tm, MM_SUB_ROWS)
    for r0 in range(0, tm, sub):
        rows = slice(r0, r0 + sub)
        xv = x_ref[rows, :]
        prods = [jnp.dot(xv, wb, preferred_element_type=F32) for wb in wbs]
        epilogue(prods, extra_refs, out_refs, rows)


def _fused_matmul(x, ws, n_cols, extras, outs, epilogue, *, tm, tn, name, tk=None, k_block=0,
                  x_buffers=2):
    m, kdim = x.shape
    tk = tk or kdim
    tm, tn = min(tm, m), min(tn, n_cols)
    assert m % tm == 0 and n_cols % tn == 0 and kdim % tk == 0
    x_mode = {} if x_buffers == 2 else {"pipeline_mode": pl.Buffered(x_buffers)}
    in_specs = [pl.BlockSpec((tm, tk), lambda i, j: (i, k_block), **x_mode)]
    operands = [x]
    vmem = x_buffers * _nbytes((tm, tk), x.dtype)
    for w, layer, off in ws:
        if layer is None:
            spec = pl.BlockSpec((tk, tn), lambda i, j, off=off: (k_block, j + off))
        else:
            spec = pl.BlockSpec((None, tk, tn), lambda i, j, off=off, layer=layer:
                                (layer, k_block, j + off))
        in_specs.append(spec)
        operands.append(w)
        vmem += 2 * _nbytes((tk, tn), w.dtype) + _nbytes((tk, tn), BF16)
    for arr, bshape, imap in extras:
        in_specs.append(pl.BlockSpec(bshape, imap))
        operands.append(arr)
        vmem += 2 * _nbytes(bshape, arr.dtype)
    out_shapes, out_specs = [], []
    for shape, dtype, bshape, imap in outs:
        out_shapes.append(jax.ShapeDtypeStruct(shape, dtype))
        out_specs.append(pl.BlockSpec(bshape, imap))
        vmem += 2 * _nbytes(bshape, dtype)
    vmem += (2 + len(ws)) * _nbytes((tm, tn), F32)
    body = functools.partial(_mm_body, nw=len(ws), ne=len(extras), no=len(outs), epilogue=epilogue)
    return pl.pallas_call(
        body,
        grid=(m // tm, n_cols // tn),
        in_specs=in_specs,
        out_specs=out_specs,
        out_shape=out_shapes,
        compiler_params=pltpu.CompilerParams(dimension_semantics=("parallel", "parallel"),
                                             vmem_limit_bytes=_vmem_limit(vmem)),
        name=name,
    )(*operands)


def _tile(tm, tn):
    return (tm, tn), (lambda i, j: (i, j))


def _ln_rows(y, g, b):
    mu = jnp.mean(y, axis=-1, keepdims=True)
    yc = y - mu
    var = jnp.mean(yc * yc, axis=-1, keepdims=True)
    return yc * lax.rsqrt(var + LN_EPS) * g + b


def _route_top2(x, w_router):
    logits = jnp.dot(x, w_router, preferred_element_type=F32, precision=lax.Precision.HIGHEST)
    ne = logits.shape[1]
    eid = lax.broadcasted_iota(jnp.int32, logits.shape, 1)
    m1 = jnp.max(logits, axis=-1, keepdims=True)
    i1 = jnp.min(jnp.where(logits == m1, eid, ne), axis=-1, keepdims=True)
    rest = jnp.where(eid == i1, -jnp.inf, logits)
    m2 = jnp.max(rest, axis=-1, keepdims=True)
    i2 = jnp.min(jnp.where(rest == m2, eid, ne), axis=-1, keepdims=True)
    e2 = jnp.exp(m2 - m1)
    w_top1 = 1.0 / (1.0 + e2)
    w_top2 = e2 / (1.0 + e2)
    gates = jnp.where(eid == i1, w_top1, 0.0) + jnp.where(eid == i2, w_top2, 0.0)
    return gates, jnp.where(eid == 0, i1, i2)[:, :TOP_K]


def _ln_body(*refs, route):
    if route:
        y_ref, g_ref, b_ref, wr_ref, o32_ref, o16_ref, gates_ref, idx_ref = refs
    else:
        y_ref, g_ref, b_ref, o32_ref, o16_ref = refs
    out = _ln_rows(y_ref[...], g_ref[...], b_ref[...])
    o32_ref[...] = out
    o16_ref[...] = out.astype(BF16)
    if route:
        gates_ref[...], idx_ref[...] = _route_top2(out, wr_ref[...])


def _layer_norm(y, g, b, w_router=None, *, tr=256):
    s, d = y.shape
    tr = min(tr, s)
    assert s % tr == 0
    row = pl.BlockSpec((tr, d), lambda i: (i, 0))
    vec = pl.BlockSpec((1, d), lambda i: (0, 0))
    in_specs, operands = [row, vec, vec], [y, g.reshape(1, d), b.reshape(1, d)]
    out_specs = [row, row]
    out_shape = [jax.ShapeDtypeStruct((s, d), F32), jax.ShapeDtypeStruct((s, d), BF16)]
    vmem = 2 * (2 * _nbytes((tr, d), F32) + _nbytes((tr, d), BF16)) + 2 * _nbytes((tr, d), F32)
    if w_router is not None:
        ne = w_router.shape[1]
        in_specs.append(pl.BlockSpec((d, ne), lambda i: (0, 0)))
        operands.append(w_router)
        out_specs += [pl.BlockSpec((tr, ne), lambda i: (i, 0)),
                      pl.BlockSpec((tr, TOP_K), lambda i: (i, 0))]
        out_shape += [jax.ShapeDtypeStruct((s, ne), F32), jax.ShapeDtypeStruct((s, TOP_K), jnp.int32)]
        vmem += 2 * _nbytes((d, V7X_LANES), F32) + 4 * _nbytes((tr, d), F32)
    return pl.pallas_call(
        functools.partial(_ln_body, route=w_router is not None),
        grid=(s // tr,),
        in_specs=in_specs,
        out_specs=out_specs,
        out_shape=out_shape,
        compiler_params=pltpu.CompilerParams(dimension_semantics=("parallel",),
                                             vmem_limit_bytes=_vmem_limit(vmem)),
        name="layer_norm_route" if w_router is not None else "layer_norm",
    )(*operands)


def _lower_bound(lbl, layer):
    mx = jnp.max(lbl, axis=0, keepdims=True)
    e = jnp.exp(lbl - mx)
    return jnp.sum(e[:layer + 1], axis=0, keepdims=True) / jnp.sum(e, axis=0, keepdims=True)


def _hgrn_in_proj(xb, w_in, lb_logits, layer, *, tm=MM_ROWS, tn=MM_COLS):
    s, d = xb.shape
    nseg = d // min(tn, d)
    blk, imap = _tile(min(tm, s), min(tn, d))

    def ep_silu(accs, extras, outs, rows):
        outs[0][rows, :] = _silu(accs[0]).astype(BF16)

    def ep_id(accs, extras, outs, rows):
        outs[0][rows, :] = accs[0].astype(BF16)

    def ep_forget(accs, extras, outs, rows):
        lb = _lower_bound(extras[0][...], layer)
        f_raw = accs[0]
        e = jnp.exp(-jnp.abs(f_raw))
        log_sig = jnp.minimum(f_raw, 0.0) - jnp.log(1.0 + e)
        sig_neg = jnp.where(f_raw >= 0.0, e, 1.0) / (1.0 + e)
        a = jnp.log(lb)
        b = jnp.log1p(-lb) + log_sig
        log_f = jnp.maximum(a, b) + jnp.log(1.0 + jnp.exp(-jnp.abs(a - b)))
        outs[0][rows, :] = log_f
        outs[1][rows, :] = ((1.0 - lb) * sig_neg).astype(BF16)
        nrow, ncol = log_f.shape
        outs[2][rows.start // A_CHUNK:rows.stop // A_CHUNK, :] = jnp.sum(
            log_f.reshape(nrow // A_CHUNK, A_CHUNK, ncol), axis=1)

    def call(seg, epilogue, extras, out_dtypes, name, more_outs=(), x_buffers=2):
        outs = [((s, d), dt, blk, imap) for dt in out_dtypes] + list(more_outs)
        return _fused_matmul(xb, [(w_in, layer, seg * nseg)], d, extras, outs, epilogue,
                             tm=tm, tn=tn, name=name, x_buffers=x_buffers)

    nslot = lb_logits.shape[0]
    lb_extra = [(lb_logits, (nslot, blk[1]), lambda i, j: (0, j))]
    decay_out = ((s // A_CHUNK, d), F32, (blk[0] // A_CHUNK, blk[1]), imap)
    (q,) = call(0, ep_silu, [], [BF16], "hgrn_q")
    log_f, k, chunk_decay = call(1, ep_forget, lb_extra, [F32, BF16], "hgrn_f",
                                 more_outs=[decay_out], x_buffers=1)
    (v,) = call(2, ep_id, [], [BF16], "hgrn_v")
    (g,) = call(3, ep_silu, [], [BF16], "hgrn_g")
    return q, k, v, log_f, g, chunk_decay


def _chunk_cumsum(lf_ref, tc):
    tg = min(tc, A_CUMSUM_ROWS)
    row = lax.broadcasted_iota(jnp.int32, (tg, tg), 0)
    col = lax.broadcasted_iota(jnp.int32, (tg, tg), 1)
    tri = jnp.where((col <= row) & (col // A_CHUNK == row // A_CHUNK), 1.0, 0.0).astype(BF16)
    cums = []
    for gi in range(tc // tg):
        lf = lf_ref[gi * tg:(gi + 1) * tg, :]
        hi = lf.astype(BF16)
        r1 = lf - hi.astype(F32)
        mid = r1.astype(BF16)
        lo = (r1 - mid.astype(F32)).astype(BF16)
        cums.append(jnp.dot(tri, hi, preferred_element_type=F32)
                    + jnp.dot(tri, mid, preferred_element_type=F32)
                    + jnp.dot(tri, lo, preferred_element_type=F32))
    return jnp.concatenate(cums, axis=0)


def _hgrn_body(bounded_ref, q_ref, k_ref, v_ref, lf_ref, g_ref, og_ref, o_ref, st_ref, *, tc):
    @pl.when(pl.program_id(1) == 0)
    def _():
        st_ref[...] = jnp.zeros_like(st_ref)

    out_g = og_ref[...]
    half = A_SUB // 2
    lane = lax.broadcasted_iota(jnp.int32, (half, A_HEAD_DIM), 1)
    hrow = lax.broadcasted_iota(jnp.int32, (half, A_HEAD_DIM), 0)
    nsub = A_CHUNK // A_SUB
    nchunk = tc // A_CHUNK

    def finish(rows, o):
        o = o * lax.rsqrt(jnp.mean(o * o, axis=-1, keepdims=True) + RMS_EPS)
        o_ref[rows, :] = (o * out_g * g_ref[rows, :].astype(F32)).astype(BF16)

    def inter_chunk(ci, cum, st):
        rows = slice(ci * A_CHUNK, (ci + 1) * A_CHUNK)
        c2 = cum[rows] * LOG2E
        qc = q_ref[rows, :].astype(F32)
        kc = k_ref[rows, :].astype(F32)
        vc = v_ref[rows, :]
        last = c2[A_CHUNK - 1:A_CHUNK]
        qe = (qc * jnp.exp2(c2)).astype(BF16)
        o_inter = lax.dot_general(qe, st.astype(BF16), (((1,), (1,)), ((), ())),
                                  preferred_element_type=F32)
        kd = (kc * jnp.exp2(last - c2)).astype(BF16)
        upd = lax.dot_general(vc, kd, (((0,), (0,)), ((), ())), preferred_element_type=F32)
        return rows, c2, qc, kc, vc, qe, o_inter, st * jnp.exp2(last) + upd

    def run_factorised():
        c2 = _chunk_cumsum(lf_ref, tc) * LOG2E
        qe = (q_ref[...].astype(F32) * jnp.exp2(c2)).astype(BF16)
        kb32 = k_ref[...].astype(F32) * jnp.exp2(-c2)
        kb = kb32.astype(BF16)
        t_i = lax.broadcasted_iota(jnp.int32, (A_CHUNK, A_CHUNK), 0)
        s_i = lax.broadcasted_iota(jnp.int32, (A_CHUNK, A_CHUNK), 1)
        causal = t_i >= s_i
        st = st_ref[...]
        outs = []
        for ci in range(nchunk):
            rows = slice(ci * A_CHUNK, (ci + 1) * A_CHUNK)
            vc = v_ref[rows, :]
            grow = jnp.exp2(c2[(ci + 1) * A_CHUNK - 1:(ci + 1) * A_CHUNK])
            o_inter = lax.dot_general(qe[rows], st.astype(BF16), (((1,), (1,)), ((), ())),
                                      preferred_element_type=F32)
            kd = (kb32[rows] * grow).astype(BF16)
            upd = lax.dot_general(vc, kd, (((0,), (0,)), ((), ())), preferred_element_type=F32)
            st = st * grow + upd
            sc = lax.dot_general(qe[rows], kb[rows], (((1,), (1,)), ((), ())),
                                 preferred_element_type=F32)
            scores = jnp.where(causal, sc, 0.0).astype(BF16)
            outs.append(o_inter + jnp.dot(scores, vc, preferred_element_type=F32))
        st_ref[...] = st
        finish(slice(0, tc), jnp.concatenate(outs, axis=0))

    def direct_chunk(ci, cum, st):
        rows, c2, qc, kc, vc, qe, o_inter, st = inter_chunk(ci, cum, st)

        p_rows = []
        k_blocks, prev_ref = [], None
        for si in range(nsub):
            lo_r = si * A_SUB
            cs, qs, ks = c2[lo_r:lo_r + A_SUB], qc[lo_r:lo_r + A_SUB], kc[lo_r:lo_r + A_SUB]
            off = None
            if si > 0:
                ref = c2[lo_r - 1:lo_r]
                if prev_ref is not None:
                    rebase = jnp.exp2(ref - prev_ref)
                    k_blocks = [kb * rebase for kb in k_blocks]
                k_blocks.append(kc[lo_r - A_SUB:lo_r] * jnp.exp2(ref - c2[lo_r - A_SUB:lo_r]))
                prev_ref = ref
                k_t = jnp.concatenate(
                    k_blocks + [jnp.zeros((A_CHUNK - lo_r, A_HEAD_DIM), F32)], axis=0)
                q_t = (qs * jnp.exp2(cs - ref)).astype(BF16)
                off = lax.dot_general(q_t, k_t.astype(BF16), (((1,), (1,)), ((), ())),
                                      preferred_element_type=F32)
            c_h = (cs[:half], cs[half:])
            q_h = (qs[:half], qs[half:])
            p_h = [jnp.zeros((half, A_HEAD_DIM), F32), jnp.zeros((half, A_HEAD_DIM), F32)]
            for sj in range(A_SUB):
                for hh in range(sj // half, 2):
                    decay = jnp.exp2(c_h[hh] - cs[sj:sj + 1])
                    colv = jnp.sum(q_h[hh] * (ks[sj:sj + 1] * decay), axis=1, keepdims=True)
                    sel = lane == lo_r + sj
                    if sj // half == hh:
                        sel = sel & (hrow >= sj - hh * half)
                    p_h[hh] = jnp.where(sel, colv, p_h[hh])
            for hh in range(2):
                ph = p_h[hh][:, :A_CHUNK]
                p_rows.append(ph if off is None else ph + off[hh * half:(hh + 1) * half])
        scores = jnp.concatenate(p_rows, axis=0).astype(BF16)
        finish(rows, o_inter + jnp.dot(scores, vc, preferred_element_type=F32))
        return st

    def run_direct():
        cum = _chunk_cumsum(lf_ref, tc)
        st = st_ref[...]
        for ci in range(nchunk):
            st = direct_chunk(ci, cum, st)
        st_ref[...] = st

    bounded = bounded_ref[pl.program_id(0), pl.program_id(1)] > 0
    pl.when(bounded)(run_factorised)
    pl.when(jnp.logical_not(bounded))(run_direct)


def _hgrn_core(q, k, v, log_f, g, out_g, chunk_decay, *, tc=512):
    s, d = q.shape
    tc = min(tc, s)
    assert s % tc == 0 and tc % A_CHUNK == 0 and d % A_HEAD_DIM == 0
    nhead, nblk = d // A_HEAD_DIM, s // tc
    worst = jnp.min(chunk_decay.reshape(nblk, tc // A_CHUNK, nhead, A_HEAD_DIM), axis=(1, 3))
    bounded = (worst.T * LOG2E >= -A_SAFE_LOG2_DECAY).astype(jnp.int32)
    blk = pl.BlockSpec((tc, A_HEAD_DIM), lambda h, c, flags: (c, h))
    vmem = 2 * (4 * _nbytes((tc, A_HEAD_DIM), BF16) + 2 * _nbytes((tc, A_HEAD_DIM), F32)) \
        + 8 * _nbytes((A_CUMSUM_ROWS, A_CUMSUM_ROWS), F32) + 8 * _nbytes((tc, A_HEAD_DIM), F32)
    grid_spec = pltpu.PrefetchScalarGridSpec(
        num_scalar_prefetch=1,
        grid=(nhead, nblk),
        in_specs=[blk, blk, blk, blk, blk,
                  pl.BlockSpec((1, A_HEAD_DIM), lambda h, c, flags: (0, h))],
        out_specs=blk,
        scratch_shapes=[pltpu.VMEM((A_HEAD_DIM, A_HEAD_DIM), F32)],
    )
    return pl.pallas_call(
        functools.partial(_hgrn_body, tc=tc),
        grid_spec=grid_spec,
        out_shape=jax.ShapeDtypeStruct((s, d), BF16),
        compiler_params=pltpu.CompilerParams(dimension_semantics=("parallel", "arbitrary"),
                                             vmem_limit_bytes=_vmem_limit(vmem)),
        name="hgrn_core",
    )(bounded, q, k, v, log_f, g, out_g.reshape(1, d))


def _residual_matmul(hb, w, layer, resid, alpha, *, name, tm=MM_ROWS, tn=MM_COLS):
    s, d = resid.shape
    kdim = hb.shape[1]
    tk = min(kdim, d)
    blk, imap = _tile(min(tm, s), min(tn, d))
    y = resid
    for kb in range(kdim // tk):
        scale = alpha if kb == 0 else 1.0

        def ep(accs, extras, outs, rows, scale=scale):
            outs[0][rows, :] = scale * extras[0][rows, :] + accs[0]

        (y,) = _fused_matmul(hb, [(w, layer, 0)], d, [(y, blk, imap)], [((s, d), F32, blk, imap)],
                             ep, tm=tm, tn=tn, tk=tk, k_block=kb, name=name, x_buffers=1)
    return y


def _rel_bucket_band():
    i = np.arange(B_BLOCK)[:, None]
    j = np.arange(2 * B_BLOCK)[None, :]
    d = np.clip(B_BLOCK + i - j, 0, None)
    max_exact = REL_BUCKETS // 2
    large = max_exact + (np.log(np.maximum(d, 1) / max_exact)
                         / np.log(REL_MAX_DISTANCE / max_exact)
                         * (REL_BUCKETS - max_exact)).astype(np.int32)
    large = np.minimum(large, REL_BUCKETS - 1)
    return np.where(d < max_exact, d, large).astype(np.int32)


def _bias_body(rb_ref, bucket_ref, o_ref, *, heads_per_step):
    h0 = pl.program_id(0) * heads_per_step
    bucket = bucket_ref[...]
    qi = lax.broadcasted_iota(jnp.int32, bucket.shape, 0)
    kj = lax.broadcasted_iota(jnp.int32, bucket.shape, 1)
    dist = B_BLOCK + qi - kj
    in_window = (dist >= 0) & (dist < WINDOW)
    for hh in range(heads_per_step):
        acc = jnp.zeros(bucket.shape, F32)
        for b in range(REL_BUCKETS):
            acc = jnp.where(bucket == b, rb_ref[b, h0 + hh], acc)
        o_ref[hh] = jnp.where(in_window, acc, -jnp.inf)


def _rel_bias_band(rel_bias, *, heads_per_step=8):
    nh = rel_bias.shape[1]
    bucket = jnp.asarray(_rel_bucket_band())
    return pl.pallas_call(
        functools.partial(_bias_body, heads_per_step=heads_per_step),
        grid=(nh // heads_per_step,),
        in_specs=[pl.BlockSpec(memory_space=pltpu.SMEM),
                  pl.BlockSpec((B_BLOCK, 2 * B_BLOCK), lambda i: (0, 0))],
        out_specs=pl.BlockSpec((heads_per_step, B_BLOCK, 2 * B_BLOCK), lambda i: (i, 0, 0)),
        out_shape=jax.ShapeDtypeStruct((nh, B_BLOCK, 2 * B_BLOCK), F32),
        compiler_params=pltpu.CompilerParams(dimension_semantics=("parallel",)),
        name="rel_bias_band",
    )(rel_bias, bucket)


def _head_split_matmul(xb, w, layer, n_cols, *, name, tm=MM_ROWS, tn=MM_COLS):
    s, kdim = xb.shape
    tm, tn = min(tm, s), min(tn, n_cols)
    hpt = tn // B_HEAD_DIM

    def ep(accs, extras, outs, rows):
        for c in range(hpt):
            outs[0][c, rows, :] = accs[0][:, c * B_HEAD_DIM:(c + 1) * B_HEAD_DIM].astype(BF16)

    outs = [((n_cols // B_HEAD_DIM, s, B_HEAD_DIM), BF16, (hpt, tm, B_HEAD_DIM),
             lambda i, j: (j, i, 0))]
    (out,) = _fused_matmul(xb, [(w, layer, 0)], n_cols, [], outs, ep, tm=tm, tn=tn, name=name)
    return out


def _attn_body(sink_ref, q_ref, kp_ref, kc_ref, vp_ref, vc_ref, bias_ref, o_ref, *, group, qb):
    h = pl.program_id(0)
    m_blk = pl.program_id(1)
    c = B_BLOCK
    k_all = jnp.concatenate([kp_ref[0], kc_ref[0]], axis=0)
    v_all = jnp.concatenate([vp_ref[0], vc_ref[0]], axis=0)
    kj = lax.broadcasted_iota(jnp.int32, (c, 2 * c), 1)
    for b in range(qb):
        q = q_ref[:, b * c:(b + 1) * c, :].reshape(group * c, B_HEAD_DIM) * B_LOGIT_SCALE
        kb = k_all[b * c:(b + 2) * c]
        vb = v_all[b * c:(b + 2) * c]
        logits = lax.dot_general(q, kb, (((1,), (1,)), ((), ())), preferred_element_type=F32)
        probs = []
        for gi in range(group):
            lg = logits[gi * c:(gi + 1) * c] + bias_ref[gi]
            if b == 0:
                lg = jnp.where((m_blk > 0) | (kj >= c), lg, -jnp.inf)
            sink = sink_ref[h * group + gi]
            m = jnp.maximum(jnp.max(lg, axis=-1, keepdims=True), sink)
            e = jnp.exp(lg - m)
            denom = jnp.sum(e, axis=-1, keepdims=True) + jnp.exp(sink - m)
            probs.append((e / denom).astype(BF16))
        pv = jnp.dot(jnp.concatenate(probs, axis=0), vb, preferred_element_type=F32)
        o_ref[b * c:(b + 1) * c, :] = jnp.concatenate(
            [pv[gi * c:(gi + 1) * c] for gi in range(group)], axis=1).astype(BF16)


def _swa_core(qh, kvh, sinks, bias, n_kv, *, qb=4):
    nq, s, hd = qh.shape
    group = nq // n_kv
    nb = s // B_BLOCK
    qb = min(qb, nb)
    assert nb % qb == 0
    prev = lambda m: jnp.maximum(m * qb - 1, 0)
    prev_blk = (1, B_BLOCK, hd)
    cur_blk = (1, qb * B_BLOCK, hd)
    in_specs = [
        pl.BlockSpec(memory_space=pltpu.SMEM),
        pl.BlockSpec((group, qb * B_BLOCK, hd), lambda h, m: (h, m, 0)),
        pl.BlockSpec(prev_blk, lambda h, m: (h, prev(m), 0)),
        pl.BlockSpec(cur_blk, lambda h, m: (h, m, 0)),
        pl.BlockSpec(prev_blk, lambda h, m: (h + n_kv, prev(m), 0)),
        pl.BlockSpec(cur_blk, lambda h, m: (h + n_kv, m, 0)),
        pl.BlockSpec((group, B_BLOCK, 2 * B_BLOCK), lambda h, m: (h, 0, 0)),
    ]
    vmem = 2 * (_nbytes((group, B_BLOCK, 2 * B_BLOCK), F32)
                + (group + 3) * _nbytes((qb * B_BLOCK, V7X_LANES), BF16)
                + _nbytes((qb * B_BLOCK, group * hd), BF16)) \
        + 6 * qb * _nbytes((group * B_BLOCK, 2 * B_BLOCK), F32)
    return pl.pallas_call(
        functools.partial(_attn_body, group=group, qb=qb),
        grid=(n_kv, nb // qb),
        in_specs=in_specs,
        out_specs=pl.BlockSpec((qb * B_BLOCK, group * hd), lambda h, m: (m, h)),
        out_shape=jax.ShapeDtypeStruct((s, nq * hd), BF16),
        compiler_params=pltpu.CompilerParams(dimension_semantics=("parallel", "parallel"),
                                             vmem_limit_bytes=_vmem_limit(vmem)),
        name="swa_core",
    )(sinks, qh, kvh, kvh, kvh, kvh, bias)


def _swiglu_up(xb, w1, w3, layer, *, tm=MM_ROWS, tn=MM_COLS):
    s, d = xb.shape
    f = w1.shape[2]
    blk, imap = _tile(min(tm, s), min(tn, f))

    def ep(accs, extras, outs, rows):
        outs[0][rows, :] = (_silu(accs[0]) * accs[1]).astype(BF16)

    (h,) = _fused_matmul(xb, [(w1, layer, 0), (w3, layer, 0)], f, [],
                         [((s, f), BF16, blk, imap)], ep, tm=tm, tn=tn, name="swiglu_up",
                         x_buffers=1)
    return h


def _routing_tables(gates, idx, tm):
    s, ne = gates.shape
    n_assign = s * TOP_K
    ntiles = n_assign // tm + ne
    e_flat = idx.reshape(n_assign)
    onehot = (e_flat[:, None] == jnp.arange(ne, dtype=jnp.int32)[None, :]).astype(jnp.int32)
    rank = jnp.take_along_axis(jnp.cumsum(onehot, axis=0) - onehot, e_flat[:, None], axis=1)[:, 0]
    counts = jnp.sum(onehot, axis=0)
    padded = ((counts + tm - 1) // tm) * tm
    ends = jnp.cumsum(padded)
    pos = (ends - padded)[e_flat] + rank
    token = jnp.arange(n_assign, dtype=jnp.int32) // TOP_K
    gate = jnp.take_along_axis(gates, idx, axis=1).reshape(n_assign)
    row_token = jnp.zeros((ntiles * tm,), jnp.int32).at[pos].set(token)
    row_gate = jnp.zeros((ntiles * tm,), F32).at[pos].set(gate)
    tile_start = jnp.arange(ntiles, dtype=jnp.int32) * tm
    tile_expert = jnp.minimum(jnp.searchsorted(ends, tile_start, side="right"), ne - 1).astype(jnp.int32)
    tile_valid = (tile_start < ends[-1]).astype(jnp.int32)
    return row_token, row_gate, pos.reshape(s, TOP_K).astype(jnp.int32), tile_expert, tile_valid, ntiles


GATHER_ISSUE_UNROLL = 8


def _gather_body(tv_ref, tok_ref, tok_next_ref, x_hbm, o_ref, buf, sem, *, tm, ntiles):
    i = pl.program_id(0)
    slot = i % 2

    def row_copy(tile_tok_ref, slot_, r):
        return pltpu.make_async_copy(x_hbm.at[pl.ds(tile_tok_ref[0, 0, r], 1)],
                                     buf.at[slot_, pl.ds(r, 1)], sem.at[slot_])

    def start_tile(tile_tok_ref, slot_):
        def body(r, carry):
            row_copy(tile_tok_ref, slot_, r).start()
            return carry
        lax.fori_loop(0, tm, body, 0, unroll=GATHER_ISSUE_UNROLL)

    @pl.when((i == 0) & (tv_ref[0] > 0))
    def _():
        start_tile(tok_ref, slot)

    @pl.when((i + 1 < ntiles) & (tv_ref[jnp.minimum(i + 1, ntiles - 1)] > 0))
    def _():
        start_tile(tok_next_ref, 1 - slot)

    @pl.when(tv_ref[i] > 0)
    def _():
        def wait_body(r, carry):
            row_copy(tok_ref, slot, r).wait()
            return carry
        lax.fori_loop(0, tm, wait_body, 0, unroll=GATHER_ISSUE_UNROLL)
        o_ref[...] = buf[slot].astype(BF16)

    @pl.when(tv_ref[i] == 0)
    def _():
        o_ref[...] = jnp.zeros_like(o_ref)


def _moe_gather(x, row_token, tile_valid, ntiles, tm):
    s, d = x.shape
    vmem = 2 * _nbytes((tm, d), F32) + 2 * _nbytes((tm, d), BF16) + _nbytes((tm, d), F32)
    tok = row_token.reshape(ntiles, 1, tm)
    grid_spec = pltpu.PrefetchScalarGridSpec(
        num_scalar_prefetch=1,
        grid=(ntiles,),
        in_specs=[pl.BlockSpec((1, 1, tm), lambda i, tv: (i, 0, 0), memory_space=pltpu.SMEM),
                  pl.BlockSpec((1, 1, tm), lambda i, tv: (jnp.minimum(i + 1, ntiles - 1), 0, 0),
                               memory_space=pltpu.SMEM),
                  pl.BlockSpec(memory_space=pl.ANY)],
        out_specs=pl.BlockSpec((tm, d), lambda i, tv: (i, 0)),
        scratch_shapes=[pltpu.VMEM((2, tm, d), F32), pltpu.SemaphoreType.DMA((2,))],
    )
    return pl.pallas_call(
        functools.partial(_gather_body, tm=tm, ntiles=ntiles),
        grid_spec=grid_spec,
        out_shape=jax.ShapeDtypeStruct((ntiles * tm, d), BF16),
        compiler_params=pltpu.CompilerParams(dimension_semantics=("arbitrary",),
                                             vmem_limit_bytes=_vmem_limit(vmem)),
        name="moe_gather",
    )(tile_valid, tok, tok, x)


def _expert_changed(te_ref, i):
    return (i == 0) | (te_ref[i] != te_ref[jnp.maximum(i - 1, 0)])


def _moe_up_body(te_ref, tv_ref, x_ref, w1_ref, w3_ref, gate_ref, o_ref, wb1, wb3):
    i = pl.program_id(1)

    @pl.when(_expert_changed(te_ref, i))
    def _():
        wb1[...] = w1_ref[...].astype(BF16)
        wb3[...] = w3_ref[...].astype(BF16)

    @pl.when(tv_ref[i] > 0)
    def _():
        xv = x_ref[...]
        a = jnp.dot(xv, wb1[...], preferred_element_type=F32)
        b = jnp.dot(xv, wb3[...], preferred_element_type=F32)
        o_ref[...] = (_silu(a) * b * gate_ref[...]).astype(BF16)

    @pl.when(tv_ref[i] == 0)
    def _():
        o_ref[...] = jnp.zeros_like(o_ref)


def _moe_up(xg, w1, w3, layer, row_gate, tile_expert, tile_valid, tm, *, tf=512):
    rows, d = xg.shape
    f = w1.shape[3]
    tf = min(tf, f)
    ntiles = rows // tm
    wspec = pl.BlockSpec((None, None, d, tf), lambda j, i, te, tv: (layer, te[i], 0, j))
    vmem = 2 * (_nbytes((tm, d), BF16) + 2 * _nbytes((d, tf), F32) + _nbytes((tm, tf), BF16)) \
        + 2 * _nbytes((d, tf), BF16) + 4 * _nbytes((tm, tf), F32)
    grid_spec = pltpu.PrefetchScalarGridSpec(
        num_scalar_prefetch=2,
        grid=(f // tf, ntiles),
        in_specs=[pl.BlockSpec((tm, d), lambda j, i, te, tv: (i, 0)), wspec, wspec,
                  pl.BlockSpec((tm, 1), lambda j, i, te, tv: (i, 0))],
        out_specs=pl.BlockSpec((tm, tf), lambda j, i, te, tv: (i, j)),
        scratch_shapes=[pltpu.VMEM((d, tf), BF16), pltpu.VMEM((d, tf), BF16)],
    )
    return pl.pallas_call(
        _moe_up_body,
        grid_spec=grid_spec,
        out_shape=jax.ShapeDtypeStruct((rows, f), BF16),
        compiler_params=pltpu.CompilerParams(dimension_semantics=("parallel", "arbitrary"),
                                             vmem_limit_bytes=_vmem_limit(vmem)),
        name="moe_up",
    )(tile_expert, tile_valid, xg, w1, w3, row_gate.reshape(rows, 1))


def _moe_down_body(te_ref, tv_ref, h_ref, w_ref, o_ref, wb):
    i = pl.program_id(1)

    @pl.when(_expert_changed(te_ref, i))
    def _():
        wb[...] = w_ref[...].astype(BF16)

    @pl.when(tv_ref[i] > 0)
    def _():
        o_ref[...] = jnp.dot(h_ref[...], wb[...], preferred_element_type=F32)

    @pl.when(tv_ref[i] == 0)
    def _():
        o_ref[...] = jnp.zeros_like(o_ref)


def _moe_down(h, w2, layer, tile_expert, tile_valid, tm, *, tn=2048):
    rows, f = h.shape
    d = w2.shape[3]
    tn = min(tn, d)
    ntiles = rows // tm
    vmem = 2 * (_nbytes((tm, f), BF16) + _nbytes((f, tn), F32) + _nbytes((tm, tn), F32)) \
        + _nbytes((f, tn), BF16) + 2 * _nbytes((tm, tn), F32)
    grid_spec = pltpu.PrefetchScalarGridSpec(
        num_scalar_prefetch=2,
        grid=(d // tn, ntiles),
        in_specs=[pl.BlockSpec((tm, f), lambda j, i, te, tv: (i, 0)),
                  pl.BlockSpec((None, None, f, tn), lambda j, i, te, tv: (layer, te[i], 0, j))],
        out_specs=pl.BlockSpec((tm, tn), lambda j, i, te, tv: (i, j)),
        scratch_shapes=[pltpu.VMEM((f, tn), BF16)],
    )
    return pl.pallas_call(
        _moe_down_body,
        grid_spec=grid_spec,
        out_shape=jax.ShapeDtypeStruct((rows, d), F32),
        compiler_params=pltpu.CompilerParams(dimension_semantics=("parallel", "arbitrary"),
                                             vmem_limit_bytes=_vmem_limit(vmem)),
        name="moe_down",
    )(tile_expert, tile_valid, h, w2)


def _combine_body(pos_ref, pos_next_ref, y_hbm, x_ref, g_ref, b_ref, o32_ref, o16_ref, buf, sem,
                  *, tr, nt, alpha):
    i = pl.program_id(0)
    slot = i % 2

    def row_copy(tile_pos_ref, slot_, r, kk):
        return pltpu.make_async_copy(y_hbm.at[pl.ds(tile_pos_ref[0, kk, r], 1)],
                                     buf.at[slot_, kk, pl.ds(r, 1)], sem.at[slot_])

    def start_tile(tile_pos_ref, slot_):
        def body(r, carry):
            for kk in range(TOP_K):
                row_copy(tile_pos_ref, slot_, r, kk).start()
            return carry
        lax.fori_loop(0, tr, body, 0, unroll=GATHER_ISSUE_UNROLL)

    @pl.when(i == 0)
    def _():
        start_tile(pos_ref, slot)

    @pl.when(i + 1 < nt)
    def _():
        start_tile(pos_next_ref, 1 - slot)

    def wait_body(r, carry):
        for kk in range(TOP_K):
            row_copy(pos_ref, slot, r, kk).wait()
        return carry
    lax.fori_loop(0, tr, wait_body, 0, unroll=GATHER_ISSUE_UNROLL)

    f = buf[slot, 0]
    for kk in range(1, TOP_K):
        f = f + buf[slot, kk]
    out = _ln_rows(alpha * x_ref[...] + f, g_ref[...], b_ref[...])
    o32_ref[...] = out
    o16_ref[...] = out.astype(BF16)


def _moe_combine_ln(y, pos, x, g, b, alpha, *, tr=256):
    s, d = x.shape
    tr = min(tr, s)
    nt = s // tr
    pos_t = pos.reshape(nt, tr, TOP_K).transpose(0, 2, 1)
    row = pl.BlockSpec((tr, d), lambda i: (i, 0))
    vec = pl.BlockSpec((1, d), lambda i: (0, 0))
    vmem = 2 * TOP_K * _nbytes((tr, d), F32) \
        + 2 * (2 * _nbytes((tr, d), F32) + _nbytes((tr, d), BF16)) + 3 * _nbytes((tr, d), F32)
    pos_blk = (1, TOP_K, tr)
    return pl.pallas_call(
        functools.partial(_combine_body, tr=tr, nt=nt, alpha=alpha),
        grid=(nt,),
        in_specs=[pl.BlockSpec(pos_blk, lambda i: (i, 0, 0), memory_space=pltpu.SMEM),
                  pl.BlockSpec(pos_blk, lambda i: (jnp.minimum(i + 1, nt - 1), 0, 0),
                               memory_space=pltpu.SMEM),
                  pl.BlockSpec(memory_space=pl.ANY), row, vec, vec],
        out_specs=[row, row],
        out_shape=[jax.ShapeDtypeStruct((s, d), F32), jax.ShapeDtypeStruct((s, d), BF16)],
        scratch_shapes=[pltpu.VMEM((2, TOP_K, tr, d), F32), pltpu.SemaphoreType.DMA((2,))],
        compiler_params=pltpu.CompilerParams(dimension_semantics=("arbitrary",),
                                             vmem_limit_bytes=_vmem_limit(vmem)),
        name="moe_combine_ln",
    )(pos_t, pos_t, y, x, g.reshape(1, d), b.reshape(1, d))


def _ple(xb, x, p, w_gate, w_proj, layer, bi, *, tm=MM_ROWS, tn=MM_COLS):
    s, d = x.shape
    pdim = p.shape[-1]
    blk, imap = _tile(min(tm, s), min(tn, d))

    def ep(accs, extras, outs, rows):
        proj = jnp.dot(extras[1][rows, :].astype(BF16), extras[2][...].astype(BF16),
                       preferred_element_type=F32)
        out = extras[0][rows, :] + jax.nn.sigmoid(accs[0]) * proj
        outs[0][rows, :] = out
        outs[1][rows, :] = out.astype(BF16)

    extras = [(x, blk, imap),
              (p, (None, None, blk[0], pdim), lambda i, j: (layer, bi, i, 0)),
              (w_proj, (None, pdim, blk[1]), lambda i, j: (layer, 0, j))]
    outs = [((s, d), F32, blk, imap), ((s, d), BF16, blk, imap)]
    return _fused_matmul(xb, [(w_gate, layer, 0)], d, extras, outs, ep, tm=tm, tn=tn, name="ple",
                         x_buffers=1)


MOE_TILE_ROWS = 512


def kernel(x, p, ln1_g, ln1_b, ln2_g, ln2_b, a_w_in, a_lb_logits, a_out_g, a_w_o, kv_w, b_w_q, b_sinks, b_w_o, rel_bias, ffn_w1, ffn_w3, ffn_w2, moe_router, moe_w1, moe_w3, moe_w2, ple_w_proj, ple_w_gate):
    batch, s, d = x.shape
    depth = p.shape[0]
    n_a = a_w_in.shape[0]
    kv_dim = kv_w.shape[1] // 2
    n_kv = kv_dim // B_HEAD_DIM
    alpha = (2.0 * depth) ** 0.25
    moe_tm = min(MOE_TILE_ROWS, s)
    bias_band = _rel_bias_band(rel_bias.astype(F32))
    lb_logits = a_lb_logits.astype(F32)

    outs = []
    for bi in range(batch):
        xf = x[bi].astype(F32)
        xb = xf.astype(BF16)
        kvh = None
        for i in range(depth):
            if i < n_a:
                q, k, v, log_f, g, chunk_decay = _hgrn_in_proj(xb, a_w_in, lb_logits, i)
                hb = _hgrn_core(q, k, v, log_f, g, a_out_g[i].astype(F32), chunk_decay)
                y = _residual_matmul(hb, a_w_o, i, xf, alpha, name="mixer_out")
            else:
                j = i - n_a
                qh = _head_split_matmul(xb, b_w_q, j, d, name="swa_q")
                hb = _swa_core(qh, kvh, b_sinks[j].astype(F32), bias_band, n_kv)
                y = _residual_matmul(hb, b_w_o, j, xf, alpha, name="mixer_out")
            li = i // 2
            if i % 2 == 0:
                xf, xb = _layer_norm(y, ln1_g[i].astype(F32), ln1_b[i].astype(F32))
                hmid = _swiglu_up(xb, ffn_w1, ffn_w3, li)
                y = _residual_matmul(hmid, ffn_w2, li, xf, alpha, name="ffn_down")
                xf, xb = _layer_norm(y, ln2_g[i].astype(F32), ln2_b[i].astype(F32))
            else:
                xf, xb, gates, idx = _layer_norm(y, ln1_g[i].astype(F32), ln1_b[i].astype(F32),
                                                 moe_router[li].astype(F32))
                row_token, row_gate, pos, tile_expert, tile_valid, ntiles = _routing_tables(
                    gates, idx, moe_tm)
                xg = _moe_gather(xf, row_token, tile_valid, ntiles, moe_tm)
                hmid = _moe_up(xg, moe_w1, moe_w3, li, row_gate, tile_expert, tile_valid, moe_tm)
                yg = _moe_down(hmid, moe_w2, li, tile_expert, tile_valid, moe_tm)
                xf, xb = _moe_combine_ln(yg, pos, xf, ln2_g[i].astype(F32), ln2_b[i].astype(F32),
                                         alpha)
            xf, xb = _ple(xb, xf, p, ple_w_gate, ple_w_proj, i, bi)
            if i == n_a - 1:
                kvh = _head_split_matmul(xb, kv_w, None, 2 * kv_dim, name="kv_proj")
        outs.append(xf)
    return jnp.stack(outs, axis=0).astype(x.dtype)
```

```python
import functools

import numpy as np
import jax
import jax.numpy as jnp
from jax import lax
from jax.experimental import pallas as pl
from jax.experimental.pallas import tpu as pltpu

F32 = jnp.float32
BF16 = jnp.bfloat16

V7X_LANES = 128
V7X_SCOPED_VMEM_BYTES = 60000 * 1024
COMPILER_SCRATCH_BYTES = 6 * 1024 * 1024

MM_ROWS = 2048
MM_COLS = 256
MM_SUB_ROWS = 512

A_HEAD_DIM = 128
A_CHUNK = 64
A_SUB = 16
A_CUMSUM_ROWS = 256
A_SAFE_LOG2_DECAY = 100.0
B_HEAD_DIM = 64
B_LOGIT_SCALE = B_HEAD_DIM ** -0.5
assert B_LOGIT_SCALE == 2.0 ** -3
B_BLOCK = 128
WINDOW = 128
REL_BUCKETS = 32
REL_MAX_DISTANCE = 128
TOP_K = 2
LN_EPS = 1e-5
RMS_EPS = 1e-6
LOG2E = 1.4426950408889634


def _vmem_limit(block_bytes):
    return int(min(block_bytes + COMPILER_SCRATCH_BYTES, V7X_SCOPED_VMEM_BYTES))


def _nbytes(shape, dtype):
    return int(np.prod([n for n in shape if n is not None])) * jnp.dtype(dtype).itemsize


def _silu(x):
    return x * jax.nn.sigmoid(x)


def _mm_body(*refs, nw, ne, no, epilogue):
    x_ref = refs[0]
    w_refs = refs[1:1 + nw]
    extra_refs = refs[1 + nw:1 + nw + ne]
    out_refs = refs[1 + nw + ne:1 + nw + ne + no]
    wbs = [w_ref[...].astype(BF16) for w_ref in w_refs]
    tm = x_ref.shape[0]
    sub = min(tm, MM_SUB_ROWS)
    for r0 in range(0, tm, sub):
        rows = slice(r0, r0 + sub)
        xv = x_ref[rows, :]
        prods = [jnp.dot(xv, wb, preferred_element_type=F32) for wb in wbs]
        epilogue(prods, extra_refs, out_refs, rows)


def _fused_matmul(x, ws, n_cols, extras, outs, epilogue, *, tm, tn, name, tk=None, k_block=0,
                  x_buffers=2):
    m, kdim = x.shape
    tk = tk or kdim
    tm, tn = min(tm, m), min(tn, n_cols)
    assert m % tm == 0 and n_cols % tn == 0 and kdim % tk == 0
    x_mode = {} if x_buffers == 2 else {"pipeline_mode": pl.Buffered(x_buffers)}
    in_specs = [pl.BlockSpec((tm, tk), lambda i, j: (i, k_block), **x_mode)]
    operands = [x]
    vmem = x_buffers * _nbytes((tm, tk), x.dtype)
    for w, layer, off in ws:
        if layer is None:
            spec = pl.BlockSpec((tk, tn), lambda i, j, off=off: (k_block, j + off))
        else:
            spec = pl.BlockSpec((None, tk, tn), lambda i, j, off=off, layer=layer:
                                (layer, k_block, j + off))
        in_specs.append(spec)
        operands.append(w)
        vmem += 2 * _nbytes((tk, tn), w.dtype) + _nbytes((tk, tn), BF16)
    for arr, bshape, imap in extras:
        in_specs.append(pl.BlockSpec(bshape, imap))
        operands.append(arr)
        vmem += 2 * _nbytes(bshape, arr.dtype)
    out_shapes, out_specs = [], []
    for shape, dtype, bshape, imap in outs:
        out_shapes.append(jax.ShapeDtypeStruct(shape, dtype))
        out_specs.append(pl.BlockSpec(bshape, imap))
        vmem += 2 * _nbytes(bshape, dtype)
    vmem += (2 + len(ws)) * _nbytes((tm, tn), F32)
    body = functools.partial(_mm_body, nw=len(ws), ne=len(extras), no=len(outs), epilogue=epilogue)
    return pl.pallas_call(
        body,
        grid=(m // tm, n_cols // tn),
        in_specs=in_specs,
        out_specs=out_specs,
        out_shape=out_shapes,
        compiler_params=pltpu.CompilerParams(dimension_semantics=("parallel", "parallel"),
                                             vmem_limit_bytes=_vmem_limit(vmem)),
        name=name,
    )(*operands)


def _tile(tm, tn):
    return (tm, tn), (lambda i, j: (i, j))


def _ln_rows(y, g, b):
    mu = jnp.mean(y, axis=-1, keepdims=True)
    yc = y - mu
    var = jnp.mean(yc * yc, axis=-1, keepdims=True)
    return yc * lax.rsqrt(var + LN_EPS) * g + b


def _route_top2(x, w_router):
    logits = jnp.dot(x, w_router, preferred_element_type=F32, precision=lax.Precision.HIGHEST)
    ne = logits.shape[1]
    eid = lax.broadcasted_iota(jnp.int32, logits.shape, 1)
    m1 = jnp.max(logits, axis=-1, keepdims=True)
    i1 = jnp.min(jnp.where(logits == m1, eid, ne), axis=-1, keepdims=True)
    rest = jnp.where(eid == i1, -jnp.inf, logits)
    m2 = jnp.max(rest, axis=-1, keepdims=True)
    i2 = jnp.min(jnp.where(rest == m2, eid, ne), axis=-1, keepdims=True)
    e2 = jnp.exp(m2 - m1)
    w_top1 = 1.0 / (1.0 + e2)
    w_top2 = e2 / (1.0 + e2)
    gates = jnp.where(eid == i1, w_top1, 0.0) + jnp.where(eid == i2, w_top2, 0.0)
    return gates, jnp.where(eid == 0, i1, i2)[:, :TOP_K]


def _ln_body(*refs, route):
    if route:
        y_ref, g_ref, b_ref, wr_ref, o32_ref, o16_ref, gates_ref, idx_ref = refs
    else:
        y_ref, g_ref, b_ref, o32_ref, o16_ref = refs
    out = _ln_rows(y_ref[...], g_ref[...], b_ref[...])
    o32_ref[...] = out
    o16_ref[...] = out.astype(BF16)
    if route:
        gates_ref[...], idx_ref[...] = _route_top2(out, wr_ref[...])


def _layer_norm(y, g, b, w_router=None, *, tr=256):
    s, d = y.shape
    tr = min(tr, s)
    assert s % tr == 0
    row = pl.BlockSpec((tr, d), lambda i: (i, 0))
    vec = pl.BlockSpec((1, d), lambda i: (0, 0))
    in_specs, operands = [row, vec, vec], [y, g.reshape(1, d), b.reshape(1, d)]
    out_specs = [row, row]
    out_shape = [jax.ShapeDtypeStruct((s, d), F32), jax.ShapeDtypeStruct((s, d), BF16)]
    vmem = 2 * (2 * _nbytes((tr, d), F32) + _nbytes((tr, d), BF16)) + 2 * _nbytes((tr, d), F32)
    if w_router is not None:
        ne = w_router.shape[1]
        in_specs.append(pl.BlockSpec((d, ne), lambda i: (0, 0)))
        operands.append(w_router)
        out_specs += [pl.BlockSpec((tr, ne), lambda i: (i, 0)),
                      pl.BlockSpec((tr, TOP_K), lambda i: (i, 0))]
        out_shape += [jax.ShapeDtypeStruct((s, ne), F32), jax.ShapeDtypeStruct((s, TOP_K), jnp.int32)]
        vmem += 2 * _nbytes((d, V7X_LANES), F32) + 4 * _nbytes((tr, d), F32)
    return pl.pallas_call(
        functools.partial(_ln_body, route=w_router is not None),
        grid=(s // tr,),
        in_specs=in_specs,
        out_specs=out_specs,
        out_shape=out_shape,
        compiler_params=pltpu.CompilerParams(dimension_semantics=("parallel",),
                                             vmem_limit_bytes=_vmem_limit(vmem)),
        name="layer_norm_route" if w_router is not None else "layer_norm",
    )(*operands)


def _lower_bound(lbl, layer):
    mx = jnp.max(lbl, axis=0, keepdims=True)
    e = jnp.exp(lbl - mx)
    return jnp.sum(e[:layer + 1], axis=0, keepdims=True) / jnp.sum(e, axis=0, keepdims=True)


def _hgrn_in_proj(xb, w_in, lb_logits, layer, *, tm=MM_ROWS, tn=MM_COLS):
    s, d = xb.shape
    nseg = d // min(tn, d)
    blk, imap = _tile(min(tm, s), min(tn, d))

    def ep_silu(accs, extras, outs, rows):
        outs[0][rows, :] = _silu(accs[0]).astype(BF16)

    def ep_id(accs, extras, outs, rows):
        outs[0][rows, :] = accs[0].astype(BF16)

    def ep_forget(accs, extras, outs, rows):
        lb = _lower_bound(extras[0][...], layer)
        f_raw = accs[0]
        e = jnp.exp(-jnp.abs(f_raw))
        log_sig = jnp.minimum(f_raw, 0.0) - jnp.log(1.0 + e)
        sig_neg = jnp.where(f_raw >= 0.0, e, 1.0) / (1.0 + e)
        a = jnp.log(lb)
        b = jnp.log1p(-lb) + log_sig
        log_f = jnp.maximum(a, b) + jnp.log(1.0 + jnp.exp(-jnp.abs(a - b)))
        outs[0][rows, :] = log_f
        outs[1][rows, :] = ((1.0 - lb) * sig_neg).astype(BF16)
        nrow, ncol = log_f.shape
        outs[2][rows.start // A_CHUNK:rows.stop // A_CHUNK, :] = jnp.sum(
            log_f.reshape(nrow // A_CHUNK, A_CHUNK, ncol), axis=1)

    def call(seg, epilogue, extras, out_dtypes, name, more_outs=(), x_buffers=2):
        outs = [((s, d), dt, blk, imap) for dt in out_dtypes] + list(more_outs)
        return _fused_matmul(xb, [(w_in, layer, seg * nseg)], d, extras, outs, epilogue,
                             tm=tm, tn=tn, name=name, x_buffers=x_buffers)

    nslot = lb_logits.shape[0]
    lb_extra = [(lb_logits, (nslot, blk[1]), lambda i, j: (0, j))]
    decay_out = ((s // A_CHUNK, d), F32, (blk[0] // A_CHUNK, blk[1]), imap)
    (q,) = call(0, ep_silu, [], [BF16], "hgrn_q")
    log_f, k, chunk_decay = call(1, ep_forget, lb_extra, [F32, BF16], "hgrn_f",
                                 more_outs=[decay_out], x_buffers=1)
    (v,) = call(2, ep_id, [], [BF16], "hgrn_v")
    (g,) = call(3, ep_silu, [], [BF16], "hgrn_g")
    return q, k, v, log_f, g, chunk_decay


def _chunk_cumsum(lf_ref, tc):
    tg = min(tc, A_CUMSUM_ROWS)
    row = lax.broadcasted_iota(jnp.int32, (tg, tg), 0)
    col = lax.broadcasted_iota(jnp.int32, (tg, tg), 1)
    tri = jnp.where((col <= row) & (col // A_CHUNK == row // A_CHUNK), 1.0, 0.0).astype(BF16)
    cums = []
    for gi in range(tc // tg):
        lf = lf_ref[gi * tg:(gi + 1) * tg, :]
        hi = lf.astype(BF16)
        r1 = lf - hi.astype(F32)
        mid = r1.astype(BF16)
        lo = (r1 - mid.astype(F32)).astype(BF16)
        cums.append(jnp.dot(tri, hi, preferred_element_type=F32)
                    + jnp.dot(tri, mid, preferred_element_type=F32)
                    + jnp.dot(tri, lo, preferred_element_type=F32))
    return jnp.concatenate(cums, axis=0)


def _hgrn_body(bounded_ref, q_ref, k_ref, v_ref, lf_ref, g_ref, og_ref, o_ref, st_ref, *, tc):
    @pl.when(pl.program_id(1) == 0)
    def _():
        st_ref[...] = jnp.zeros_like(st_ref)

    out_g = og_ref[...]
    half = A_SUB // 2
    lane = lax.broadcasted_iota(jnp.int32, (half, A_HEAD_DIM), 1)
    hrow = lax.broadcasted_iota(jnp.int32, (half, A_HEAD_DIM), 0)
    nsub = A_CHUNK // A_SUB
    nchunk = tc // A_CHUNK

    def finish(rows, o):
        o = o * lax.rsqrt(jnp.mean(o * o, axis=-1, keepdims=True) + RMS_EPS)
        o_ref[rows, :] = (o * out_g * g_ref[rows, :].astype(F32)).astype(BF16)

    def inter_chunk(ci, cum, st):
        rows = slice(ci * A_CHUNK, (ci + 1) * A_CHUNK)
        c2 = cum[rows] * LOG2E
        qc = q_ref[rows, :].astype(F32)
        kc = k_ref[rows, :].astype(F32)
        vc = v_ref[rows, :]
        last = c2[A_CHUNK - 1:A_CHUNK]
        qe = (qc * jnp.exp2(c2)).astype(BF16)
        o_inter = lax.dot_general(qe, st.astype(BF16), (((1,), (1,)), ((), ())),
                                  preferred_element_type=F32)
        kd = (kc * jnp.exp2(last - c2)).astype(BF16)
        upd = lax.dot_general(vc, kd, (((0,), (0,)), ((), ())), preferred_element_type=F32)
        return rows, c2, qc, kc, vc, qe, o_inter, st * jnp.exp2(last) + upd

    def run_factorised():
        c2 = _chunk_cumsum(lf_ref, tc) * LOG2E
        qe = (q_ref[...].astype(F32) * jnp.exp2(c2)).astype(BF16)
        kb32 = k_ref[...].astype(F32) * jnp.exp2(-c2)
        kb = kb32.astype(BF16)
        t_i = lax.broadcasted_iota(jnp.int32, (A_CHUNK, A_CHUNK), 0)
        s_i = lax.broadcasted_iota(jnp.int32, (A_CHUNK, A_CHUNK), 1)
        causal = t_i >= s_i
        st = st_ref[...]
        outs = []
        for ci in range(nchunk):
            rows = slice(ci * A_CHUNK, (ci + 1) * A_CHUNK)
            vc = v_ref[rows, :]
            grow = jnp.exp2(c2[(ci + 1) * A_CHUNK - 1:(ci + 1) * A_CHUNK])
            o_inter = lax.dot_general(qe[rows], st.astype(BF16), (((1,), (1,)), ((), ())),
                                      preferred_element_type=F32)
            kd = (kb32[rows] * grow).astype(BF16)
            upd = lax.dot_general(vc, kd, (((0,), (0,)), ((), ())), preferred_element_type=F32)
            st = st * grow + upd
            sc = lax.dot_general(qe[rows], kb[rows], (((1,), (1,)), ((), ())),
                                 preferred_element_type=F32)
            scores = jnp.where(causal, sc, 0.0).astype(BF16)
            outs.append(o_inter + jnp.dot(scores, vc, preferred_element_type=F32))
        st_ref[...] = st
        finish(slice(0, tc), jnp.concatenate(outs, axis=0))

    def direct_chunk(ci, cum, st):
        rows, c2, qc, kc, vc, qe, o_inter, st = inter_chunk(ci, cum, st)

        p_rows = []
        k_blocks, prev_ref = [], None
        for si in range(nsub):
            lo_r = si * A_SUB
            cs, qs, ks = c2[lo_r:lo_r + A_SUB], qc[lo_r:lo_r + A_SUB], kc[lo_r:lo_r + A_SUB]
            off = None
            if si > 0:
                ref = c2[lo_r - 1:lo_r]
                if prev_ref is not None:
                    rebase = jnp.exp2(ref - prev_ref)
                    k_blocks = [kb * rebase for kb in k_blocks]
                k_blocks.append(kc[lo_r - A_SUB:lo_r] * jnp.exp2(ref - c2[lo_r - A_SUB:lo_r]))
                prev_ref = ref
                k_t = jnp.concatenate(
                    k_blocks + [jnp.zeros((A_CHUNK - lo_r, A_HEAD_DIM), F32)], axis=0)
                q_t = (qs * jnp.exp2(cs - ref)).astype(BF16)
                off = lax.dot_general(q_t, k_t.astype(BF16), (((1,), (1,)), ((), ())),
                                      preferred_element_type=F32)
            c_h = (cs[:half], cs[half:])
            q_h = (qs[:half], qs[half:])
            p_h = [jnp.zeros((half, A_HEAD_DIM), F32), jnp.zeros((half, A_HEAD_DIM), F32)]
            for sj in range(A_SUB):
                for hh in range(sj // half, 2):
                    decay = jnp.exp2(c_h[hh] - cs[sj:sj + 1])
                    colv = jnp.sum(q_h[hh] * (ks[sj:sj + 1] * decay), axis=1, keepdims=True)
                    sel = lane == lo_r + sj
                    if sj // half == hh:
                        sel = sel & (hrow >= sj - hh * half)
                    p_h[hh] = jnp.where(sel, colv, p_h[hh])
            for hh in range(2):
                ph = p_h[hh][:, :A_CHUNK]
                p_rows.append(ph if off is None else ph + off[hh * half:(hh + 1) * half])
        scores = jnp.concatenate(p_rows, axis=0).astype(BF16)
        finish(rows, o_inter + jnp.dot(scores, vc, preferred_element_type=F32))
        return st

    def run_direct():
        cum = _chunk_cumsum(lf_ref, tc)
        st = st_ref[...]
        for ci in range(nchunk):
            st = direct_chunk(ci, cum, st)
        st_ref[...] = st

    bounded = bounded_ref[pl.program_id(0), pl.program_id(1)] > 0
    pl.when(bounded)(run_factorised)
    pl.when(jnp.logical_not(bounded))(run_direct)


def _hgrn_core(q, k, v, log_f, g, out_g, chunk_decay, *, tc=512):
    s, d = q.shape
    tc = min(tc, s)
    assert s % tc == 0 and tc % A_CHUNK == 0 and d % A_HEAD_DIM == 0
    nhead, nblk = d // A_HEAD_DIM, s // tc
    worst = jnp.min(chunk_decay.reshape(nblk, tc // A_CHUNK, nhead, A_HEAD_DIM), axis=(1, 3))
    bounded = (worst.T * LOG2E >= -A_SAFE_LOG2_DECAY).astype(jnp.int32)
    blk = pl.BlockSpec((tc, A_HEAD_DIM), lambda h, c, flags: (c, h))
    vmem = 2 * (4 * _nbytes((tc, A_HEAD_DIM), BF16) + 2 * _nbytes((tc, A_HEAD_DIM), F32)) \
        + 8 * _nbytes((A_CUMSUM_ROWS, A_CUMSUM_ROWS), F32) + 8 * _nbytes((tc, A_HEAD_DIM), F32)
    grid_spec = pltpu.PrefetchScalarGridSpec(
        num_scalar_prefetch=1,
        grid=(nhead, nblk),
        in_specs=[blk, blk, blk, blk, blk,
                  pl.BlockSpec((1, A_HEAD_DIM), lambda h, c, flags: (0, h))],
        out_specs=blk,
        scratch_shapes=[pltpu.VMEM((A_HEAD_DIM, A_HEAD_DIM), F32)],
    )
    return pl.pallas_call(
        functools.partial(_hgrn_body, tc=tc),
        grid_spec=grid_spec,
        out_shape=jax.ShapeDtypeStruct((s, d), BF16),
        compiler_params=pltpu.CompilerParams(dimension_semantics=("parallel", "arbitrary"),
                                             vmem_limit_bytes=_vmem_limit(vmem)),
        name="hgrn_core",
    )(bounded, q, k, v, log_f, g, out_g.reshape(1, d))


def _residual_matmul(hb, w, layer, resid, alpha, *, name, tm=MM_ROWS, tn=MM_COLS):
    s, d = resid.shape
    kdim = hb.shape[1]
    tk = min(kdim, d)
    blk, imap = _tile(min(tm, s), min(tn, d))
    y = resid
    for kb in range(kdim // tk):
        scale = alpha if kb == 0 else 1.0

        def ep(accs, extras, outs, rows, scale=scale):
            outs[0][rows, :] = scale * extras[0][rows, :] + accs[0]

        (y,) = _fused_matmul(hb, [(w, layer, 0)], d, [(y, blk, imap)], [((s, d), F32, blk, imap)],
                             ep, tm=tm, tn=tn, tk=tk, k_block=kb, name=name, x_buffers=1)
    return y


def _rel_bucket_band():
    i = np.arange(B_BLOCK)[:, None]
    j = np.arange(2 * B_BLOCK)[None, :]
    d = np.clip(B_BLOCK + i - j, 0, None)
    max_exact = REL_BUCKETS // 2
    large = max_exact + (np.log(np.maximum(d, 1) / max_exact)
                         / np.log(REL_MAX_DISTANCE / max_exact)
                         * (REL_BUCKETS - max_exact)).astype(np.int32)
    large = np.minimum(large, REL_BUCKETS - 1)
    return np.where(d < max_exact, d, large).astype(np.int32)


def _bias_body(rb_ref, bucket_ref, o_ref, *, heads_per_step):
    h0 = pl.program_id(0) * heads_per_step
    bucket = bucket_ref[...]
    qi = lax.broadcasted_iota(jnp.int32, bucket.shape, 0)
    kj = lax.broadcasted_iota(jnp.int32, bucket.shape, 1)
    dist = B_BLOCK + qi - kj
    in_window = (dist >= 0) & (dist < WINDOW)
    for hh in range(heads_per_step):
        acc = jnp.zeros(bucket.shape, F32)
        for b in range(REL_BUCKETS):
            acc = jnp.where(bucket == b, rb_ref[b, h0 + hh], acc)
        o_ref[hh] = jnp.where(in_window, acc, -jnp.inf)


def _rel_bias_band(rel_bias, *, heads_per_step=8):
    nh = rel_bias.shape[1]
    bucket = jnp.asarray(_rel_bucket_band())
    return pl.pallas_call(
        functools.partial(_bias_body, heads_per_step=heads_per_step),
        grid=(nh // heads_per_step,),
        in_specs=[pl.BlockSpec(memory_space=pltpu.SMEM),
                  pl.BlockSpec((B_BLOCK, 2 * B_BLOCK), lambda i: (0, 0))],
        out_specs=pl.BlockSpec((heads_per_step, B_BLOCK, 2 * B_BLOCK), lambda i: (i, 0, 0)),
        out_shape=jax.ShapeDtypeStruct((nh, B_BLOCK, 2 * B_BLOCK), F32),
        compiler_params=pltpu.CompilerParams(dimension_semantics=("parallel",)),
        name="rel_bias_band",
    )(rel_bias, bucket)


def _head_split_matmul(xb, w, layer, n_cols, *, name, tm=MM_ROWS, tn=MM_COLS):
    s, kdim = xb.shape
    tm, tn = min(tm, s), min(tn, n_cols)
    hpt = tn // B_HEAD_DIM

    def ep(accs, extras, outs, rows):
        for c in range(hpt):
            outs[0][c, rows, :] = accs[0][:, c * B_HEAD_DIM:(c + 1) * B_HEAD_DIM].astype(BF16)

    outs = [((n_cols // B_HEAD_DIM, s, B_HEAD_DIM), BF16, (hpt, tm, B_HEAD_DIM),
             lambda i, j: (j, i, 0))]
    (out,) = _fused_matmul(xb, [(w, layer, 0)], n_cols, [], outs, ep, tm=tm, tn=tn, name=name)
    return out


def _attn_body(sink_ref, q_ref, kp_ref, kc_ref, vp_ref, vc_ref, bias_ref, o_ref, *, group, qb):
    h = pl.program_id(0)
    m_blk = pl.program_id(1)
    c = B_BLOCK
    k_all = jnp.concatenate([kp_ref[0], kc_ref[0]], axis=0)
    v_all = jnp.concatenate([vp_ref[0], vc_ref[0]], axis=0)
    kj = lax.broadcasted_iota(jnp.int32, (c, 2 * c), 1)
    for b in range(qb):
        q = q_ref[:, b * c:(b + 1) * c, :].reshape(group * c, B_HEAD_DIM) * B_LOGIT_SCALE
        kb = k_all[b * c:(b + 2) * c]
        vb = v_all[b * c:(b + 2) * c]
        logits = lax.dot_general(q, kb, (((1,), (1,)), ((), ())), preferred_element_type=F32)
        probs = []
        for gi in range(group):
            lg = logits[gi * c:(gi + 1) * c] + bias_ref[gi]
            if b == 0:
                lg = jnp.where((m_blk > 0) | (kj >= c), lg, -jnp.inf)
            sink = sink_ref[h * group + gi]
            m = jnp.maximum(jnp.max(lg, axis=-1, keepdims=True), sink)
            e = jnp.exp(lg - m)
            denom = jnp.sum(e, axis=-1, keepdims=True) + jnp.exp(sink - m)
            probs.append((e / denom).astype(BF16))
        pv = jnp.dot(jnp.concatenate(probs, axis=0), vb, preferred_element_type=F32)
        o_ref[b * c:(b + 1) * c, :] = jnp.concatenate(
            [pv[gi * c:(gi + 1) * c] for gi in range(group)], axis=1).astype(BF16)


def _swa_core(qh, kvh, sinks, bias, n_kv, *, qb=4):
    nq, s, hd = qh.shape
    group = nq // n_kv
    nb = s // B_BLOCK
    qb = min(qb, nb)
    assert nb % qb == 0
    prev = lambda m: jnp.maximum(m * qb - 1, 0)
    prev_blk = (1, B_BLOCK, hd)
    cur_blk = (1, qb * B_BLOCK, hd)
    in_specs = [
        pl.BlockSpec(memory_space=pltpu.SMEM),
        pl.BlockSpec((group, qb * B_BLOCK, hd), lambda h, m: (h, m, 0)),
        pl.BlockSpec(prev_blk, lambda h, m: (h, prev(m), 0)),
        pl.BlockSpec(cur_blk, lambda h, m: (h, m, 0)),
        pl.BlockSpec(prev_blk, lambda h, m: (h + n_kv, prev(m), 0)),
        pl.BlockSpec(cur_blk, lambda h, m: (h + n_kv, m, 0)),
        pl.BlockSpec((group, B_BLOCK, 2 * B_BLOCK), lambda h, m: (h, 0, 0)),
    ]
    vmem = 2 * (_nbytes((group, B_BLOCK, 2 * B_BLOCK), F32)
                + (group + 3) * _nbytes((qb * B_BLOCK, V7X_LANES), BF16)
                + _nbytes((qb * B_BLOCK, group * hd), BF16)) \
        + 6 * qb * _nbytes((group * B_BLOCK, 2 * B_BLOCK), F32)
    return pl.pallas_call(
        functools.partial(_attn_body, group=group, qb=qb),
        grid=(n_kv, nb // qb),
        in_specs=in_specs,
        out_specs=pl.BlockSpec((qb * B_BLOCK, group * hd), lambda h, m: (m, h)),
        out_shape=jax.ShapeDtypeStruct((s, nq * hd), BF16),
        compiler_params=pltpu.CompilerParams(dimension_semantics=("parallel", "parallel"),
                                             vmem_limit_bytes=_vmem_limit(vmem)),
        name="swa_core",
    )(sinks, qh, kvh, kvh, kvh, kvh, bias)


def _swiglu_up(xb, w1, w3, layer, *, tm=MM_ROWS, tn=MM_COLS):
    s, d = xb.shape
    f = w1.shape[2]
    blk, imap = _tile(min(tm, s), min(tn, f))

    def ep(accs, extras, outs, rows):
        outs[0][rows, :] = (_silu(accs[0]) * accs[1]).astype(BF16)

    (h,) = _fused_matmul(xb, [(w1, layer, 0), (w3, layer, 0)], f, [],
                         [((s, f), BF16, blk, imap)], ep, tm=tm, tn=tn, name="swiglu_up",
                         x_buffers=1)
    return h


def _routing_tables(gates, idx, tm):
    s, ne = gates.shape
    n_assign = s * TOP_K
    ntiles = n_assign // tm + ne
    e_flat = idx.reshape(n_assign)
    onehot = (e_flat[:, None] == jnp.arange(ne, dtype=jnp.int32)[None, :]).astype(jnp.int32)
    rank = jnp.take_along_axis(jnp.cumsum(onehot, axis=0) - onehot, e_flat[:, None], axis=1)[:, 0]
    counts = jnp.sum(onehot, axis=0)
    padded = ((counts + tm - 1) // tm) * tm
    ends = jnp.cumsum(padded)
    pos = (ends - padded)[e_flat] + rank
    token = jnp.arange(n_assign, dtype=jnp.int32) // TOP_K
    gate = jnp.take_along_axis(gates, idx, axis=1)
    row_token = jnp.zeros((ntiles * tm,), jnp.int32).at[pos].set(token)
    tile_start = jnp.arange(ntiles, dtype=jnp.int32) * tm
    tile_expert = jnp.minimum(jnp.searchsorted(ends, tile_start, side="right"), ne - 1).astype(jnp.int32)
    tile_valid = (tile_start < ends[-1]).astype(jnp.int32)
    return row_token, gate, pos.reshape(s, TOP_K).astype(jnp.int32), tile_expert, tile_valid, ntiles


GATHER_ISSUE_UNROLL = 8


def _gather_body(tv_ref, tok_ref, tok_next_ref, x_hbm, o_ref, buf, sem, *, tm, ntiles):
    i = pl.program_id(0)
    slot = i % 2

    def row_copy(tile_tok_ref, slot_, r):
        return pltpu.make_async_copy(x_hbm.at[pl.ds(tile_tok_ref[0, 0, r], 1)],
                                     buf.at[slot_, pl.ds(r, 1)], sem.at[slot_])

    def start_tile(tile_tok_ref, slot_):
        def body(g, carry):
            for u in range(GATHER_ISSUE_UNROLL):
                row_copy(tile_tok_ref, slot_, g * GATHER_ISSUE_UNROLL + u).start(priority=u % 2)
            return carry
        lax.fori_loop(0, tm // GATHER_ISSUE_UNROLL, body, 0)

    @pl.when((i == 0) & (tv_ref[0] > 0))
    def _():
        start_tile(tok_ref, slot)

    @pl.when((i + 1 < ntiles) & (tv_ref[jnp.minimum(i + 1, ntiles - 1)] > 0))
    def _():
        start_tile(tok_next_ref, 1 - slot)

    @pl.when(tv_ref[i] > 0)
    def _():
        def wait_body(r, carry):
            row_copy(tok_ref, slot, r).wait()
            return carry
        lax.fori_loop(0, tm, wait_body, 0, unroll=GATHER_ISSUE_UNROLL)
        o_ref[...] = buf[slot].astype(BF16)

    @pl.when(tv_ref[i] == 0)
    def _():
        o_ref[...] = jnp.zeros_like(o_ref)


def _moe_gather(x, row_token, tile_valid, ntiles, tm):
    s, d = x.shape
    vmem = 2 * _nbytes((tm, d), F32) + 2 * _nbytes((tm, d), BF16) + _nbytes((tm, d), F32)
    tok = row_token.reshape(ntiles, 1, tm)
    grid_spec = pltpu.PrefetchScalarGridSpec(
        num_scalar_prefetch=1,
        grid=(ntiles,),
        in_specs=[pl.BlockSpec((1, 1, tm), lambda i, tv: (i, 0, 0), memory_space=pltpu.SMEM),
                  pl.BlockSpec((1, 1, tm), lambda i, tv: (jnp.minimum(i + 1, ntiles - 1), 0, 0),
                               memory_space=pltpu.SMEM),
                  pl.BlockSpec(memory_space=pl.ANY)],
        out_specs=pl.BlockSpec((tm, d), lambda i, tv: (i, 0)),
        scratch_shapes=[pltpu.VMEM((2, tm, d), F32), pltpu.SemaphoreType.DMA((2,))],
    )
    return pl.pallas_call(
        functools.partial(_gather_body, tm=tm, ntiles=ntiles),
        grid_spec=grid_spec,
        out_shape=jax.ShapeDtypeStruct((ntiles * tm, d), BF16),
        compiler_params=pltpu.CompilerParams(dimension_semantics=("arbitrary",),
                                             vmem_limit_bytes=_vmem_limit(vmem)),
        name="moe_gather",
    )(tile_valid, tok, tok, x)


def _expert_changed(te_ref, i):
    return (i == 0) | (te_ref[i] != te_ref[jnp.maximum(i - 1, 0)])


def _moe_up_body(te_ref, tv_ref, x_ref, w1_ref, w3_ref, o_ref, wb1, wb3):
    i = pl.program_id(1)

    @pl.when(_expert_changed(te_ref, i))
    def _():
        wb1[...] = w1_ref[...].astype(BF16)
        wb3[...] = w3_ref[...].astype(BF16)

    @pl.when(tv_ref[i] > 0)
    def _():
        xv = x_ref[...]
        a = jnp.dot(xv, wb1[...], preferred_element_type=F32)
        b = jnp.dot(xv, wb3[...], preferred_element_type=F32)
        o_ref[...] = (_silu(a) * b).astype(BF16)

    @pl.when(tv_ref[i] == 0)
    def _():
        o_ref[...] = jnp.zeros_like(o_ref)


def _moe_up(xg, w1, w3, layer, tile_expert, tile_valid, tm, *, tf=512):
    rows, d = xg.shape
    f = w1.shape[3]
    tf = min(tf, f)
    ntiles = rows // tm
    wspec = pl.BlockSpec((None, None, d, tf), lambda j, i, te, tv: (layer, te[i], 0, j))
    vmem = 2 * (_nbytes((tm, d), BF16) + 2 * _nbytes((d, tf), F32) + _nbytes((tm, tf), BF16)) \
        + 2 * _nbytes((d, tf), BF16) + 4 * _nbytes((tm, tf), F32)
    grid_spec = pltpu.PrefetchScalarGridSpec(
        num_scalar_prefetch=2,
        grid=(f // tf, ntiles),
        in_specs=[pl.BlockSpec((tm, d), lambda j, i, te, tv: (i, 0)), wspec, wspec],
        out_specs=pl.BlockSpec((tm, tf), lambda j, i, te, tv: (i, j)),
        scratch_shapes=[pltpu.VMEM((d, tf), BF16), pltpu.VMEM((d, tf), BF16)],
    )
    return pl.pallas_call(
        _moe_up_body,
        grid_spec=grid_spec,
        out_shape=jax.ShapeDtypeStruct((rows, f), BF16),
        compiler_params=pltpu.CompilerParams(dimension_semantics=("parallel", "arbitrary"),
                                             vmem_limit_bytes=_vmem_limit(vmem)),
        name="moe_up",
    )(tile_expert, tile_valid, xg, w1, w3)


def _moe_down_body(te_ref, tv_ref, h_ref, w_ref, o_ref, wb):
    i = pl.program_id(1)

    @pl.when(_expert_changed(te_ref, i))
    def _():
        wb[...] = w_ref[...].astype(BF16)

    @pl.when(tv_ref[i] > 0)
    def _():
        o_ref[...] = jnp.dot(h_ref[...], wb[...], preferred_element_type=F32)

    @pl.when(tv_ref[i] == 0)
    def _():
        o_ref[...] = jnp.zeros_like(o_ref)


def _moe_down(h, w2, layer, tile_expert, tile_valid, tm, *, tn=2048):
    rows, f = h.shape
    d = w2.shape[3]
    tn = min(tn, d)
    ntiles = rows // tm
    vmem = 2 * (_nbytes((tm, f), BF16) + _nbytes((f, tn), F32) + _nbytes((tm, tn), F32)) \
        + _nbytes((f, tn), BF16) + 2 * _nbytes((tm, tn), F32)
    grid_spec = pltpu.PrefetchScalarGridSpec(
        num_scalar_prefetch=2,
        grid=(d // tn, ntiles),
        in_specs=[pl.BlockSpec((tm, f), lambda j, i, te, tv: (i, 0)),
                  pl.BlockSpec((None, None, f, tn), lambda j, i, te, tv: (layer, te[i], 0, j))],
        out_specs=pl.BlockSpec((tm, tn), lambda j, i, te, tv: (i, j)),
        scratch_shapes=[pltpu.VMEM((f, tn), BF16)],
    )
    return pl.pallas_call(
        _moe_down_body,
        grid_spec=grid_spec,
        out_shape=jax.ShapeDtypeStruct((rows, d), F32),
        compiler_params=pltpu.CompilerParams(dimension_semantics=("parallel", "arbitrary"),
                                             vmem_limit_bytes=_vmem_limit(vmem)),
        name="moe_down",
    )(tile_expert, tile_valid, h, w2)


def _combine_body(pos_ref, pos_next_ref, y_hbm, x_ref, gate_ref, g_ref, b_ref, o32_ref, o16_ref,
                  buf, sem, *, tr, nt, alpha):
    i = pl.program_id(0)
    slot = i % 2

    def row_copy(tile_pos_ref, slot_, r, kk):
        return pltpu.make_async_copy(y_hbm.at[pl.ds(tile_pos_ref[0, kk, r], 1)],
                                     buf.at[slot_, kk, pl.ds(r, 1)], sem.at[slot_])

    def start_tile(tile_pos_ref, slot_):
        def body(r, carry):
            for kk in range(TOP_K):
                row_copy(tile_pos_ref, slot_, r, kk).start(priority=kk % 2)
            return carry
        lax.fori_loop(0, tr, body, 0, unroll=GATHER_ISSUE_UNROLL)

    @pl.when(i == 0)
    def _():
        start_tile(pos_ref, slot)

    @pl.when(i + 1 < nt)
    def _():
        start_tile(pos_next_ref, 1 - slot)

    def wait_body(r, carry):
        for kk in range(TOP_K):
            row_copy(pos_ref, slot, r, kk).wait()
        return carry
    lax.fori_loop(0, tr, wait_body, 0, unroll=GATHER_ISSUE_UNROLL)

    gate = gate_ref[...]
    f = buf[slot, 0] * gate[:, 0:1]
    for kk in range(1, TOP_K):
        f = f + buf[slot, kk] * gate[:, kk:kk + 1]
    out = _ln_rows(alpha * x_ref[...] + f, g_ref[...], b_ref[...])
    o32_ref[...] = out
    o16_ref[...] = out.astype(BF16)


def _moe_combine_ln(y, pos, gate, x, g, b, alpha, *, tr=256):
    s, d = x.shape
    tr = min(tr, s)
    nt = s // tr
    pos_t = pos.reshape(nt, tr, TOP_K).transpose(0, 2, 1)
    row = pl.BlockSpec((tr, d), lambda i: (i, 0))
    vec = pl.BlockSpec((1, d), lambda i: (0, 0))
    vmem = 2 * TOP_K * _nbytes((tr, d), F32) \
        + 2 * (2 * _nbytes((tr, d), F32) + _nbytes((tr, d), BF16)) + 3 * _nbytes((tr, d), F32)
    pos_blk = (1, TOP_K, tr)
    return pl.pallas_call(
        functools.partial(_combine_body, tr=tr, nt=nt, alpha=alpha),
        grid=(nt,),
        in_specs=[pl.BlockSpec(pos_blk, lambda i: (i, 0, 0), memory_space=pltpu.SMEM),
                  pl.BlockSpec(pos_blk, lambda i: (jnp.minimum(i + 1, nt - 1), 0, 0),
                               memory_space=pltpu.SMEM),
                  pl.BlockSpec(memory_space=pl.ANY), row,
                  pl.BlockSpec((tr, TOP_K), lambda i: (i, 0)), vec, vec],
        out_specs=[row, row],
        out_shape=[jax.ShapeDtypeStruct((s, d), F32), jax.ShapeDtypeStruct((s, d), BF16)],
        scratch_shapes=[pltpu.VMEM((2, TOP_K, tr, d), F32), pltpu.SemaphoreType.DMA((2,))],
        compiler_params=pltpu.CompilerParams(dimension_semantics=("arbitrary",),
                                             vmem_limit_bytes=_vmem_limit(vmem)),
        name="moe_combine_ln",
    )(pos_t, pos_t, y, x, gate, g.reshape(1, d), b.reshape(1, d))


def _ple(xb, x, p, w_gate, w_proj, layer, bi, *, tm=MM_ROWS, tn=MM_COLS):
    s, d = x.shape
    pdim = p.shape[-1]
    blk, imap = _tile(min(tm, s), min(tn, d))

    def ep(accs, extras, outs, rows):
        proj = jnp.dot(extras[1][rows, :].astype(BF16), extras[2][...].astype(BF16),
                       preferred_element_type=F32)
        out = extras[0][rows, :] + jax.nn.sigmoid(accs[0]) * proj
        outs[0][rows, :] = out
        outs[1][rows, :] = out.astype(BF16)

    extras = [(x, blk, imap),
              (p, (None, None, blk[0], pdim), lambda i, j: (layer, bi, i, 0)),
              (w_proj, (None, pdim, blk[1]), lambda i, j: (layer, 0, j))]
    outs = [((s, d), F32, blk, imap), ((s, d), BF16, blk, imap)]
    return _fused_matmul(xb, [(w_gate, layer, 0)], d, extras, outs, ep, tm=tm, tn=tn, name="ple",
                         x_buffers=1)


MOE_TILE_ROWS = 512


def kernel(x, p, ln1_g, ln1_b, ln2_g, ln2_b, a_w_in, a_lb_logits, a_out_g, a_w_o, kv_w, b_w_q, b_sinks, b_w_o, rel_bias, ffn_w1, ffn_w3, ffn_w2, moe_router, moe_w1, moe_w3, moe_w2, ple_w_proj, ple_w_gate):
    batch, s, d = x.shape
    depth = p.shape[0]
    n_a = a_w_in.shape[0]
    kv_dim = kv_w.shape[1] // 2
    n_kv = kv_dim // B_HEAD_DIM
    alpha = (2.0 * depth) ** 0.25
    moe_tm = min(MOE_TILE_ROWS, s)
    bias_band = _rel_bias_band(rel_bias.astype(F32))
    lb_logits = a_lb_logits.astype(F32)

    outs = []
    for bi in range(batch):
        xf = x[bi].astype(F32)
        xb = xf.astype(BF16)
        kvh = None
        for i in range(depth):
            if i < n_a:
                q, k, v, log_f, g, chunk_decay = _hgrn_in_proj(xb, a_w_in, lb_logits, i)
                hb = _hgrn_core(q, k, v, log_f, g, a_out_g[i].astype(F32), chunk_decay)
                y = _residual_matmul(hb, a_w_o, i, xf, alpha, name="mixer_out")
            else:
                j = i - n_a
                qh = _head_split_matmul(xb, b_w_q, j, d, name="swa_q")
                hb = _swa_core(qh, kvh, b_sinks[j].astype(F32), bias_band, n_kv)
                y = _residual_matmul(hb, b_w_o, j, xf, alpha, name="mixer_out")
            li = i // 2
            if i % 2 == 0:
                xf, xb = _layer_norm(y, ln1_g[i].astype(F32), ln1_b[i].astype(F32))
                hmid = _swiglu_up(xb, ffn_w1, ffn_w3, li)
                y = _residual_matmul(hmid, ffn_w2, li, xf, alpha, name="ffn_down")
                xf, xb = _layer_norm(y, ln2_g[i].astype(F32), ln2_b[i].astype(F32))
            else:
                xf, xb, gates, idx = _layer_norm(y, ln1_g[i].astype(F32), ln1_b[i].astype(F32),
                                                 moe_router[li].astype(F32))
                row_token, gate, pos, tile_expert, tile_valid, ntiles = _routing_tables(
                    gates, idx, moe_tm)
                xg = _moe_gather(xf, row_token, tile_valid, ntiles, moe_tm)
                hmid = _moe_up(xg, moe_w1, moe_w3, li, tile_expert, tile_valid, moe_tm)
                yg = _moe_down(hmid, moe_w2, li, tile_expert, tile_valid, moe_tm)
                xf, xb = _moe_combine_ln(yg, pos, gate, xf, ln2_g[i].astype(F32),
                                         ln2_b[i].astype(F32), alpha)
            xf, xb = _ple(xb, xf, p, ple_w_gate, ple_w_proj, i, bi)
            if i == n_a - 1:
                kvh = _head_split_matmul(xb, kv_w, None, 2 * kv_dim, name="kv_proj")
        outs.append(xf)
    return jnp.stack(outs, axis=0).astype(x.dtype)
```

```python
import functools

import numpy as np
import jax
import jax.numpy as jnp
from jax import lax
from jax.experimental import pallas as pl
from jax.experimental.pallas import tpu as pltpu

F32 = jnp.float32
BF16 = jnp.bfloat16

V7X_LANES = 128
V7X_SCOPED_VMEM_BYTES = 60000 * 1024
COMPILER_SCRATCH_BYTES = 6 * 1024 * 1024

MM_ROWS = 2048
MM_COLS = 256
MM_SUB_ROWS = 128

A_HEAD_DIM = 128
A_CHUNK = 64
A_SUB = 16
A_CUMSUM_ROWS = 256
A_SAFE_LOG2_DECAY = 100.0
B_HEAD_DIM = 64
B_LOGIT_SCALE = B_HEAD_DIM ** -0.5
assert B_LOGIT_SCALE == 2.0 ** -3
B_BLOCK = 128
WINDOW = 128
REL_BUCKETS = 32
REL_MAX_DISTANCE = 128
TOP_K = 2
LN_EPS = 1e-5
RMS_EPS = 1e-6
LOG2E = 1.4426950408889634


def _vmem_limit(block_bytes):
    return int(min(block_bytes + COMPILER_SCRATCH_BYTES, V7X_SCOPED_VMEM_BYTES))


def _nbytes(shape, dtype):
    return int(np.prod([n for n in shape if n is not None])) * jnp.dtype(dtype).itemsize


def _silu(x):
    return x * jax.nn.sigmoid(x)


def _mm_body(*refs, nw, ne, no, epilogue):
    x_ref = refs[0]
    w_refs = refs[1:1 + nw]
    extra_refs = refs[1 + nw:1 + nw + ne]
    out_refs = refs[1 + nw + ne:1 + nw + ne + no]
    wbs = [w_ref[...].astype(BF16) for w_ref in w_refs]
    tm = x_ref.shape[0]
    sub = min(tm, MM_SUB_ROWS)
    for r0 in range(0, tm, sub):
        rows = slice(r0, r0 + sub)
        xv = x_ref[rows, :]
        prods = [jnp.dot(xv, wb, preferred_element_type=F32) for wb in wbs]
        epilogue(prods, extra_refs, out_refs, rows)


def _fused_matmul(x, ws, n_cols, extras, outs, epilogue, *, tm, tn, name, tk=None, k_block=0,
                  x_buffers=2):
    m, kdim = x.shape
    tk = tk or kdim
    tm, tn = min(tm, m), min(tn, n_cols)
    assert m % tm == 0 and n_cols % tn == 0 and kdim % tk == 0
    x_mode = {} if x_buffers == 2 else {"pipeline_mode": pl.Buffered(x_buffers)}
    in_specs = [pl.BlockSpec((tm, tk), lambda i, j: (i, k_block), **x_mode)]
    operands = [x]
    vmem = x_buffers * _nbytes((tm, tk), x.dtype)
    for w, layer, off in ws:
        if layer is None:
            spec = pl.BlockSpec((tk, tn), lambda i, j, off=off: (k_block, j + off))
        else:
            spec = pl.BlockSpec((None, tk, tn), lambda i, j, off=off, layer=layer:
                                (layer, k_block, j + off))
        in_specs.append(spec)
        operands.append(w)
        vmem += 2 * _nbytes((tk, tn), w.dtype) + _nbytes((tk, tn), BF16)
    for arr, bshape, imap in extras:
        in_specs.append(pl.BlockSpec(bshape, imap))
        operands.append(arr)
        vmem += 2 * _nbytes(bshape, arr.dtype)
    out_shapes, out_specs = [], []
    for shape, dtype, bshape, imap in outs:
        out_shapes.append(jax.ShapeDtypeStruct(shape, dtype))
        out_specs.append(pl.BlockSpec(bshape, imap))
        vmem += 2 * _nbytes(bshape, dtype)
    vmem += (2 + len(ws)) * _nbytes((tm, tn), F32)
    body = functools.partial(_mm_body, nw=len(ws), ne=len(extras), no=len(outs), epilogue=epilogue)
    return pl.pallas_call(
        body,
        grid=(m // tm, n_cols // tn),
        in_specs=in_specs,
        out_specs=out_specs,
        out_shape=out_shapes,
        compiler_params=pltpu.CompilerParams(dimension_semantics=("parallel", "parallel"),
                                             vmem_limit_bytes=_vmem_limit(vmem)),
        name=name,
    )(*operands)


def _tile(tm, tn):
    return (tm, tn), (lambda i, j: (i, j))


def _ln_rows(y, g, b):
    mu = jnp.mean(y, axis=-1, keepdims=True)
    yc = y - mu
    var = jnp.mean(yc * yc, axis=-1, keepdims=True)
    return yc * lax.rsqrt(var + LN_EPS) * g + b


def _route_top2(x, w_router):
    logits = jnp.dot(x, w_router, preferred_element_type=F32, precision=lax.Precision.HIGHEST)
    ne = logits.shape[1]
    eid = lax.broadcasted_iota(jnp.int32, logits.shape, 1)
    m1 = jnp.max(logits, axis=-1, keepdims=True)
    i1 = jnp.min(jnp.where(logits == m1, eid, ne), axis=-1, keepdims=True)
    rest = jnp.where(eid == i1, -jnp.inf, logits)
    m2 = jnp.max(rest, axis=-1, keepdims=True)
    i2 = jnp.min(jnp.where(rest == m2, eid, ne), axis=-1, keepdims=True)
    e2 = jnp.exp(m2 - m1)
    w_top1 = 1.0 / (1.0 + e2)
    w_top2 = e2 / (1.0 + e2)
    gates = jnp.where(eid == i1, w_top1, 0.0) + jnp.where(eid == i2, w_top2, 0.0)
    return gates, jnp.where(eid == 0, i1, i2)[:, :TOP_K]


def _ln_body(*refs, route):
    if route:
        y_ref, g_ref, b_ref, wr_ref, o32_ref, o16_ref, gates_ref, idx_ref = refs
    else:
        y_ref, g_ref, b_ref, o32_ref, o16_ref = refs
    out = _ln_rows(y_ref[...], g_ref[...], b_ref[...])
    o32_ref[...] = out
    o16_ref[...] = out.astype(BF16)
    if route:
        gates_ref[...], idx_ref[...] = _route_top2(out, wr_ref[...])


def _layer_norm(y, g, b, w_router=None, *, tr=256):
    s, d = y.shape
    tr = min(tr, s)
    assert s % tr == 0
    row = pl.BlockSpec((tr, d), lambda i: (i, 0))
    vec = pl.BlockSpec((1, d), lambda i: (0, 0))
    in_specs, operands = [row, vec, vec], [y, g.reshape(1, d), b.reshape(1, d)]
    out_specs = [row, row]
    out_shape = [jax.ShapeDtypeStruct((s, d), F32), jax.ShapeDtypeStruct((s, d), BF16)]
    vmem = 2 * (2 * _nbytes((tr, d), F32) + _nbytes((tr, d), BF16)) + 2 * _nbytes((tr, d), F32)
    if w_router is not None:
        ne = w_router.shape[1]
        in_specs.append(pl.BlockSpec((d, ne), lambda i: (0, 0)))
        operands.append(w_router)
        out_specs += [pl.BlockSpec((tr, ne), lambda i: (i, 0)),
                      pl.BlockSpec((tr, TOP_K), lambda i: (i, 0))]
        out_shape += [jax.ShapeDtypeStruct((s, ne), F32), jax.ShapeDtypeStruct((s, TOP_K), jnp.int32)]
        vmem += 2 * _nbytes((d, V7X_LANES), F32) + 4 * _nbytes((tr, d), F32)
    return pl.pallas_call(
        functools.partial(_ln_body, route=w_router is not None),
        grid=(s // tr,),
        in_specs=in_specs,
        out_specs=out_specs,
        out_shape=out_shape,
        compiler_params=pltpu.CompilerParams(dimension_semantics=("parallel",),
                                             vmem_limit_bytes=_vmem_limit(vmem)),
        name="layer_norm_route" if w_router is not None else "layer_norm",
    )(*operands)


def _lower_bound(lbl, layer):
    mx = jnp.max(lbl, axis=0, keepdims=True)
    e = jnp.exp(lbl - mx)
    return jnp.sum(e[:layer + 1], axis=0, keepdims=True) / jnp.sum(e, axis=0, keepdims=True)


def _hgrn_in_proj(xb, w_in, lb_logits, layer, *, tm=MM_ROWS, tn=MM_COLS):
    s, d = xb.shape
    nseg = d // min(tn, d)
    blk, imap = _tile(min(tm, s), min(tn, d))

    def ep_silu(accs, extras, outs, rows):
        outs[0][rows, :] = _silu(accs[0]).astype(BF16)

    def ep_id(accs, extras, outs, rows):
        outs[0][rows, :] = accs[0].astype(BF16)

    def ep_forget(accs, extras, outs, rows):
        lb = _lower_bound(extras[0][...], layer)
        f_raw = accs[0]
        e = jnp.exp(-jnp.abs(f_raw))
        log_sig = jnp.minimum(f_raw, 0.0) - jnp.log(1.0 + e)
        sig_neg = jnp.where(f_raw >= 0.0, e, 1.0) / (1.0 + e)
        a = jnp.log(lb)
        b = jnp.log1p(-lb) + log_sig
        log_f = jnp.maximum(a, b) + jnp.log(1.0 + jnp.exp(-jnp.abs(a - b)))
        outs[0][rows, :] = log_f
        outs[1][rows, :] = ((1.0 - lb) * sig_neg).astype(BF16)
        nrow, ncol = log_f.shape
        outs[2][rows.start // A_CHUNK:rows.stop // A_CHUNK, :] = jnp.sum(
            log_f.reshape(nrow // A_CHUNK, A_CHUNK, ncol), axis=1)

    def call(seg, epilogue, extras, out_dtypes, name, more_outs=(), x_buffers=2):
        outs = [((s, d), dt, blk, imap) for dt in out_dtypes] + list(more_outs)
        return _fused_matmul(xb, [(w_in, layer, seg * nseg)], d, extras, outs, epilogue,
                             tm=tm, tn=tn, name=name, x_buffers=x_buffers)

    nslot = lb_logits.shape[0]
    lb_extra = [(lb_logits, (nslot, blk[1]), lambda i, j: (0, j))]
    decay_out = ((s // A_CHUNK, d), F32, (blk[0] // A_CHUNK, blk[1]), imap)
    (q,) = call(0, ep_silu, [], [BF16], "hgrn_q")
    log_f, k, chunk_decay = call(1, ep_forget, lb_extra, [F32, BF16], "hgrn_f",
                                 more_outs=[decay_out])
    (v,) = call(2, ep_id, [], [BF16], "hgrn_v")
    (g,) = call(3, ep_silu, [], [BF16], "hgrn_g")
    return q, k, v, log_f, g, chunk_decay


def _chunk_cumsum(lf_ref, tc):
    tg = min(tc, A_CUMSUM_ROWS)
    row = lax.broadcasted_iota(jnp.int32, (tg, tg), 0)
    col = lax.broadcasted_iota(jnp.int32, (tg, tg), 1)
    tri = jnp.where((col <= row) & (col // A_CHUNK == row // A_CHUNK), 1.0, 0.0).astype(BF16)
    cums = []
    for gi in range(tc // tg):
        lf = lf_ref[gi * tg:(gi + 1) * tg, :]
        hi = lf.astype(BF16)
        r1 = lf - hi.astype(F32)
        mid = r1.astype(BF16)
        lo = (r1 - mid.astype(F32)).astype(BF16)
        cums.append(jnp.dot(tri, hi, preferred_element_type=F32)
                    + jnp.dot(tri, mid, preferred_element_type=F32)
                    + jnp.dot(tri, lo, preferred_element_type=F32))
    return jnp.concatenate(cums, axis=0)


def _hgrn_body(bounded_ref, q_ref, k_ref, v_ref, lf_ref, g_ref, og_ref, o_ref, st_ref, *, tc):
    @pl.when(pl.program_id(1) == 0)
    def _():
        st_ref[...] = jnp.zeros_like(st_ref)

    out_g = og_ref[...]
    half = A_SUB // 2
    lane = lax.broadcasted_iota(jnp.int32, (half, A_HEAD_DIM), 1)
    hrow = lax.broadcasted_iota(jnp.int32, (half, A_HEAD_DIM), 0)
    nsub = A_CHUNK // A_SUB
    nchunk = tc // A_CHUNK

    def finish(rows, o):
        o = o * lax.rsqrt(jnp.mean(o * o, axis=-1, keepdims=True) + RMS_EPS)
        o_ref[rows, :] = (o * out_g * g_ref[rows, :].astype(F32)).astype(BF16)

    def inter_chunk(ci, cum, st):
        rows = slice(ci * A_CHUNK, (ci + 1) * A_CHUNK)
        c2 = cum[rows] * LOG2E
        qc = q_ref[rows, :].astype(F32)
        kc = k_ref[rows, :].astype(F32)
        vc = v_ref[rows, :]
        last = c2[A_CHUNK - 1:A_CHUNK]
        qe = (qc * jnp.exp2(c2)).astype(BF16)
        o_inter = lax.dot_general(qe, st.astype(BF16), (((1,), (1,)), ((), ())),
                                  preferred_element_type=F32)
        kd = (kc * jnp.exp2(last - c2)).astype(BF16)
        upd = lax.dot_general(vc, kd, (((0,), (0,)), ((), ())), preferred_element_type=F32)
        return rows, c2, qc, kc, vc, qe, o_inter, st * jnp.exp2(last) + upd

    def run_factorised():
        c2 = _chunk_cumsum(lf_ref, tc) * LOG2E
        qe = (q_ref[...].astype(F32) * jnp.exp2(c2)).astype(BF16)
        kb32 = k_ref[...].astype(F32) * jnp.exp2(-c2)
        kb = kb32.astype(BF16)
        t_i = lax.broadcasted_iota(jnp.int32, (A_CHUNK, A_CHUNK), 0)
        s_i = lax.broadcasted_iota(jnp.int32, (A_CHUNK, A_CHUNK), 1)
        causal = t_i >= s_i
        st = st_ref[...]
        outs = []
        for ci in range(nchunk):
            rows = slice(ci * A_CHUNK, (ci + 1) * A_CHUNK)
            vc = v_ref[rows, :]
            grow = jnp.exp2(c2[(ci + 1) * A_CHUNK - 1:(ci + 1) * A_CHUNK])
            o_inter = lax.dot_general(qe[rows], st.astype(BF16), (((1,), (1,)), ((), ())),
                                      preferred_element_type=F32)
            kd = (kb32[rows] * grow).astype(BF16)
            upd = lax.dot_general(vc, kd, (((0,), (0,)), ((), ())), preferred_element_type=F32)
            st = st * grow + upd
            sc = lax.dot_general(qe[rows], kb[rows], (((1,), (1,)), ((), ())),
                                 preferred_element_type=F32)
            scores = jnp.where(causal, sc, 0.0).astype(BF16)
            outs.append(o_inter + jnp.dot(scores, vc, preferred_element_type=F32))
        st_ref[...] = st
        finish(slice(0, tc), jnp.concatenate(outs, axis=0))

    def direct_chunk(ci, cum, st):
        rows, c2, qc, kc, vc, qe, o_inter, st = inter_chunk(ci, cum, st)

        p_rows = []
        k_blocks, prev_ref = [], None
        for si in range(nsub):
            lo_r = si * A_SUB
            cs, qs, ks = c2[lo_r:lo_r + A_SUB], qc[lo_r:lo_r + A_SUB], kc[lo_r:lo_r + A_SUB]
            off = None
            if si > 0:
                ref = c2[lo_r - 1:lo_r]
                if prev_ref is not None:
                    rebase = jnp.exp2(ref - prev_ref)
                    k_blocks = [kb * rebase for kb in k_blocks]
                k_blocks.append(kc[lo_r - A_SUB:lo_r] * jnp.exp2(ref - c2[lo_r - A_SUB:lo_r]))
                prev_ref = ref
                k_t = jnp.concatenate(
                    k_blocks + [jnp.zeros((A_CHUNK - lo_r, A_HEAD_DIM), F32)], axis=0)
                q_t = (qs * jnp.exp2(cs - ref)).astype(BF16)
                off = lax.dot_general(q_t, k_t.astype(BF16), (((1,), (1,)), ((), ())),
                                      preferred_element_type=F32)
            c_h = (cs[:half], cs[half:])
            q_h = (qs[:half], qs[half:])
            p_h = [jnp.zeros((half, A_HEAD_DIM), F32), jnp.zeros((half, A_HEAD_DIM), F32)]
            for sj in range(A_SUB):
                for hh in range(sj // half, 2):
                    decay = jnp.exp2(c_h[hh] - cs[sj:sj + 1])
                    colv = jnp.sum(q_h[hh] * (ks[sj:sj + 1] * decay), axis=1, keepdims=True)
                    sel = lane == lo_r + sj
                    if sj // half == hh:
                        sel = sel & (hrow >= sj - hh * half)
                    p_h[hh] = jnp.where(sel, colv, p_h[hh])
            for hh in range(2):
                ph = p_h[hh][:, :A_CHUNK]
                p_rows.append(ph if off is None else ph + off[hh * half:(hh + 1) * half])
        scores = jnp.concatenate(p_rows, axis=0).astype(BF16)
        finish(rows, o_inter + jnp.dot(scores, vc, preferred_element_type=F32))
        return st

    def run_direct():
        cum = _chunk_cumsum(lf_ref, tc)
        st = st_ref[...]
        for ci in range(nchunk):
            st = direct_chunk(ci, cum, st)
        st_ref[...] = st

    bounded = bounded_ref[pl.program_id(0), pl.program_id(1)] > 0
    pl.when(bounded)(run_factorised)
    pl.when(jnp.logical_not(bounded))(run_direct)


def _hgrn_core(q, k, v, log_f, g, out_g, chunk_decay, *, tc=512):
    s, d = q.shape
    tc = min(tc, s)
    assert s % tc == 0 and tc % A_CHUNK == 0 and d % A_HEAD_DIM == 0
    nhead, nblk = d // A_HEAD_DIM, s // tc
    worst = jnp.min(chunk_decay.reshape(nblk, tc // A_CHUNK, nhead, A_HEAD_DIM), axis=(1, 3))
    bounded = (worst.T * LOG2E >= -A_SAFE_LOG2_DECAY).astype(jnp.int32)
    blk = pl.BlockSpec((tc, A_HEAD_DIM), lambda h, c, flags: (c, h))
    vmem = 2 * (4 * _nbytes((tc, A_HEAD_DIM), BF16) + 2 * _nbytes((tc, A_HEAD_DIM), F32)) \
        + 8 * _nbytes((A_CUMSUM_ROWS, A_CUMSUM_ROWS), F32) + 8 * _nbytes((tc, A_HEAD_DIM), F32)
    grid_spec = pltpu.PrefetchScalarGridSpec(
        num_scalar_prefetch=1,
        grid=(nhead, nblk),
        in_specs=[blk, blk, blk, blk, blk,
                  pl.BlockSpec((1, A_HEAD_DIM), lambda h, c, flags: (0, h))],
        out_specs=blk,
        scratch_shapes=[pltpu.VMEM((A_HEAD_DIM, A_HEAD_DIM), F32)],
    )
    return pl.pallas_call(
        functools.partial(_hgrn_body, tc=tc),
        grid_spec=grid_spec,
        out_shape=jax.ShapeDtypeStruct((s, d), BF16),
        compiler_params=pltpu.CompilerParams(dimension_semantics=("parallel", "arbitrary"),
                                             vmem_limit_bytes=_vmem_limit(vmem)),
        name="hgrn_core",
    )(bounded, q, k, v, log_f, g, out_g.reshape(1, d))


def _residual_matmul(hb, w, layer, resid, alpha, *, name, tm=MM_ROWS, tn=MM_COLS):
    s, d = resid.shape
    kdim = hb.shape[1]
    tk = min(kdim, d)
    blk, imap = _tile(min(tm, s), min(tn, d))
    y = resid
    for kb in range(kdim // tk):
        scale = alpha if kb == 0 else 1.0

        def ep(accs, extras, outs, rows, scale=scale):
            outs[0][rows, :] = scale * extras[0][rows, :] + accs[0]

        (y,) = _fused_matmul(hb, [(w, layer, 0)], d, [(y, blk, imap)], [((s, d), F32, blk, imap)],
                             ep, tm=tm, tn=tn, tk=tk, k_block=kb, name=name)
    return y


def _rel_bucket_band():
    i = np.arange(B_BLOCK)[:, None]
    j = np.arange(2 * B_BLOCK)[None, :]
    d = np.clip(B_BLOCK + i - j, 0, None)
    max_exact = REL_BUCKETS // 2
    large = max_exact + (np.log(np.maximum(d, 1) / max_exact)
                         / np.log(REL_MAX_DISTANCE / max_exact)
                         * (REL_BUCKETS - max_exact)).astype(np.int32)
    large = np.minimum(large, REL_BUCKETS - 1)
    return np.where(d < max_exact, d, large).astype(np.int32)


def _bias_body(rb_ref, bucket_ref, o_ref, *, heads_per_step):
    h0 = pl.program_id(0) * heads_per_step
    bucket = bucket_ref[...]
    qi = lax.broadcasted_iota(jnp.int32, bucket.shape, 0)
    kj = lax.broadcasted_iota(jnp.int32, bucket.shape, 1)
    dist = B_BLOCK + qi - kj
    in_window = (dist >= 0) & (dist < WINDOW)
    for hh in range(heads_per_step):
        acc = jnp.zeros(bucket.shape, F32)
        for b in range(REL_BUCKETS):
            acc = jnp.where(bucket == b, rb_ref[b, h0 + hh], acc)
        o_ref[hh] = jnp.where(in_window, acc, -jnp.inf)


def _rel_bias_band(rel_bias, *, heads_per_step=8):
    nh = rel_bias.shape[1]
    bucket = jnp.asarray(_rel_bucket_band())
    return pl.pallas_call(
        functools.partial(_bias_body, heads_per_step=heads_per_step),
        grid=(nh // heads_per_step,),
        in_specs=[pl.BlockSpec(memory_space=pltpu.SMEM),
                  pl.BlockSpec((B_BLOCK, 2 * B_BLOCK), lambda i: (0, 0))],
        out_specs=pl.BlockSpec((heads_per_step, B_BLOCK, 2 * B_BLOCK), lambda i: (i, 0, 0)),
        out_shape=jax.ShapeDtypeStruct((nh, B_BLOCK, 2 * B_BLOCK), F32),
        compiler_params=pltpu.CompilerParams(dimension_semantics=("parallel",)),
        name="rel_bias_band",
    )(rel_bias, bucket)


def _head_split_matmul(xb, w, layer, n_cols, *, name, tm=MM_ROWS, tn=MM_COLS):
    s, kdim = xb.shape
    tm, tn = min(tm, s), min(tn, n_cols)
    hpt = tn // B_HEAD_DIM

    def ep(accs, extras, outs, rows):
        for c in range(hpt):
            outs[0][c, rows, :] = accs[0][:, c * B_HEAD_DIM:(c + 1) * B_HEAD_DIM].astype(BF16)

    outs = [((n_cols // B_HEAD_DIM, s, B_HEAD_DIM), BF16, (hpt, tm, B_HEAD_DIM),
             lambda i, j: (j, i, 0))]
    (out,) = _fused_matmul(xb, [(w, layer, 0)], n_cols, [], outs, ep, tm=tm, tn=tn, name=name)
    return out


def _attn_body(sink_ref, q_ref, kp_ref, kc_ref, vp_ref, vc_ref, bias_ref, o_ref, *, group, qb):
    h = pl.program_id(0)
    m_blk = pl.program_id(1)
    c = B_BLOCK
    k_all = jnp.concatenate([kp_ref[0], kc_ref[0]], axis=0)
    v_all = jnp.concatenate([vp_ref[0], vc_ref[0]], axis=0)
    kj = lax.broadcasted_iota(jnp.int32, (c, 2 * c), 1)
    for b in range(qb):
        q = q_ref[:, b * c:(b + 1) * c, :].reshape(group * c, B_HEAD_DIM) * B_LOGIT_SCALE
        kb = k_all[b * c:(b + 2) * c]
        vb = v_all[b * c:(b + 2) * c]
        logits = lax.dot_general(q, kb, (((1,), (1,)), ((), ())), preferred_element_type=F32)
        probs = []
        for gi in range(group):
            lg = logits[gi * c:(gi + 1) * c] + bias_ref[gi]
            if b == 0:
                lg = jnp.where((m_blk > 0) | (kj >= c), lg, -jnp.inf)
            sink = sink_ref[h * group + gi]
            m = jnp.maximum(jnp.max(lg, axis=-1, keepdims=True), sink)
            e = jnp.exp(lg - m)
            denom = jnp.sum(e, axis=-1, keepdims=True) + jnp.exp(sink - m)
            probs.append((e / denom).astype(BF16))
        pv = jnp.dot(jnp.concatenate(probs, axis=0), vb, preferred_element_type=F32)
        o_ref[b * c:(b + 1) * c, :] = jnp.concatenate(
            [pv[gi * c:(gi + 1) * c] for gi in range(group)], axis=1).astype(BF16)


def _swa_core(qh, kvh, sinks, bias, n_kv, *, qb=4):
    nq, s, hd = qh.shape
    group = nq // n_kv
    nb = s // B_BLOCK
    qb = min(qb, nb)
    assert nb % qb == 0
    prev = lambda m: jnp.maximum(m * qb - 1, 0)
    prev_blk = (1, B_BLOCK, hd)
    cur_blk = (1, qb * B_BLOCK, hd)
    in_specs = [
        pl.BlockSpec(memory_space=pltpu.SMEM),
        pl.BlockSpec((group, qb * B_BLOCK, hd), lambda h, m: (h, m, 0)),
        pl.BlockSpec(prev_blk, lambda h, m: (h, prev(m), 0)),
        pl.BlockSpec(cur_blk, lambda h, m: (h, m, 0)),
        pl.BlockSpec(prev_blk, lambda h, m: (h + n_kv, prev(m), 0)),
        pl.BlockSpec(cur_blk, lambda h, m: (h + n_kv, m, 0)),
        pl.BlockSpec((group, B_BLOCK, 2 * B_BLOCK), lambda h, m: (h, 0, 0)),
    ]
    vmem = 2 * (_nbytes((group, B_BLOCK, 2 * B_BLOCK), F32)
                + (group + 3) * _nbytes((qb * B_BLOCK, V7X_LANES), BF16)
                + _nbytes((qb * B_BLOCK, group * hd), BF16)) \
        + 6 * qb * _nbytes((group * B_BLOCK, 2 * B_BLOCK), F32)
    return pl.pallas_call(
        functools.partial(_attn_body, group=group, qb=qb),
        grid=(n_kv, nb // qb),
        in_specs=in_specs,
        out_specs=pl.BlockSpec((qb * B_BLOCK, group * hd), lambda h, m: (m, h)),
        out_shape=jax.ShapeDtypeStruct((s, nq * hd), BF16),
        compiler_params=pltpu.CompilerParams(dimension_semantics=("parallel", "parallel"),
                                             vmem_limit_bytes=_vmem_limit(vmem)),
        name="swa_core",
    )(sinks, qh, kvh, kvh, kvh, kvh, bias)


def _swiglu_up(xb, w1, w3, layer, *, tm=MM_ROWS, tn=MM_COLS):
    s, d = xb.shape
    f = w1.shape[2]
    blk, imap = _tile(min(tm, s), min(tn, f))

    def ep(accs, extras, outs, rows):
        outs[0][rows, :] = (_silu(accs[0]) * accs[1]).astype(BF16)

    (h,) = _fused_matmul(xb, [(w1, layer, 0), (w3, layer, 0)], f, [],
                         [((s, f), BF16, blk, imap)], ep, tm=tm, tn=tn, name="swiglu_up")
    return h


def _routing_tables(gates, idx, tm):
    s, ne = gates.shape
    n_assign = s * TOP_K
    ntiles = n_assign // tm + ne
    e_flat = idx.reshape(n_assign)
    onehot = (e_flat[:, None] == jnp.arange(ne, dtype=jnp.int32)[None, :]).astype(jnp.int32)
    rank = jnp.take_along_axis(jnp.cumsum(onehot, axis=0) - onehot, e_flat[:, None], axis=1)[:, 0]
    counts = jnp.sum(onehot, axis=0)
    padded = ((counts + tm - 1) // tm) * tm
    ends = jnp.cumsum(padded)
    pos = (ends - padded)[e_flat] + rank
    token = jnp.arange(n_assign, dtype=jnp.int32) // TOP_K
    gate = jnp.take_along_axis(gates, idx, axis=1)
    row_token = jnp.zeros((ntiles * tm,), jnp.int32).at[pos].set(token)
    tile_start = jnp.arange(ntiles, dtype=jnp.int32) * tm
    tile_expert = jnp.minimum(jnp.searchsorted(ends, tile_start, side="right"), ne - 1).astype(jnp.int32)
    tile_valid = (tile_start < ends[-1]).astype(jnp.int32)
    return row_token, gate, pos.reshape(s, TOP_K).astype(jnp.int32), tile_expert, tile_valid, ntiles


GATHER_ISSUE_UNROLL = 8


def _gather_body(tv_ref, tok_ref, tok_next_ref, x_hbm, o_ref, buf, sem, *, tm, ntiles):
    i = pl.program_id(0)
    slot = i % 2

    def row_copy(tile_tok_ref, slot_, r):
        return pltpu.make_async_copy(x_hbm.at[pl.ds(tile_tok_ref[0, 0, r], 1)],
                                     buf.at[slot_, pl.ds(r, 1)], sem.at[slot_])

    def start_tile(tile_tok_ref, slot_):
        def body(g, carry):
            for u in range(GATHER_ISSUE_UNROLL):
                row_copy(tile_tok_ref, slot_, g * GATHER_ISSUE_UNROLL + u).start(priority=u % 2)
            return carry
        lax.fori_loop(0, tm // GATHER_ISSUE_UNROLL, body, 0)

    @pl.when((i == 0) & (tv_ref[0] > 0))
    def _():
        start_tile(tok_ref, slot)

    @pl.when((i + 1 < ntiles) & (tv_ref[jnp.minimum(i + 1, ntiles - 1)] > 0))
    def _():
        start_tile(tok_next_ref, 1 - slot)

    @pl.when(tv_ref[i] > 0)
    def _():
        def wait_body(r, carry):
            row_copy(tok_ref, slot, r).wait()
            return carry
        lax.fori_loop(0, tm, wait_body, 0, unroll=GATHER_ISSUE_UNROLL)
        o_ref[...] = buf[slot].astype(BF16)

    @pl.when(tv_ref[i] == 0)
    def _():
        o_ref[...] = jnp.zeros_like(o_ref)


def _moe_gather(x, row_token, tile_valid, ntiles, tm):
    s, d = x.shape
    vmem = 2 * _nbytes((tm, d), F32) + 2 * _nbytes((tm, d), BF16) + _nbytes((tm, d), F32)
    tok = row_token.reshape(ntiles, 1, tm)
    grid_spec = pltpu.PrefetchScalarGridSpec(
        num_scalar_prefetch=1,
        grid=(ntiles,),
        in_specs=[pl.BlockSpec((1, 1, tm), lambda i, tv: (i, 0, 0), memory_space=pltpu.SMEM),
                  pl.BlockSpec((1, 1, tm), lambda i, tv: (jnp.minimum(i + 1, ntiles - 1), 0, 0),
                               memory_space=pltpu.SMEM),
                  pl.BlockSpec(memory_space=pl.ANY)],
        out_specs=pl.BlockSpec((tm, d), lambda i, tv: (i, 0)),
        scratch_shapes=[pltpu.VMEM((2, tm, d), F32), pltpu.SemaphoreType.DMA((2,))],
    )
    return pl.pallas_call(
        functools.partial(_gather_body, tm=tm, ntiles=ntiles),
        grid_spec=grid_spec,
        out_shape=jax.ShapeDtypeStruct((ntiles * tm, d), BF16),
        compiler_params=pltpu.CompilerParams(dimension_semantics=("arbitrary",),
                                             vmem_limit_bytes=_vmem_limit(vmem)),
        name="moe_gather",
    )(tile_valid, tok, tok, x)


def _expert_changed(te_ref, i):
    return (i == 0) | (te_ref[i] != te_ref[jnp.maximum(i - 1, 0)])


def _moe_up_body(te_ref, tv_ref, x_ref, w1_ref, w3_ref, o_ref, wb1, wb3):
    i = pl.program_id(1)

    @pl.when(_expert_changed(te_ref, i))
    def _():
        wb1[...] = w1_ref[...].astype(BF16)
        wb3[...] = w3_ref[...].astype(BF16)

    @pl.when(tv_ref[i] > 0)
    def _():
        xv = x_ref[...]
        a = jnp.dot(xv, wb1[...], preferred_element_type=F32)
        b = jnp.dot(xv, wb3[...], preferred_element_type=F32)
        o_ref[...] = (_silu(a) * b).astype(BF16)

    @pl.when(tv_ref[i] == 0)
    def _():
        o_ref[...] = jnp.zeros_like(o_ref)


def _moe_up(xg, w1, w3, layer, tile_expert, tile_valid, tm, *, tf=512):
    rows, d = xg.shape
    f = w1.shape[3]
    tf = min(tf, f)
    ntiles = rows // tm
    wspec = pl.BlockSpec((None, None, d, tf), lambda j, i, te, tv: (layer, te[i], 0, j))
    vmem = 2 * (_nbytes((tm, d), BF16) + 2 * _nbytes((d, tf), F32) + _nbytes((tm, tf), BF16)) \
        + 2 * _nbytes((d, tf), BF16) + 4 * _nbytes((tm, tf), F32)
    grid_spec = pltpu.PrefetchScalarGridSpec(
        num_scalar_prefetch=2,
        grid=(f // tf, ntiles),
        in_specs=[pl.BlockSpec((tm, d), lambda j, i, te, tv: (i, 0)), wspec, wspec],
        out_specs=pl.BlockSpec((tm, tf), lambda j, i, te, tv: (i, j)),
        scratch_shapes=[pltpu.VMEM((d, tf), BF16), pltpu.VMEM((d, tf), BF16)],
    )
    return pl.pallas_call(
        _moe_up_body,
        grid_spec=grid_spec,
        out_shape=jax.ShapeDtypeStruct((rows, f), BF16),
        compiler_params=pltpu.CompilerParams(dimension_semantics=("parallel", "arbitrary"),
                                             vmem_limit_bytes=_vmem_limit(vmem)),
        name="moe_up",
    )(tile_expert, tile_valid, xg, w1, w3)


def _moe_down_body(te_ref, tv_ref, h_ref, w_ref, o_ref, wb):
    i = pl.program_id(1)

    @pl.when(_expert_changed(te_ref, i))
    def _():
        wb[...] = w_ref[...].astype(BF16)

    @pl.when(tv_ref[i] > 0)
    def _():
        o_ref[...] = jnp.dot(h_ref[...], wb[...], preferred_element_type=F32)

    @pl.when(tv_ref[i] == 0)
    def _():
        o_ref[...] = jnp.zeros_like(o_ref)


def _moe_down(h, w2, layer, tile_expert, tile_valid, tm, *, tn=2048):
    rows, f = h.shape
    d = w2.shape[3]
    tn = min(tn, d)
    ntiles = rows // tm
    vmem = 2 * (_nbytes((tm, f), BF16) + _nbytes((f, tn), F32) + _nbytes((tm, tn), F32)) \
        + _nbytes((f, tn), BF16) + 2 * _nbytes((tm, tn), F32)
    grid_spec = pltpu.PrefetchScalarGridSpec(
        num_scalar_prefetch=2,
        grid=(d // tn, ntiles),
        in_specs=[pl.BlockSpec((tm, f), lambda j, i, te, tv: (i, 0)),
                  pl.BlockSpec((None, None, f, tn), lambda j, i, te, tv: (layer, te[i], 0, j))],
        out_specs=pl.BlockSpec((tm, tn), lambda j, i, te, tv: (i, j)),
        scratch_shapes=[pltpu.VMEM((f, tn), BF16)],
    )
    return pl.pallas_call(
        _moe_down_body,
        grid_spec=grid_spec,
        out_shape=jax.ShapeDtypeStruct((rows, d), F32),
        compiler_params=pltpu.CompilerParams(dimension_semantics=("parallel", "arbitrary"),
                                             vmem_limit_bytes=_vmem_limit(vmem)),
        name="moe_down",
    )(tile_expert, tile_valid, h, w2)


def _combine_body(pos_ref, pos_next_ref, y_hbm, x_ref, gate_ref, g_ref, b_ref, o32_ref, o16_ref,
                  buf, sem, *, tr, nt, alpha):
    i = pl.program_id(0)
    slot = i % 2

    def row_copy(tile_pos_ref, slot_, r, kk):
        return pltpu.make_async_copy(y_hbm.at[pl.ds(tile_pos_ref[0, kk, r], 1)],
                                     buf.at[slot_, kk, pl.ds(r, 1)], sem.at[slot_])

    def start_tile(tile_pos_ref, slot_):
        def body(r, carry):
            for kk in range(TOP_K):
                row_copy(tile_pos_ref, slot_, r, kk).start(priority=kk % 2)
            return carry
        lax.fori_loop(0, tr, body, 0, unroll=GATHER_ISSUE_UNROLL)

    @pl.when(i == 0)
    def _():
        start_tile(pos_ref, slot)

    @pl.when(i + 1 < nt)
    def _():
        start_tile(pos_next_ref, 1 - slot)

    def wait_body(r, carry):
        for kk in range(TOP_K):
            row_copy(pos_ref, slot, r, kk).wait()
        return carry
    lax.fori_loop(0, tr, wait_body, 0, unroll=GATHER_ISSUE_UNROLL)

    gate = gate_ref[...]
    f = buf[slot, 0] * gate[:, 0:1]
    for kk in range(1, TOP_K):
        f = f + buf[slot, kk] * gate[:, kk:kk + 1]
    out = _ln_rows(alpha * x_ref[...] + f, g_ref[...], b_ref[...])
    o32_ref[...] = out
    o16_ref[...] = out.astype(BF16)


def _moe_combine_ln(y, pos, gate, x, g, b, alpha, *, tr=256):
    s, d = x.shape
    tr = min(tr, s)
    nt = s // tr
    pos_t = pos.reshape(nt, tr, TOP_K).transpose(0, 2, 1)
    row = pl.BlockSpec((tr, d), lambda i: (i, 0))
    vec = pl.BlockSpec((1, d), lambda i: (0, 0))
    vmem = 2 * TOP_K * _nbytes((tr, d), F32) \
        + 2 * (2 * _nbytes((tr, d), F32) + _nbytes((tr, d), BF16)) + 3 * _nbytes((tr, d), F32)
    pos_blk = (1, TOP_K, tr)
    return pl.pallas_call(
        functools.partial(_combine_body, tr=tr, nt=nt, alpha=alpha),
        grid=(nt,),
        in_specs=[pl.BlockSpec(pos_blk, lambda i: (i, 0, 0), memory_space=pltpu.SMEM),
                  pl.BlockSpec(pos_blk, lambda i: (jnp.minimum(i + 1, nt - 1), 0, 0),
                               memory_space=pltpu.SMEM),
                  pl.BlockSpec(memory_space=pl.ANY), row,
                  pl.BlockSpec((tr, TOP_K), lambda i: (i, 0)), vec, vec],
        out_specs=[row, row],
        out_shape=[jax.ShapeDtypeStruct((s, d), F32), jax.ShapeDtypeStruct((s, d), BF16)],
        scratch_shapes=[pltpu.VMEM((2, TOP_K, tr, d), F32), pltpu.SemaphoreType.DMA((2,))],
        compiler_params=pltpu.CompilerParams(dimension_semantics=("arbitrary",),
                                             vmem_limit_bytes=_vmem_limit(vmem)),
        name="moe_combine_ln",
    )(pos_t, pos_t, y, x, gate, g.reshape(1, d), b.reshape(1, d))


def _ple(xb, x, p, w_gate, w_proj, layer, bi, *, tm=MM_ROWS, tn=MM_COLS):
    s, d = x.shape
    pdim = p.shape[-1]
    blk, imap = _tile(min(tm, s), min(tn, d))

    def ep(accs, extras, outs, rows):
        proj = jnp.dot(extras[1][rows, :].astype(BF16), extras[2][...].astype(BF16),
                       preferred_element_type=F32)
        out = extras[0][rows, :] + jax.nn.sigmoid(accs[0]) * proj
        outs[0][rows, :] = out
        outs[1][rows, :] = out.astype(BF16)

    extras = [(x, blk, imap),
              (p, (None, None, blk[0], pdim), lambda i, j: (layer, bi, i, 0)),
              (w_proj, (None, pdim, blk[1]), lambda i, j: (layer, 0, j))]
    outs = [((s, d), F32, blk, imap), ((s, d), BF16, blk, imap)]
    return _fused_matmul(xb, [(w_gate, layer, 0)], d, extras, outs, ep, tm=tm, tn=tn, name="ple")


MOE_TILE_ROWS = 512


def kernel(x, p, ln1_g, ln1_b, ln2_g, ln2_b, a_w_in, a_lb_logits, a_out_g, a_w_o, kv_w, b_w_q, b_sinks, b_w_o, rel_bias, ffn_w1, ffn_w3, ffn_w2, moe_router, moe_w1, moe_w3, moe_w2, ple_w_proj, ple_w_gate):
    batch, s, d = x.shape
    depth = p.shape[0]
    n_a = a_w_in.shape[0]
    kv_dim = kv_w.shape[1] // 2
    n_kv = kv_dim // B_HEAD_DIM
    alpha = (2.0 * depth) ** 0.25
    moe_tm = min(MOE_TILE_ROWS, s)
    bias_band = _rel_bias_band(rel_bias.astype(F32))
    lb_logits = a_lb_logits.astype(F32)

    outs = []
    for bi in range(batch):
        xf = x[bi].astype(F32)
        xb = xf.astype(BF16)
        kvh = None
        for i in range(depth):
            if i < n_a:
                q, k, v, log_f, g, chunk_decay = _hgrn_in_proj(xb, a_w_in, lb_logits, i)
                hb = _hgrn_core(q, k, v, log_f, g, a_out_g[i].astype(F32), chunk_decay)
                y = _residual_matmul(hb, a_w_o, i, xf, alpha, name="mixer_out")
            else:
                j = i - n_a
                qh = _head_split_matmul(xb, b_w_q, j, d, name="swa_q")
                hb = _swa_core(qh, kvh, b_sinks[j].astype(F32), bias_band, n_kv)
                y = _residual_matmul(hb, b_w_o, j, xf, alpha, name="mixer_out")
            li = i // 2
            if i % 2 == 0:
                xf, xb = _layer_norm(y, ln1_g[i].astype(F32), ln1_b[i].astype(F32))
                hmid = _swiglu_up(xb, ffn_w1, ffn_w3, li)
                y = _residual_matmul(hmid, ffn_w2, li, xf, alpha, name="ffn_down")
                xf, xb = _layer_norm(y, ln2_g[i].astype(F32), ln2_b[i].astype(F32))
            else:
                xf, xb, gates, idx = _layer_norm(y, ln1_g[i].astype(F32), ln1_b[i].astype(F32),
                                                 moe_router[li].astype(F32))
                row_token, gate, pos, tile_expert, tile_valid, ntiles = _routing_tables(
                    gates, idx, moe_tm)
                xg = _moe_gather(xf, row_token, tile_valid, ntiles, moe_tm)
                hmid = _moe_up(xg, moe_w1, moe_w3, li, tile_expert, tile_valid, moe_tm)
                yg = _moe_down(hmid, moe_w2, li, tile_expert, tile_valid, moe_tm)
                xf, xb = _moe_combine_ln(yg, pos, gate, xf, ln2_g[i].astype(F32),
                                         ln2_b[i].astype(F32), alpha)
            xf, xb = _ple(xb, xf, p, ple_w_gate, ple_w_proj, i, bi)
            if i == n_a - 1:
                kvh = _head_split_matmul(xb, kv_w, None, 2 * kv_dim, name="kv_proj")
        outs.append(xf)
    return jnp.stack(outs, axis=0).astype(x.dtype)
```

```python
import functools

import numpy as np
import jax
import jax.numpy as jnp
from jax import lax
from jax.experimental import pallas as pl
from jax.experimental.pallas import tpu as pltpu

F32 = jnp.float32
BF16 = jnp.bfloat16

V7X_LANES = 128
V7X_SCOPED_VMEM_BYTES = 60000 * 1024
COMPILER_SCRATCH_BYTES = 6 * 1024 * 1024

MM_ROWS = 2048
MM_COLS = 256
MM_SUB_ROWS = 128

A_HEAD_DIM = 128
A_CHUNK = 64
A_SUB = 16
A_CUMSUM_ROWS = 256
A_SAFE_LOG2_DECAY = 100.0
B_HEAD_DIM = 64
B_LOGIT_SCALE = B_HEAD_DIM ** -0.5
assert B_LOGIT_SCALE == 2.0 ** -3
B_BLOCK = 128
WINDOW = 128
REL_BUCKETS = 32
REL_MAX_DISTANCE = 128
TOP_K = 2
LN_EPS = 1e-5
RMS_EPS = 1e-6
LOG2E = 1.4426950408889634


def _vmem_limit(block_bytes):
    return int(min(block_bytes + COMPILER_SCRATCH_BYTES, V7X_SCOPED_VMEM_BYTES))


def _nbytes(shape, dtype):
    return int(np.prod([n for n in shape if n is not None])) * jnp.dtype(dtype).itemsize


def _silu(x):
    return x * jax.nn.sigmoid(x)


def _mm_body(*refs, nw, ne, no, epilogue):
    x_ref = refs[0]
    w_refs = refs[1:1 + nw]
    extra_refs = refs[1 + nw:1 + nw + ne]
    out_refs = refs[1 + nw + ne:1 + nw + ne + no]
    wbs = [w_ref[...].astype(BF16) for w_ref in w_refs]
    tm = x_ref.shape[0]
    sub = min(tm, MM_SUB_ROWS)
    for r0 in range(0, tm, sub):
        rows = slice(r0, r0 + sub)
        xv = x_ref[rows, :]
        prods = [jnp.dot(xv, wb, preferred_element_type=F32) for wb in wbs]
        epilogue(prods, extra_refs, out_refs, rows)


def _fused_matmul(x, ws, n_cols, extras, outs, epilogue, *, tm, tn, name, tk=None, k_block=0,
                  x_buffers=2):
    m, kdim = x.shape
    tk = tk or kdim
    tm, tn = min(tm, m), min(tn, n_cols)
    assert m % tm == 0 and n_cols % tn == 0 and kdim % tk == 0
    x_mode = {} if x_buffers == 2 else {"pipeline_mode": pl.Buffered(x_buffers)}
    in_specs = [pl.BlockSpec((tm, tk), lambda i, j: (i, k_block), **x_mode)]
    operands = [x]
    vmem = x_buffers * _nbytes((tm, tk), x.dtype)
    for w, layer, off in ws:
        if layer is None:
            spec = pl.BlockSpec((tk, tn), lambda i, j, off=off: (k_block, j + off))
        else:
            spec = pl.BlockSpec((None, tk, tn), lambda i, j, off=off, layer=layer:
                                (layer, k_block, j + off))
        in_specs.append(spec)
        operands.append(w)
        vmem += 2 * _nbytes((tk, tn), w.dtype) + _nbytes((tk, tn), BF16)
    for arr, bshape, imap in extras:
        in_specs.append(pl.BlockSpec(bshape, imap))
        operands.append(arr)
        vmem += 2 * _nbytes(bshape, arr.dtype)
    out_shapes, out_specs = [], []
    for shape, dtype, bshape, imap in outs:
        out_shapes.append(jax.ShapeDtypeStruct(shape, dtype))
        out_specs.append(pl.BlockSpec(bshape, imap))
        vmem += 2 * _nbytes(bshape, dtype)
    vmem += (2 + len(ws)) * _nbytes((tm, tn), F32)
    body = functools.partial(_mm_body, nw=len(ws), ne=len(extras), no=len(outs), epilogue=epilogue)
    return pl.pallas_call(
        body,
        grid=(m // tm, n_cols // tn),
        in_specs=in_specs,
        out_specs=out_specs,
        out_shape=out_shapes,
        compiler_params=pltpu.CompilerParams(dimension_semantics=("parallel", "parallel"),
                                             vmem_limit_bytes=_vmem_limit(vmem)),
        name=name,
    )(*operands)


def _tile(tm, tn):
    return (tm, tn), (lambda i, j: (i, j))


def _ln_rows(y, g, b):
    mu = jnp.mean(y, axis=-1, keepdims=True)
    yc = y - mu
    var = jnp.mean(yc * yc, axis=-1, keepdims=True)
    return yc * lax.rsqrt(var + LN_EPS) * g + b


def _route_top2(x, w_router):
    logits = jnp.dot(x, w_router, preferred_element_type=F32, precision=lax.Precision.HIGHEST)
    ne = logits.shape[1]
    eid = lax.broadcasted_iota(jnp.int32, logits.shape, 1)
    m1 = jnp.max(logits, axis=-1, keepdims=True)
    i1 = jnp.min(jnp.where(logits == m1, eid, ne), axis=-1, keepdims=True)
    rest = jnp.where(eid == i1, -jnp.inf, logits)
    m2 = jnp.max(rest, axis=-1, keepdims=True)
    i2 = jnp.min(jnp.where(rest == m2, eid, ne), axis=-1, keepdims=True)
    e2 = jnp.exp(m2 - m1)
    w_top1 = 1.0 / (1.0 + e2)
    w_top2 = e2 / (1.0 + e2)
    gates = jnp.where(eid == i1, w_top1, 0.0) + jnp.where(eid == i2, w_top2, 0.0)
    return gates, jnp.where(eid == 0, i1, i2)[:, :TOP_K]


def _ln_body(*refs, route):
    if route:
        y_ref, g_ref, b_ref, wr_ref, o32_ref, o16_ref, gates_ref, idx_ref = refs
    else:
        y_ref, g_ref, b_ref, o32_ref, o16_ref = refs
    out = _ln_rows(y_ref[...], g_ref[...], b_ref[...])
    o32_ref[...] = out
    o16_ref[...] = out.astype(BF16)
    if route:
        gates_ref[...], idx_ref[...] = _route_top2(out, wr_ref[...])


def _layer_norm(y, g, b, w_router=None, *, tr=256):
    s, d = y.shape
    tr = min(tr, s)
    assert s % tr == 0
    row = pl.BlockSpec((tr, d), lambda i: (i, 0))
    vec = pl.BlockSpec((1, d), lambda i: (0, 0))
    in_specs, operands = [row, vec, vec], [y, g.reshape(1, d), b.reshape(1, d)]
    out_specs = [row, row]
    out_shape = [jax.ShapeDtypeStruct((s, d), F32), jax.ShapeDtypeStruct((s, d), BF16)]
    vmem = 2 * (2 * _nbytes((tr, d), F32) + _nbytes((tr, d), BF16)) + 2 * _nbytes((tr, d), F32)
    if w_router is not None:
        ne = w_router.shape[1]
        in_specs.append(pl.BlockSpec((d, ne), lambda i: (0, 0)))
        operands.append(w_router)
        out_specs += [pl.BlockSpec((tr, ne), lambda i: (i, 0)),
                      pl.BlockSpec((tr, TOP_K), lambda i: (i, 0))]
        out_shape += [jax.ShapeDtypeStruct((s, ne), F32), jax.ShapeDtypeStruct((s, TOP_K), jnp.int32)]
        vmem += 2 * _nbytes((d, V7X_LANES), F32) + 4 * _nbytes((tr, d), F32)
    return pl.pallas_call(
        functools.partial(_ln_body, route=w_router is not None),
        grid=(s // tr,),
        in_specs=in_specs,
        out_specs=out_specs,
        out_shape=out_shape,
        compiler_params=pltpu.CompilerParams(dimension_semantics=("parallel",),
                                             vmem_limit_bytes=_vmem_limit(vmem)),
        name="layer_norm_route" if w_router is not None else "layer_norm",
    )(*operands)


def _lower_bound(lbl, layer):
    mx = jnp.max(lbl, axis=0, keepdims=True)
    e = jnp.exp(lbl - mx)
    return jnp.sum(e[:layer + 1], axis=0, keepdims=True) / jnp.sum(e, axis=0, keepdims=True)


def _hgrn_in_proj(xb, w_in, lb_logits, layer, *, tm=MM_ROWS, tn=MM_COLS):
    s, d = xb.shape
    nseg = d // min(tn, d)
    blk, imap = _tile(min(tm, s), min(tn, d))

    def ep_silu(accs, extras, outs, rows):
        outs[0][rows, :] = _silu(accs[0]).astype(BF16)

    def ep_id(accs, extras, outs, rows):
        outs[0][rows, :] = accs[0].astype(BF16)

    def ep_forget(accs, extras, outs, rows):
        lb = _lower_bound(extras[0][...], layer)
        f_raw = accs[0]
        e = jnp.exp(-jnp.abs(f_raw))
        log_sig = jnp.minimum(f_raw, 0.0) - jnp.log(1.0 + e)
        sig_neg = jnp.where(f_raw >= 0.0, e, 1.0) / (1.0 + e)
        a = jnp.log(lb)
        b = jnp.log1p(-lb) + log_sig
        log_f = jnp.maximum(a, b) + jnp.log(1.0 + jnp.exp(-jnp.abs(a - b)))
        outs[0][rows, :] = log_f
        outs[1][rows, :] = ((1.0 - lb) * sig_neg).astype(BF16)
        nrow, ncol = log_f.shape
        outs[2][rows.start // A_CHUNK:rows.stop // A_CHUNK, :] = jnp.sum(
            log_f.reshape(nrow // A_CHUNK, A_CHUNK, ncol), axis=1)

    def call(seg, epilogue, extras, out_dtypes, name, more_outs=(), x_buffers=2):
        outs = [((s, d), dt, blk, imap) for dt in out_dtypes] + list(more_outs)
        return _fused_matmul(xb, [(w_in, layer, seg * nseg)], d, extras, outs, epilogue,
                             tm=tm, tn=tn, name=name, x_buffers=x_buffers)

    nslot = lb_logits.shape[0]
    lb_extra = [(lb_logits, (nslot, blk[1]), lambda i, j: (0, j))]
    decay_out = ((s // A_CHUNK, d), F32, (blk[0] // A_CHUNK, blk[1]), imap)
    (q,) = call(0, ep_silu, [], [BF16], "hgrn_q")
    log_f, k, chunk_decay = call(1, ep_forget, lb_extra, [F32, BF16], "hgrn_f",
                                 more_outs=[decay_out])
    (v,) = call(2, ep_id, [], [BF16], "hgrn_v")
    (g,) = call(3, ep_silu, [], [BF16], "hgrn_g")
    return q, k, v, log_f, g, chunk_decay


def _chunk_cumsum(lf_ref, tc):
    tg = min(tc, A_CUMSUM_ROWS)
    row = lax.broadcasted_iota(jnp.int32, (tg, tg), 0)
    col = lax.broadcasted_iota(jnp.int32, (tg, tg), 1)
    tri = jnp.where((col <= row) & (col // A_CHUNK == row // A_CHUNK), 1.0, 0.0).astype(BF16)
    cums = []
    for gi in range(tc // tg):
        lf = lf_ref[gi * tg:(gi + 1) * tg, :]
        hi = lf.astype(BF16)
        r1 = lf - hi.astype(F32)
        mid = r1.astype(BF16)
        lo = (r1 - mid.astype(F32)).astype(BF16)
        cums.append(jnp.dot(tri, hi, preferred_element_type=F32)
                    + jnp.dot(tri, mid, preferred_element_type=F32)
                    + jnp.dot(tri, lo, preferred_element_type=F32))
    return jnp.concatenate(cums, axis=0)


def _hgrn_body(bounded_ref, q_ref, k_ref, v_ref, lf_ref, g_ref, og_ref, o_ref, st_ref, *, tc):
    @pl.when(pl.program_id(1) == 0)
    def _():
        st_ref[...] = jnp.zeros_like(st_ref)

    out_g = og_ref[...]
    half = A_SUB // 2
    lane = lax.broadcasted_iota(jnp.int32, (half, A_HEAD_DIM), 1)
    hrow = lax.broadcasted_iota(jnp.int32, (half, A_HEAD_DIM), 0)
    nsub = A_CHUNK // A_SUB
    nchunk = tc // A_CHUNK

    def finish(rows, o):
        o = o * lax.rsqrt(jnp.mean(o * o, axis=-1, keepdims=True) + RMS_EPS)
        o_ref[rows, :] = (o * out_g * g_ref[rows, :].astype(F32)).astype(BF16)

    def inter_chunk(ci, cum, st):
        rows = slice(ci * A_CHUNK, (ci + 1) * A_CHUNK)
        c2 = cum[rows] * LOG2E
        qc = q_ref[rows, :].astype(F32)
        kc = k_ref[rows, :].astype(F32)
        vc = v_ref[rows, :]
        last = c2[A_CHUNK - 1:A_CHUNK]
        qe = (qc * jnp.exp2(c2)).astype(BF16)
        o_inter = lax.dot_general(qe, st.astype(BF16), (((1,), (1,)), ((), ())),
                                  preferred_element_type=F32)
        kd = (kc * jnp.exp2(last - c2)).astype(BF16)
        upd = lax.dot_general(vc, kd, (((0,), (0,)), ((), ())), preferred_element_type=F32)
        return rows, c2, qc, kc, vc, qe, o_inter, st * jnp.exp2(last) + upd

    def run_factorised():
        c2 = _chunk_cumsum(lf_ref, tc) * LOG2E
        qe = (q_ref[...].astype(F32) * jnp.exp2(c2)).astype(BF16)
        kb32 = k_ref[...].astype(F32) * jnp.exp2(-c2)
        kb = kb32.astype(BF16)
        t_i = lax.broadcasted_iota(jnp.int32, (A_CHUNK, A_CHUNK), 0)
        s_i = lax.broadcasted_iota(jnp.int32, (A_CHUNK, A_CHUNK), 1)
        causal = t_i >= s_i
        st = st_ref[...]
        outs = []
        for ci in range(nchunk):
            rows = slice(ci * A_CHUNK, (ci + 1) * A_CHUNK)
            vc = v_ref[rows, :]
            grow = jnp.exp2(c2[(ci + 1) * A_CHUNK - 1:(ci + 1) * A_CHUNK])
            o_inter = lax.dot_general(qe[rows], st.astype(BF16), (((1,), (1,)), ((), ())),
                                      preferred_element_type=F32)
            kd = (kb32[rows] * grow).astype(BF16)
            upd = lax.dot_general(vc, kd, (((0,), (0,)), ((), ())), preferred_element_type=F32)
            st = st * grow + upd
            sc = lax.dot_general(qe[rows], kb[rows], (((1,), (1,)), ((), ())),
                                 preferred_element_type=F32)
            scores = jnp.where(causal, sc, 0.0).astype(BF16)
            outs.append(o_inter + jnp.dot(scores, vc, preferred_element_type=F32))
        st_ref[...] = st
        finish(slice(0, tc), jnp.concatenate(outs, axis=0))

    def direct_chunk(ci, cum, st):
        rows, c2, qc, kc, vc, qe, o_inter, st = inter_chunk(ci, cum, st)

        p_rows = []
        k_blocks, prev_ref = [], None
        for si in range(nsub):
            lo_r = si * A_SUB
            cs, qs, ks = c2[lo_r:lo_r + A_SUB], qc[lo_r:lo_r + A_SUB], kc[lo_r:lo_r + A_SUB]
            off = None
            if si > 0:
                ref = c2[lo_r - 1:lo_r]
                if prev_ref is not None:
                    rebase = jnp.exp2(ref - prev_ref)
                    k_blocks = [kb * rebase for kb in k_blocks]
                k_blocks.append(kc[lo_r - A_SUB:lo_r] * jnp.exp2(ref - c2[lo_r - A_SUB:lo_r]))
                prev_ref = ref
                k_t = jnp.concatenate(
                    k_blocks + [jnp.zeros((A_CHUNK - lo_r, A_HEAD_DIM), F32)], axis=0)
                q_t = (qs * jnp.exp2(cs - ref)).astype(BF16)
                off = lax.dot_general(q_t, k_t.astype(BF16), (((1,), (1,)), ((), ())),
                                      preferred_element_type=F32)
            c_h = (cs[:half], cs[half:])
            q_h = (qs[:half], qs[half:])
            p_h = [jnp.zeros((half, A_HEAD_DIM), F32), jnp.zeros((half, A_HEAD_DIM), F32)]
            for sj in range(A_SUB):
                for hh in range(sj // half, 2):
                    decay = jnp.exp2(c_h[hh] - cs[sj:sj + 1])
                    colv = jnp.sum(q_h[hh] * (ks[sj:sj + 1] * decay), axis=1, keepdims=True)
                    sel = lane == lo_r + sj
                    if sj // half == hh:
                        sel = sel & (hrow >= sj - hh * half)
                    p_h[hh] = jnp.where(sel, colv, p_h[hh])
            for hh in range(2):
                ph = p_h[hh][:, :A_CHUNK]
                p_rows.append(ph if off is None else ph + off[hh * half:(hh + 1) * half])
        scores = jnp.concatenate(p_rows, axis=0).astype(BF16)
        finish(rows, o_inter + jnp.dot(scores, vc, preferred_element_type=F32))
        return st

    def run_direct():
        cum = _chunk_cumsum(lf_ref, tc)
        st = st_ref[...]
        for ci in range(nchunk):
            st = direct_chunk(ci, cum, st)
        st_ref[...] = st

    bounded = bounded_ref[pl.program_id(0), pl.program_id(1)] > 0
    pl.when(bounded)(run_factorised)
    pl.when(jnp.logical_not(bounded))(run_direct)


def _hgrn_core(q, k, v, log_f, g, out_g, chunk_decay, *, tc=512):
    s, d = q.shape
    tc = min(tc, s)
    assert s % tc == 0 and tc % A_CHUNK == 0 and d % A_HEAD_DIM == 0
    nhead, nblk = d // A_HEAD_DIM, s // tc
    worst = jnp.min(chunk_decay.reshape(nblk, tc // A_CHUNK, nhead, A_HEAD_DIM), axis=(1, 3))
    bounded = (worst.T * LOG2E >= -A_SAFE_LOG2_DECAY).astype(jnp.int32)
    blk = pl.BlockSpec((tc, A_HEAD_DIM), lambda h, c, flags: (c, h))
    vmem = 2 * (4 * _nbytes((tc, A_HEAD_DIM), BF16) + 2 * _nbytes((tc, A_HEAD_DIM), F32)) \
        + 8 * _nbytes((A_CUMSUM_ROWS, A_CUMSUM_ROWS), F32) + 8 * _nbytes((tc, A_HEAD_DIM), F32)
    grid_spec = pltpu.PrefetchScalarGridSpec(
        num_scalar_prefetch=1,
        grid=(nhead, nblk),
        in_specs=[blk, blk, blk, blk, blk,
                  pl.BlockSpec((1, A_HEAD_DIM), lambda h, c, flags: (0, h))],
        out_specs=blk,
        scratch_shapes=[pltpu.VMEM((A_HEAD_DIM, A_HEAD_DIM), F32)],
    )
    return pl.pallas_call(
        functools.partial(_hgrn_body, tc=tc),
        grid_spec=grid_spec,
        out_shape=jax.ShapeDtypeStruct((s, d), BF16),
        compiler_params=pltpu.CompilerParams(dimension_semantics=("parallel", "arbitrary"),
                                             vmem_limit_bytes=_vmem_limit(vmem)),
        name="hgrn_core",
    )(bounded, q, k, v, log_f, g, out_g.reshape(1, d))


def _residual_matmul(hb, w, layer, resid, alpha, *, name, tm=MM_ROWS, tn=MM_COLS):
    s, d = resid.shape
    kdim = hb.shape[1]
    tk = min(kdim, d)
    blk, imap = _tile(min(tm, s), min(tn, d))
    y = resid
    for kb in range(kdim // tk):
        scale = alpha if kb == 0 else 1.0

        def ep(accs, extras, outs, rows, scale=scale):
            outs[0][rows, :] = scale * extras[0][rows, :] + accs[0]

        (y,) = _fused_matmul(hb, [(w, layer, 0)], d, [(y, blk, imap)], [((s, d), F32, blk, imap)],
                             ep, tm=tm, tn=tn, tk=tk, k_block=kb, name=name)
    return y


def _rel_bucket_band():
    i = np.arange(B_BLOCK)[:, None]
    j = np.arange(2 * B_BLOCK)[None, :]
    d = np.clip(B_BLOCK + i - j, 0, None)
    max_exact = REL_BUCKETS // 2
    large = max_exact + (np.log(np.maximum(d, 1) / max_exact)
                         / np.log(REL_MAX_DISTANCE / max_exact)
                         * (REL_BUCKETS - max_exact)).astype(np.int32)
    large = np.minimum(large, REL_BUCKETS - 1)
    return np.where(d < max_exact, d, large).astype(np.int32)


def _bias_body(rb_ref, bucket_ref, o_ref, *, heads_per_step):
    h0 = pl.program_id(0) * heads_per_step
    bucket = bucket_ref[...]
    qi = lax.broadcasted_iota(jnp.int32, bucket.shape, 0)
    kj = lax.broadcasted_iota(jnp.int32, bucket.shape, 1)
    dist = B_BLOCK + qi - kj
    in_window = (dist >= 0) & (dist < WINDOW)
    for hh in range(heads_per_step):
        acc = jnp.zeros(bucket.shape, F32)
        for b in range(REL_BUCKETS):
            acc = jnp.where(bucket == b, rb_ref[b, h0 + hh], acc)
        o_ref[hh] = jnp.where(in_window, acc, -jnp.inf)


def _rel_bias_band(rel_bias, *, heads_per_step=8):
    nh = rel_bias.shape[1]
    bucket = jnp.asarray(_rel_bucket_band())
    return pl.pallas_call(
        functools.partial(_bias_body, heads_per_step=heads_per_step),
        grid=(nh // heads_per_step,),
        in_specs=[pl.BlockSpec(memory_space=pltpu.SMEM),
                  pl.BlockSpec((B_BLOCK, 2 * B_BLOCK), lambda i: (0, 0))],
        out_specs=pl.BlockSpec((heads_per_step, B_BLOCK, 2 * B_BLOCK), lambda i: (i, 0, 0)),
        out_shape=jax.ShapeDtypeStruct((nh, B_BLOCK, 2 * B_BLOCK), F32),
        compiler_params=pltpu.CompilerParams(dimension_semantics=("parallel",)),
        name="rel_bias_band",
    )(rel_bias, bucket)


def _head_split_matmul(xb, w, layer, n_cols, *, name, tm=MM_ROWS, tn=MM_COLS):
    s, kdim = xb.shape
    tm, tn = min(tm, s), min(tn, n_cols)
    hpt = tn // B_HEAD_DIM

    def ep(accs, extras, outs, rows):
        for c in range(hpt):
            outs[0][c, rows, :] = accs[0][:, c * B_HEAD_DIM:(c + 1) * B_HEAD_DIM].astype(BF16)

    outs = [((n_cols // B_HEAD_DIM, s, B_HEAD_DIM), BF16, (hpt, tm, B_HEAD_DIM),
             lambda i, j: (j, i, 0))]
    (out,) = _fused_matmul(xb, [(w, layer, 0)], n_cols, [], outs, ep, tm=tm, tn=tn, name=name)
    return out


def _attn_body(sink_ref, q_ref, kp_ref, kc_ref, vp_ref, vc_ref, bias_ref, o_ref, *, group, qb):
    h = pl.program_id(0)
    m_blk = pl.program_id(1)
    c = B_BLOCK
    k_all = jnp.concatenate([kp_ref[0], kc_ref[0]], axis=0)
    v_all = jnp.concatenate([vp_ref[0], vc_ref[0]], axis=0)
    kj = lax.broadcasted_iota(jnp.int32, (c, 2 * c), 1)
    for b in range(qb):
        q = q_ref[:, b * c:(b + 1) * c, :].reshape(group * c, B_HEAD_DIM) * B_LOGIT_SCALE
        kb = k_all[b * c:(b + 2) * c]
        vb = v_all[b * c:(b + 2) * c]
        logits = lax.dot_general(q, kb, (((1,), (1,)), ((), ())), preferred_element_type=F32)
        probs = []
        for gi in range(group):
            lg = logits[gi * c:(gi + 1) * c] + bias_ref[gi]
            if b == 0:
                lg = jnp.where((m_blk > 0) | (kj >= c), lg, -jnp.inf)
            sink = sink_ref[h * group + gi]
            m = jnp.maximum(jnp.max(lg, axis=-1, keepdims=True), sink)
            e = jnp.exp(lg - m)
            denom = jnp.sum(e, axis=-1, keepdims=True) + jnp.exp(sink - m)
            probs.append((e / denom).astype(BF16))
        pv = jnp.dot(jnp.concatenate(probs, axis=0), vb, preferred_element_type=F32)
        o_ref[b * c:(b + 1) * c, :] = jnp.concatenate(
            [pv[gi * c:(gi + 1) * c] for gi in range(group)], axis=1).astype(BF16)


def _swa_core(qh, kvh, sinks, bias, n_kv, *, qb=4):
    nq, s, hd = qh.shape
    group = nq // n_kv
    nb = s // B_BLOCK
    qb = min(qb, nb)
    assert nb % qb == 0
    prev = lambda m: jnp.maximum(m * qb - 1, 0)
    prev_blk = (1, B_BLOCK, hd)
    cur_blk = (1, qb * B_BLOCK, hd)
    in_specs = [
        pl.BlockSpec(memory_space=pltpu.SMEM),
        pl.BlockSpec((group, qb * B_BLOCK, hd), lambda h, m: (h, m, 0)),
        pl.BlockSpec(prev_blk, lambda h, m: (h, prev(m), 0)),
        pl.BlockSpec(cur_blk, lambda h, m: (h, m, 0)),
        pl.BlockSpec(prev_blk, lambda h, m: (h + n_kv, prev(m), 0)),
        pl.BlockSpec(cur_blk, lambda h, m: (h + n_kv, m, 0)),
        pl.BlockSpec((group, B_BLOCK, 2 * B_BLOCK), lambda h, m: (h, 0, 0)),
    ]
    vmem = 2 * (_nbytes((group, B_BLOCK, 2 * B_BLOCK), F32)
                + (group + 3) * _nbytes((qb * B_BLOCK, V7X_LANES), BF16)
                + _nbytes((qb * B_BLOCK, group * hd), BF16)) \
        + 6 * qb * _nbytes((group * B_BLOCK, 2 * B_BLOCK), F32)
    return pl.pallas_call(
        functools.partial(_attn_body, group=group, qb=qb),
        grid=(n_kv, nb // qb),
        in_specs=in_specs,
        out_specs=pl.BlockSpec((qb * B_BLOCK, group * hd), lambda h, m: (m, h)),
        out_shape=jax.ShapeDtypeStruct((s, nq * hd), BF16),
        compiler_params=pltpu.CompilerParams(dimension_semantics=("parallel", "parallel"),
                                             vmem_limit_bytes=_vmem_limit(vmem)),
        name="swa_core",
    )(sinks, qh, kvh, kvh, kvh, kvh, bias)


def _swiglu_up(xb, w1, w3, layer, *, tm=MM_ROWS, tn=MM_COLS):
    s, d = xb.shape
    f = w1.shape[2]
    blk, imap = _tile(min(tm, s), min(tn, f))

    def ep(accs, extras, outs, rows):
        outs[0][rows, :] = (_silu(accs[0]) * accs[1]).astype(BF16)

    (h,) = _fused_matmul(xb, [(w1, layer, 0), (w3, layer, 0)], f, [],
                         [((s, f), BF16, blk, imap)], ep, tm=tm, tn=tn, name="swiglu_up")
    return h


def _routing_tables(gates, idx, tm):
    s, ne = gates.shape
    n_assign = s * TOP_K
    ntiles = n_assign // tm + ne
    e_flat = idx.reshape(n_assign)
    onehot = (e_flat[:, None] == jnp.arange(ne, dtype=jnp.int32)[None, :]).astype(jnp.int32)
    rank = jnp.take_along_axis(jnp.cumsum(onehot, axis=0) - onehot, e_flat[:, None], axis=1)[:, 0]
    counts = jnp.sum(onehot, axis=0)
    padded = ((counts + tm - 1) // tm) * tm
    ends = jnp.cumsum(padded)
    pos = (ends - padded)[e_flat] + rank
    token = jnp.arange(n_assign, dtype=jnp.int32) // TOP_K
    gate = jnp.take_along_axis(gates, idx, axis=1)
    row_token = jnp.zeros((ntiles * tm,), jnp.int32).at[pos].set(token)
    tile_start = jnp.arange(ntiles, dtype=jnp.int32) * tm
    tile_expert = jnp.minimum(jnp.searchsorted(ends, tile_start, side="right"), ne - 1).astype(jnp.int32)
    tile_valid = tile_start < ends[-1]
    first = tile_valid & jnp.concatenate(
        [jnp.ones((1,), bool), tile_expert[1:] != tile_expert[:-1]])
    slot = (jnp.cumsum(first.astype(jnp.int32)) - 1) % 2
    eid = jnp.arange(ne, dtype=jnp.int32)
    later = (counts > 0)[None, :] & (eid[None, :] > eid[:, None])
    next_used = jnp.min(jnp.where(later, eid[None, :], ne), axis=1)
    next_used = jnp.where(next_used == ne, -1, next_used)
    sched = jnp.stack([tile_expert, tile_valid.astype(jnp.int32), first.astype(jnp.int32),
                       slot.astype(jnp.int32), next_used[tile_expert]]).astype(jnp.int32)
    return row_token, gate, pos.reshape(s, TOP_K).astype(jnp.int32), sched, ntiles


GATHER_ISSUE_UNROLL = 8


def _gather_body(tv_ref, tok_ref, tok_next_ref, x_hbm, o_ref, buf, sem, *, tm, ntiles):
    i = pl.program_id(0)
    slot = i % 2

    def row_copy(tile_tok_ref, slot_, r):
        return pltpu.make_async_copy(x_hbm.at[pl.ds(tile_tok_ref[0, 0, r], 1)],
                                     buf.at[slot_, pl.ds(r, 1)], sem.at[slot_])

    def start_tile(tile_tok_ref, slot_):
        def body(g, carry):
            for u in range(GATHER_ISSUE_UNROLL):
                row_copy(tile_tok_ref, slot_, g * GATHER_ISSUE_UNROLL + u).start(priority=u % 2)
            return carry
        lax.fori_loop(0, tm // GATHER_ISSUE_UNROLL, body, 0)

    @pl.when((i == 0) & (tv_ref[0] > 0))
    def _():
        start_tile(tok_ref, slot)

    @pl.when((i + 1 < ntiles) & (tv_ref[jnp.minimum(i + 1, ntiles - 1)] > 0))
    def _():
        start_tile(tok_next_ref, 1 - slot)

    @pl.when(tv_ref[i] > 0)
    def _():
        def wait_body(r, carry):
            row_copy(tok_ref, slot, r).wait()
            return carry
        lax.fori_loop(0, tm, wait_body, 0, unroll=GATHER_ISSUE_UNROLL)
        o_ref[...] = buf[slot].astype(BF16)

    @pl.when(tv_ref[i] == 0)
    def _():
        o_ref[...] = jnp.zeros_like(o_ref)


def _moe_gather(x, row_token, tile_valid, ntiles, tm):
    s, d = x.shape
    vmem = 2 * _nbytes((tm, d), F32) + 2 * _nbytes((tm, d), BF16) + _nbytes((tm, d), F32)
    tok = row_token.reshape(ntiles, 1, tm)
    grid_spec = pltpu.PrefetchScalarGridSpec(
        num_scalar_prefetch=1,
        grid=(ntiles,),
        in_specs=[pl.BlockSpec((1, 1, tm), lambda i, tv: (i, 0, 0), memory_space=pltpu.SMEM),
                  pl.BlockSpec((1, 1, tm), lambda i, tv: (jnp.minimum(i + 1, ntiles - 1), 0, 0),
                               memory_space=pltpu.SMEM),
                  pl.BlockSpec(memory_space=pl.ANY)],
        out_specs=pl.BlockSpec((tm, d), lambda i, tv: (i, 0)),
        scratch_shapes=[pltpu.VMEM((2, tm, d), F32), pltpu.SemaphoreType.DMA((2,))],
    )
    return pl.pallas_call(
        functools.partial(_gather_body, tm=tm, ntiles=ntiles),
        grid_spec=grid_spec,
        out_shape=jax.ShapeDtypeStruct((ntiles * tm, d), BF16),
        compiler_params=pltpu.CompilerParams(dimension_semantics=("arbitrary",),
                                             vmem_limit_bytes=_vmem_limit(vmem)),
        name="moe_gather",
    )(tile_valid, tok, tok, x)


SCHED_EXPERT, SCHED_VALID, SCHED_FIRST, SCHED_SLOT, SCHED_NEXT = range(5)


def _grouped_body(sched_ref, x_ref, *rest, nw, layer, tcol, epilogue):
    w_hbm, o_ref, wbufs, sem = rest[:nw], rest[nw], rest[nw + 1:2 * nw + 1], rest[2 * nw + 1]
    j, i = pl.program_id(0), pl.program_id(1)
    expert, slot, nxt = sched_ref[SCHED_EXPERT, i], sched_ref[SCHED_SLOT, i], sched_ref[SCHED_NEXT, i]
    first = sched_ref[SCHED_FIRST, i] > 0
    col0 = pl.multiple_of(j * tcol, tcol)

    def copies(e, s):
        return [pltpu.make_async_copy(w.at[layer, e, :, pl.ds(col0, tcol)], wb.at[s], sem.at[s])
                for w, wb in zip(w_hbm, wbufs)]

    @pl.when(i == 0)
    def _():
        for c in copies(expert, slot):
            c.start()

    @pl.when(first & (nxt >= 0))
    def _():
        for c in copies(nxt, 1 - slot):
            c.start()

    @pl.when(first)
    def _():
        for c in copies(expert, slot):
            c.wait()

    @pl.when(sched_ref[SCHED_VALID, i] > 0)
    def _():
        wbs = [wb[slot].astype(BF16) for wb in wbufs]
        sub = min(x_ref.shape[0], MM_SUB_ROWS)
        for r0 in range(0, x_ref.shape[0], sub):
            xv = x_ref[r0:r0 + sub, :]
            o_ref[r0:r0 + sub, :] = epilogue(
                [jnp.dot(xv, wb, preferred_element_type=F32) for wb in wbs])

    @pl.when(sched_ref[SCHED_VALID, i] == 0)
    def _():
        o_ref[...] = jnp.zeros_like(o_ref)


def _grouped_matmul(x, ws, layer, sched, tm, tcol, out_dtype, epilogue, name):
    rows, kdim = x.shape
    n = ws[0].shape[3]
    tcol = min(tcol, n)
    ntiles = rows // tm
    vmem = 2 * _nbytes((tm, kdim), BF16) + 2 * _nbytes((tm, tcol), out_dtype) \
        + len(ws) * (2 * _nbytes((kdim, tcol), F32) + _nbytes((kdim, tcol), BF16)) \
        + (1 + len(ws)) * _nbytes((tm, tcol), F32)
    grid_spec = pltpu.PrefetchScalarGridSpec(
        num_scalar_prefetch=1,
        grid=(n // tcol, ntiles),
        in_specs=[pl.BlockSpec((tm, kdim), lambda j, i, sched: (i, 0))]
        + [pl.BlockSpec(memory_space=pl.ANY)] * len(ws),
        out_specs=pl.BlockSpec((tm, tcol), lambda j, i, sched: (i, j)),
        scratch_shapes=[pltpu.VMEM((2, kdim, tcol), F32) for _ in ws]
        + [pltpu.SemaphoreType.DMA((2,))],
    )
    return pl.pallas_call(
        functools.partial(_grouped_body, nw=len(ws), layer=layer, tcol=tcol, epilogue=epilogue),
        grid_spec=grid_spec,
        out_shape=jax.ShapeDtypeStruct((rows, n), out_dtype),
        compiler_params=pltpu.CompilerParams(dimension_semantics=("arbitrary", "arbitrary"),
                                             vmem_limit_bytes=_vmem_limit(vmem)),
        name=name,
    )(sched, x, *ws)


def _moe_up(xg, w1, w3, layer, sched, tm, *, tf=512):
    return _grouped_matmul(xg, [w1, w3], layer, sched, tm, tf, BF16,
                           lambda p: (_silu(p[0]) * p[1]).astype(BF16), "moe_up")


def _moe_down(h, w2, layer, sched, tm, *, tn=2048):
    return _grouped_matmul(h, [w2], layer, sched, tm, tn, F32, lambda p: p[0], "moe_down")


def _combine_body(pos_ref, pos_next_ref, y_hbm, x_ref, gate_ref, g_ref, b_ref, o32_ref, o16_ref,
                  buf, sem, *, tr, nt, alpha):
    i = pl.program_id(0)
    slot = i % 2

    def row_copy(tile_pos_ref, slot_, r, kk):
        return pltpu.make_async_copy(y_hbm.at[pl.ds(tile_pos_ref[0, kk, r], 1)],
                                     buf.at[slot_, kk, pl.ds(r, 1)], sem.at[slot_])

    def start_tile(tile_pos_ref, slot_):
        def body(r, carry):
            for kk in range(TOP_K):
                row_copy(tile_pos_ref, slot_, r, kk).start(priority=kk % 2)
            return carry
        lax.fori_loop(0, tr, body, 0, unroll=GATHER_ISSUE_UNROLL)

    @pl.when(i == 0)
    def _():
        start_tile(pos_ref, slot)

    @pl.when(i + 1 < nt)
    def _():
        start_tile(pos_next_ref, 1 - slot)

    def wait_body(r, carry):
        for kk in range(TOP_K):
            row_copy(pos_ref, slot, r, kk).wait()
        return carry
    lax.fori_loop(0, tr, wait_body, 0, unroll=GATHER_ISSUE_UNROLL)

    gate = gate_ref[...]
    f = buf[slot, 0] * gate[:, 0:1]
    for kk in range(1, TOP_K):
        f = f + buf[slot, kk] * gate[:, kk:kk + 1]
    out = _ln_rows(alpha * x_ref[...] + f, g_ref[...], b_ref[...])
    o32_ref[...] = out
    o16_ref[...] = out.astype(BF16)


def _moe_combine_ln(y, pos, gate, x, g, b, alpha, *, tr=256):
    s, d = x.shape
    tr = min(tr, s)
    nt = s // tr
    pos_t = pos.reshape(nt, tr, TOP_K).transpose(0, 2, 1)
    row = pl.BlockSpec((tr, d), lambda i: (i, 0))
    vec = pl.BlockSpec((1, d), lambda i: (0, 0))
    vmem = 2 * TOP_K * _nbytes((tr, d), F32) \
        + 2 * (2 * _nbytes((tr, d), F32) + _nbytes((tr, d), BF16)) + 3 * _nbytes((tr, d), F32)
    pos_blk = (1, TOP_K, tr)
    return pl.pallas_call(
        functools.partial(_combine_body, tr=tr, nt=nt, alpha=alpha),
        grid=(nt,),
        in_specs=[pl.BlockSpec(pos_blk, lambda i: (i, 0, 0), memory_space=pltpu.SMEM),
                  pl.BlockSpec(pos_blk, lambda i: (jnp.minimum(i + 1, nt - 1), 0, 0),
                               memory_space=pltpu.SMEM),
                  pl.BlockSpec(memory_space=pl.ANY), row,
                  pl.BlockSpec((tr, TOP_K), lambda i: (i, 0)), vec, vec],
        out_specs=[row, row],
        out_shape=[jax.ShapeDtypeStruct((s, d), F32), jax.ShapeDtypeStruct((s, d), BF16)],
        scratch_shapes=[pltpu.VMEM((2, TOP_K, tr, d), F32), pltpu.SemaphoreType.DMA((2,))],
        compiler_params=pltpu.CompilerParams(dimension_semantics=("arbitrary",),
                                             vmem_limit_bytes=_vmem_limit(vmem)),
        name="moe_combine_ln",
    )(pos_t, pos_t, y, x, gate, g.reshape(1, d), b.reshape(1, d))


def _ple(xb, x, p, w_gate, w_proj, layer, bi, *, tm=MM_ROWS, tn=MM_COLS):
    s, d = x.shape
    pdim = p.shape[-1]
    blk, imap = _tile(min(tm, s), min(tn, d))

    def ep(accs, extras, outs, rows):
        proj = jnp.dot(extras[1][rows, :].astype(BF16), extras[2][...].astype(BF16),
                       preferred_element_type=F32)
        out = extras[0][rows, :] + jax.nn.sigmoid(accs[0]) * proj
        outs[0][rows, :] = out
        outs[1][rows, :] = out.astype(BF16)

    extras = [(x, blk, imap),
              (p, (None, None, blk[0], pdim), lambda i, j: (layer, bi, i, 0)),
              (w_proj, (None, pdim, blk[1]), lambda i, j: (layer, 0, j))]
    outs = [((s, d), F32, blk, imap), ((s, d), BF16, blk, imap)]
    return _fused_matmul(xb, [(w_gate, layer, 0)], d, extras, outs, ep, tm=tm, tn=tn, name="ple")


MOE_TILE_ROWS = 512


def kernel(x, p, ln1_g, ln1_b, ln2_g, ln2_b, a_w_in, a_lb_logits, a_out_g, a_w_o, kv_w, b_w_q, b_sinks, b_w_o, rel_bias, ffn_w1, ffn_w3, ffn_w2, moe_router, moe_w1, moe_w3, moe_w2, ple_w_proj, ple_w_gate):
    batch, s, d = x.shape
    depth = p.shape[0]
    n_a = a_w_in.shape[0]
    kv_dim = kv_w.shape[1] // 2
    n_kv = kv_dim // B_HEAD_DIM
    alpha = (2.0 * depth) ** 0.25
    moe_tm = min(MOE_TILE_ROWS, s)
    bias_band = _rel_bias_band(rel_bias.astype(F32))
    lb_logits = a_lb_logits.astype(F32)

    outs = []
    for bi in range(batch):
        xf = x[bi].astype(F32)
        xb = xf.astype(BF16)
        kvh = None
        for i in range(depth):
            if i < n_a:
                q, k, v, log_f, g, chunk_decay = _hgrn_in_proj(xb, a_w_in, lb_logits, i)
                hb = _hgrn_core(q, k, v, log_f, g, a_out_g[i].astype(F32), chunk_decay)
                y = _residual_matmul(hb, a_w_o, i, xf, alpha, name="mixer_out")
            else:
                j = i - n_a
                qh = _head_split_matmul(xb, b_w_q, j, d, name="swa_q")
                hb = _swa_core(qh, kvh, b_sinks[j].astype(F32), bias_band, n_kv)
                y = _residual_matmul(hb, b_w_o, j, xf, alpha, name="mixer_out")
            li = i // 2
            if i % 2 == 0:
                xf, xb = _layer_norm(y, ln1_g[i].astype(F32), ln1_b[i].astype(F32))
                hmid = _swiglu_up(xb, ffn_w1, ffn_w3, li)
                y = _residual_matmul(hmid, ffn_w2, li, xf, alpha, name="ffn_down")
                xf, xb = _layer_norm(y, ln2_g[i].astype(F32), ln2_b[i].astype(F32))
            else:
                xf, xb, gates, idx = _layer_norm(y, ln1_g[i].astype(F32), ln1_b[i].astype(F32),
                                                 moe_router[li].astype(F32))
                row_token, gate, pos, sched, ntiles = _routing_tables(gates, idx, moe_tm)
                xg = _moe_gather(xf, row_token, sched[SCHED_VALID], ntiles, moe_tm)
                hmid = _moe_up(xg, moe_w1, moe_w3, li, sched, moe_tm)
                yg = _moe_down(hmid, moe_w2, li, sched, moe_tm)
                xf, xb = _moe_combine_ln(yg, pos, gate, xf, ln2_g[i].astype(F32),
                                         ln2_b[i].astype(F32), alpha)
            xf, xb = _ple(xb, xf, p, ple_w_gate, ple_w_proj, i, bi)
            if i == n_a - 1:
                kvh = _head_split_matmul(xb, kv_w, None, 2 * kv_dim, name="kv_proj")
        outs.append(xf)
    return jnp.stack(outs, axis=0).astype(x.dtype)
```

```python
import functools

import numpy as np
import jax
import jax.numpy as jnp
from jax import lax
from jax.experimental import pallas as pl
from jax.experimental.pallas import tpu as pltpu

F32 = jnp.float32
BF16 = jnp.bfloat16

V7X_LANES = 128
V7X_SCOPED_VMEM_BYTES = 60000 * 1024
COMPILER_SCRATCH_BYTES = 6 * 1024 * 1024

MM_ROWS = 2048
MM_COLS = 256
MM_SUB_ROWS = 128

A_HEAD_DIM = 128
A_CHUNK = 64
A_SUB = 16
A_SAFE_LOG2_DECAY = 100.0
B_HEAD_DIM = 64
B_LOGIT_SCALE = B_HEAD_DIM ** -0.5
assert B_LOGIT_SCALE == 2.0 ** -3
B_BLOCK = 128
WINDOW = 128
REL_BUCKETS = 32
REL_MAX_DISTANCE = 128
TOP_K = 2
LN_EPS = 1e-5
RMS_EPS = 1e-6
LOG2E = 1.4426950408889634


def _vmem_limit(block_bytes):
    return int(min(block_bytes + COMPILER_SCRATCH_BYTES, V7X_SCOPED_VMEM_BYTES))


def _nbytes(shape, dtype):
    return int(np.prod([n for n in shape if n is not None])) * jnp.dtype(dtype).itemsize


def _silu(x):
    return x * jax.nn.sigmoid(x)


def _mm_body(*refs, nw, ne, no, epilogue):
    x_ref = refs[0]
    w_refs = refs[1:1 + nw]
    extra_refs = refs[1 + nw:1 + nw + ne]
    out_refs = refs[1 + nw + ne:1 + nw + ne + no]
    wbs = [w_ref[...].astype(BF16) for w_ref in w_refs]
    tm = x_ref.shape[0]
    sub = min(tm, MM_SUB_ROWS)
    for r0 in range(0, tm, sub):
        rows = slice(r0, r0 + sub)
        xv = x_ref[rows, :]
        prods = [jnp.dot(xv, wb, preferred_element_type=F32) for wb in wbs]
        epilogue(prods, extra_refs, out_refs, rows)


def _fused_matmul(x, ws, n_cols, extras, outs, epilogue, *, tm, tn, name, tk=None, k_block=0,
                  x_buffers=2):
    m, kdim = x.shape
    tk = tk or kdim
    tm, tn = min(tm, m), min(tn, n_cols)
    assert m % tm == 0 and n_cols % tn == 0 and kdim % tk == 0
    x_mode = {} if x_buffers == 2 else {"pipeline_mode": pl.Buffered(x_buffers)}
    in_specs = [pl.BlockSpec((tm, tk), lambda i, j: (i, k_block), **x_mode)]
    operands = [x]
    vmem = x_buffers * _nbytes((tm, tk), x.dtype)
    for w, layer, off in ws:
        if layer is None:
            spec = pl.BlockSpec((tk, tn), lambda i, j, off=off: (k_block, j + off))
        else:
            spec = pl.BlockSpec((None, tk, tn), lambda i, j, off=off, layer=layer:
                                (layer, k_block, j + off))
        in_specs.append(spec)
        operands.append(w)
        vmem += 2 * _nbytes((tk, tn), w.dtype) + _nbytes((tk, tn), BF16)
    for arr, bshape, imap in extras:
        in_specs.append(pl.BlockSpec(bshape, imap))
        operands.append(arr)
        vmem += 2 * _nbytes(bshape, arr.dtype)
    out_shapes, out_specs = [], []
    for shape, dtype, bshape, imap in outs:
        out_shapes.append(jax.ShapeDtypeStruct(shape, dtype))
        out_specs.append(pl.BlockSpec(bshape, imap))
        vmem += 2 * _nbytes(bshape, dtype)
    vmem += (2 + len(ws)) * _nbytes((tm, tn), F32)
    body = functools.partial(_mm_body, nw=len(ws), ne=len(extras), no=len(outs), epilogue=epilogue)
    return pl.pallas_call(
        body,
        grid=(m // tm, n_cols // tn),
        in_specs=in_specs,
        out_specs=out_specs,
        out_shape=out_shapes,
        compiler_params=pltpu.CompilerParams(dimension_semantics=("parallel", "parallel"),
                                             vmem_limit_bytes=_vmem_limit(vmem)),
        name=name,
    )(*operands)


def _tile(tm, tn):
    return (tm, tn), (lambda i, j: (i, j))


def _ln_rows(y, g, b):
    mu = jnp.mean(y, axis=-1, keepdims=True)
    yc = y - mu
    var = jnp.mean(yc * yc, axis=-1, keepdims=True)
    return yc * lax.rsqrt(var + LN_EPS) * g + b


def _route_top2(x, w_router):
    logits = jnp.dot(x, w_router, preferred_element_type=F32, precision=lax.Precision.HIGHEST)
    ne = logits.shape[1]
    eid = lax.broadcasted_iota(jnp.int32, logits.shape, 1)
    m1 = jnp.max(logits, axis=-1, keepdims=True)
    i1 = jnp.min(jnp.where(logits == m1, eid, ne), axis=-1, keepdims=True)
    rest = jnp.where(eid == i1, -jnp.inf, logits)
    m2 = jnp.max(rest, axis=-1, keepdims=True)
    i2 = jnp.min(jnp.where(rest == m2, eid, ne), axis=-1, keepdims=True)
    e2 = jnp.exp(m2 - m1)
    w_top1 = 1.0 / (1.0 + e2)
    w_top2 = e2 / (1.0 + e2)
    gates = jnp.where(eid == i1, w_top1, 0.0) + jnp.where(eid == i2, w_top2, 0.0)
    return gates, jnp.where(eid == 0, i1, i2)[:, :TOP_K]


def _ln_body(*refs, route):
    if route:
        y_ref, g_ref, b_ref, wr_ref, o32_ref, o16_ref, gates_ref, idx_ref = refs
    else:
        y_ref, g_ref, b_ref, o32_ref, o16_ref = refs
    out = _ln_rows(y_ref[...], g_ref[...], b_ref[...])
    o32_ref[...] = out
    o16_ref[...] = out.astype(BF16)
    if route:
        gates_ref[...], idx_ref[...] = _route_top2(out, wr_ref[...])


def _layer_norm(y, g, b, w_router=None, *, tr=256):
    s, d = y.shape
    tr = min(tr, s)
    assert s % tr == 0
    row = pl.BlockSpec((tr, d), lambda i: (i, 0))
    vec = pl.BlockSpec((1, d), lambda i: (0, 0))
    in_specs, operands = [row, vec, vec], [y, g.reshape(1, d), b.reshape(1, d)]
    out_specs = [row, row]
    out_shape = [jax.ShapeDtypeStruct((s, d), F32), jax.ShapeDtypeStruct((s, d), BF16)]
    vmem = 2 * (2 * _nbytes((tr, d), F32) + _nbytes((tr, d), BF16)) + 2 * _nbytes((tr, d), F32)
    if w_router is not None:
        ne = w_router.shape[1]
        in_specs.append(pl.BlockSpec((d, ne), lambda i: (0, 0)))
        operands.append(w_router)
        out_specs += [pl.BlockSpec((tr, ne), lambda i: (i, 0)),
                      pl.BlockSpec((tr, TOP_K), lambda i: (i, 0))]
        out_shape += [jax.ShapeDtypeStruct((s, ne), F32), jax.ShapeDtypeStruct((s, TOP_K), jnp.int32)]
        vmem += 2 * _nbytes((d, V7X_LANES), F32) + 4 * _nbytes((tr, d), F32)
    return pl.pallas_call(
        functools.partial(_ln_body, route=w_router is not None),
        grid=(s // tr,),
        in_specs=in_specs,
        out_specs=out_specs,
        out_shape=out_shape,
        compiler_params=pltpu.CompilerParams(dimension_semantics=("parallel",),
                                             vmem_limit_bytes=_vmem_limit(vmem)),
        name="layer_norm_route" if w_router is not None else "layer_norm",
    )(*operands)


def _lower_bound(lbl, layer):
    mx = jnp.max(lbl, axis=0, keepdims=True)
    e = jnp.exp(lbl - mx)
    return jnp.sum(e[:layer + 1], axis=0, keepdims=True) / jnp.sum(e, axis=0, keepdims=True)


def _hgrn_in_proj(xb, w_in, lb_logits, layer, *, tm=MM_ROWS, tn=MM_COLS):
    s, d = xb.shape
    nseg = d // min(tn, d)
    blk, imap = _tile(min(tm, s), min(tn, d))

    def ep_silu(accs, extras, outs, rows):
        outs[0][rows, :] = _silu(accs[0]).astype(BF16)

    def ep_id(accs, extras, outs, rows):
        outs[0][rows, :] = accs[0].astype(BF16)

    def ep_forget(accs, extras, outs, rows):
        lb = _lower_bound(extras[0][...], layer)
        f_raw = accs[0]
        e = jnp.exp(-jnp.abs(f_raw))
        log_sig = jnp.minimum(f_raw, 0.0) - jnp.log(1.0 + e)
        sig_neg = jnp.where(f_raw >= 0.0, e, 1.0) / (1.0 + e)
        a = jnp.log(lb)
        b = jnp.log1p(-lb) + log_sig
        log_f = jnp.maximum(a, b) + jnp.log(1.0 + jnp.exp(-jnp.abs(a - b)))
        outs[0][rows, :] = log_f
        outs[1][rows, :] = ((1.0 - lb) * sig_neg).astype(BF16)
        nrow, ncol = log_f.shape
        outs[2][rows.start // A_CHUNK:rows.stop // A_CHUNK, :] = jnp.sum(
            log_f.reshape(nrow // A_CHUNK, A_CHUNK, ncol), axis=1)

    def call(seg, epilogue, extras, out_dtypes, name, more_outs=(), x_buffers=2):
        outs = [((s, d), dt, blk, imap) for dt in out_dtypes] + list(more_outs)
        return _fused_matmul(xb, [(w_in, layer, seg * nseg)], d, extras, outs, epilogue,
                             tm=tm, tn=tn, name=name, x_buffers=x_buffers)

    nslot = lb_logits.shape[0]
    lb_extra = [(lb_logits, (nslot, blk[1]), lambda i, j: (0, j))]
    decay_out = ((s // A_CHUNK, d), F32, (blk[0] // A_CHUNK, blk[1]), imap)
    (q,) = call(0, ep_silu, [], [BF16], "hgrn_q")
    log_f, k, chunk_decay = call(1, ep_forget, lb_extra, [F32, BF16], "hgrn_f",
                                 more_outs=[decay_out])
    (v,) = call(2, ep_id, [], [BF16], "hgrn_v")
    (g,) = call(3, ep_silu, [], [BF16], "hgrn_g")
    return q, k, v, log_f, g, chunk_decay


def _chunk_cumsum(lf_ref, tc):
    x = lf_ref[...]
    row_in_chunk = lax.broadcasted_iota(jnp.int32, x.shape, 0) % A_CHUNK
    shift = 1
    while shift < A_CHUNK:
        x = x + jnp.where(row_in_chunk >= shift, pltpu.roll(x, shift, axis=0), 0.0)
        shift *= 2
    return x


def _hgrn_body(bounded_ref, q_ref, k_ref, v_ref, lf_ref, g_ref, og_ref, o_ref, st_ref, *, tc):
    @pl.when(pl.program_id(1) == 0)
    def _():
        st_ref[...] = jnp.zeros_like(st_ref)

    out_g = og_ref[...]
    half = A_SUB // 2
    lane = lax.broadcasted_iota(jnp.int32, (half, A_HEAD_DIM), 1)
    hrow = lax.broadcasted_iota(jnp.int32, (half, A_HEAD_DIM), 0)
    nsub = A_CHUNK // A_SUB
    nchunk = tc // A_CHUNK

    def finish(rows, o):
        o = o * lax.rsqrt(jnp.mean(o * o, axis=-1, keepdims=True) + RMS_EPS)
        o_ref[rows, :] = (o * out_g * g_ref[rows, :].astype(F32)).astype(BF16)

    def as_column(row):
        return jnp.transpose(jnp.broadcast_to(row, (8, row.shape[1])))[:, :1]

    def inter_chunk(ci, cum, st_t):
        rows = slice(ci * A_CHUNK, (ci + 1) * A_CHUNK)
        c2 = cum[rows] * LOG2E
        qc = q_ref[rows, :].astype(F32)
        kc = k_ref[rows, :].astype(F32)
        vc = v_ref[rows, :]
        last = c2[A_CHUNK - 1:A_CHUNK]
        qe = (qc * jnp.exp2(c2)).astype(BF16)
        o_inter = lax.dot_general(qe, st_t.astype(BF16), (((1,), (1,)), ((), ())),
                                  preferred_element_type=F32)
        kd = (kc * jnp.exp2(last - c2)).astype(BF16)
        upd = lax.dot_general(vc, kd, (((0,), (0,)), ((), ())), preferred_element_type=F32)
        return rows, c2, qc, kc, vc, qe, o_inter, st_t * jnp.exp2(last) + upd

    def run_factorised():
        c2 = _chunk_cumsum(lf_ref, tc) * LOG2E
        qe = (q_ref[...].astype(F32) * jnp.exp2(c2)).astype(BF16)
        kb32 = k_ref[...].astype(F32) * jnp.exp2(-c2)
        kb = kb32.astype(BF16)
        t_i = lax.broadcasted_iota(jnp.int32, (A_CHUNK, A_CHUNK), 0)
        s_i = lax.broadcasted_iota(jnp.int32, (A_CHUNK, A_CHUNK), 1)
        causal = t_i >= s_i
        st = st_ref[...]
        outs = []
        for ci in range(nchunk):
            rows = slice(ci * A_CHUNK, (ci + 1) * A_CHUNK)
            vc = v_ref[rows, :]
            grow = jnp.exp2(c2[(ci + 1) * A_CHUNK - 1:(ci + 1) * A_CHUNK])
            sc = lax.dot_general(qe[rows], kb[rows], (((1,), (1,)), ((), ())),
                                 preferred_element_type=F32)
            scores = jnp.where(causal, sc, 0.0).astype(BF16)
            outs.append(jnp.dot(jnp.concatenate([qe[rows], scores], axis=1),
                                jnp.concatenate([st.astype(BF16), vc], axis=0),
                                preferred_element_type=F32))
            kd = (kb32[rows] * grow).astype(BF16)
            upd = lax.dot_general(kd, vc, (((0,), (0,)), ((), ())), preferred_element_type=F32)
            st = st * as_column(grow) + upd
        st_ref[...] = st
        finish(slice(0, tc), jnp.concatenate(outs, axis=0))

    def direct_chunk(ci, cum, st):
        rows, c2, qc, kc, vc, qe, o_inter, st = inter_chunk(ci, cum, st)

        p_rows = []
        k_blocks, prev_ref = [], None
        for si in range(nsub):
            lo_r = si * A_SUB
            cs, qs, ks = c2[lo_r:lo_r + A_SUB], qc[lo_r:lo_r + A_SUB], kc[lo_r:lo_r + A_SUB]
            off = None
            if si > 0:
                ref = c2[lo_r - 1:lo_r]
                if prev_ref is not None:
                    rebase = jnp.exp2(ref - prev_ref)
                    k_blocks = [kb * rebase for kb in k_blocks]
                k_blocks.append(kc[lo_r - A_SUB:lo_r] * jnp.exp2(ref - c2[lo_r - A_SUB:lo_r]))
                prev_ref = ref
                k_t = jnp.concatenate(
                    k_blocks + [jnp.zeros((A_CHUNK - lo_r, A_HEAD_DIM), F32)], axis=0)
                q_t = (qs * jnp.exp2(cs - ref)).astype(BF16)
                off = lax.dot_general(q_t, k_t.astype(BF16), (((1,), (1,)), ((), ())),
                                      preferred_element_type=F32)
            c_h = (cs[:half], cs[half:])
            q_h = (qs[:half], qs[half:])
            p_h = [jnp.zeros((half, A_HEAD_DIM), F32), jnp.zeros((half, A_HEAD_DIM), F32)]
            for sj in range(A_SUB):
                for hh in range(sj // half, 2):
                    decay = jnp.exp2(c_h[hh] - cs[sj:sj + 1])
                    colv = jnp.sum(q_h[hh] * (ks[sj:sj + 1] * decay), axis=1, keepdims=True)
                    sel = lane == lo_r + sj
                    if sj // half == hh:
                        sel = sel & (hrow >= sj - hh * half)
                    p_h[hh] = jnp.where(sel, colv, p_h[hh])
            for hh in range(2):
                ph = p_h[hh][:, :A_CHUNK]
                p_rows.append(ph if off is None else ph + off[hh * half:(hh + 1) * half])
        scores = jnp.concatenate(p_rows, axis=0).astype(BF16)
        finish(rows, o_inter + jnp.dot(scores, vc, preferred_element_type=F32))
        return st

    def run_direct():
        cum = _chunk_cumsum(lf_ref, tc)
        st_t = st_ref[...].T
        for ci in range(nchunk):
            st_t = direct_chunk(ci, cum, st_t)
        st_ref[...] = st_t.T

    bounded = bounded_ref[pl.program_id(0), pl.program_id(1)] > 0
    pl.when(bounded)(run_factorised)
    pl.when(jnp.logical_not(bounded))(run_direct)


def _hgrn_core(q, k, v, log_f, g, out_g, chunk_decay, *, tc=512):
    s, d = q.shape
    tc = min(tc, s)
    assert s % tc == 0 and tc % A_CHUNK == 0 and d % A_HEAD_DIM == 0
    nhead, nblk = d // A_HEAD_DIM, s // tc
    worst = jnp.min(chunk_decay.reshape(nblk, tc // A_CHUNK, nhead, A_HEAD_DIM), axis=(1, 3))
    bounded = (worst.T * LOG2E >= -A_SAFE_LOG2_DECAY).astype(jnp.int32)
    blk = pl.BlockSpec((tc, A_HEAD_DIM), lambda h, c, flags: (c, h))
    vmem = 2 * (4 * _nbytes((tc, A_HEAD_DIM), BF16) + 2 * _nbytes((tc, A_HEAD_DIM), F32)) \
        + 12 * _nbytes((tc, A_HEAD_DIM), F32)
    grid_spec = pltpu.PrefetchScalarGridSpec(
        num_scalar_prefetch=1,
        grid=(nhead, nblk),
        in_specs=[blk, blk, blk, blk, blk,
                  pl.BlockSpec((1, A_HEAD_DIM), lambda h, c, flags: (0, h))],
        out_specs=blk,
        scratch_shapes=[pltpu.VMEM((A_HEAD_DIM, A_HEAD_DIM), F32)],
    )
    return pl.pallas_call(
        functools.partial(_hgrn_body, tc=tc),
        grid_spec=grid_spec,
        out_shape=jax.ShapeDtypeStruct((s, d), BF16),
        compiler_params=pltpu.CompilerParams(dimension_semantics=("parallel", "arbitrary"),
                                             vmem_limit_bytes=_vmem_limit(vmem)),
        name="hgrn_core",
    )(bounded, q, k, v, log_f, g, out_g.reshape(1, d))


def _residual_matmul(hb, w, layer, resid, alpha, *, name, tm=MM_ROWS, tn=MM_COLS):
    s, d = resid.shape
    kdim = hb.shape[1]
    tk = min(kdim, d)
    blk, imap = _tile(min(tm, s), min(tn, d))
    y = resid
    for kb in range(kdim // tk):
        scale = alpha if kb == 0 else 1.0

        def ep(accs, extras, outs, rows, scale=scale):
            outs[0][rows, :] = scale * extras[0][rows, :] + accs[0]

        (y,) = _fused_matmul(hb, [(w, layer, 0)], d, [(y, blk, imap)], [((s, d), F32, blk, imap)],
                             ep, tm=tm, tn=tn, tk=tk, k_block=kb, name=name)
    return y


def _rel_bucket_band():
    i = np.arange(B_BLOCK)[:, None]
    j = np.arange(2 * B_BLOCK)[None, :]
    d = np.clip(B_BLOCK + i - j, 0, None)
    max_exact = REL_BUCKETS // 2
    large = max_exact + (np.log(np.maximum(d, 1) / max_exact)
                         / np.log(REL_MAX_DISTANCE / max_exact)
                         * (REL_BUCKETS - max_exact)).astype(np.int32)
    large = np.minimum(large, REL_BUCKETS - 1)
    return np.where(d < max_exact, d, large).astype(np.int32)


def _bias_body(rb_ref, bucket_ref, o_ref, *, heads_per_step):
    h0 = pl.program_id(0) * heads_per_step
    bucket = bucket_ref[...]
    qi = lax.broadcasted_iota(jnp.int32, bucket.shape, 0)
    kj = lax.broadcasted_iota(jnp.int32, bucket.shape, 1)
    dist = B_BLOCK + qi - kj
    in_window = (dist >= 0) & (dist < WINDOW)
    for hh in range(heads_per_step):
        acc = jnp.zeros(bucket.shape, F32)
        for b in range(REL_BUCKETS):
            acc = jnp.where(bucket == b, rb_ref[b, h0 + hh], acc)
        o_ref[hh] = jnp.where(in_window, acc, -jnp.inf)


def _rel_bias_band(rel_bias, *, heads_per_step=8):
    nh = rel_bias.shape[1]
    bucket = jnp.asarray(_rel_bucket_band())
    return pl.pallas_call(
        functools.partial(_bias_body, heads_per_step=heads_per_step),
        grid=(nh // heads_per_step,),
        in_specs=[pl.BlockSpec(memory_space=pltpu.SMEM),
                  pl.BlockSpec((B_BLOCK, 2 * B_BLOCK), lambda i: (0, 0))],
        out_specs=pl.BlockSpec((heads_per_step, B_BLOCK, 2 * B_BLOCK), lambda i: (i, 0, 0)),
        out_shape=jax.ShapeDtypeStruct((nh, B_BLOCK, 2 * B_BLOCK), F32),
        compiler_params=pltpu.CompilerParams(dimension_semantics=("parallel",)),
        name="rel_bias_band",
    )(rel_bias, bucket)


def _head_split_matmul(xb, w, layer, n_cols, *, name, tm=MM_ROWS, tn=MM_COLS):
    s, kdim = xb.shape
    tm, tn = min(tm, s), min(tn, n_cols)
    hpt = tn // B_HEAD_DIM

    def ep(accs, extras, outs, rows):
        for c in range(hpt):
            outs[0][c, rows, :] = accs[0][:, c * B_HEAD_DIM:(c + 1) * B_HEAD_DIM].astype(BF16)

    outs = [((n_cols // B_HEAD_DIM, s, B_HEAD_DIM), BF16, (hpt, tm, B_HEAD_DIM),
             lambda i, j: (j, i, 0))]
    (out,) = _fused_matmul(xb, [(w, layer, 0)], n_cols, [], outs, ep, tm=tm, tn=tn, name=name)
    return out


def _attn_body(sink_ref, q_ref, kp_ref, kc_ref, vp_ref, vc_ref, bias_ref, o_ref, *, group, qb):
    h = pl.program_id(0)
    m_blk = pl.program_id(1)
    c = B_BLOCK
    k_all = jnp.concatenate([kp_ref[0], kc_ref[0]], axis=0)
    v_all = jnp.concatenate([vp_ref[0], vc_ref[0]], axis=0)
    kj = lax.broadcasted_iota(jnp.int32, (c, 2 * c), 1)
    for b in range(qb):
        q = q_ref[:, b * c:(b + 1) * c, :].reshape(group * c, B_HEAD_DIM) * B_LOGIT_SCALE
        kb = k_all[b * c:(b + 2) * c]
        vb = v_all[b * c:(b + 2) * c]
        logits = lax.dot_general(q, kb, (((1,), (1,)), ((), ())), preferred_element_type=F32)
        probs = []
        for gi in range(group):
            lg = logits[gi * c:(gi + 1) * c] + bias_ref[gi]
            if b == 0:
                lg = jnp.where((m_blk > 0) | (kj >= c), lg, -jnp.inf)
            sink = sink_ref[h * group + gi]
            m = jnp.maximum(jnp.max(lg, axis=-1, keepdims=True), sink)
            e = jnp.exp(lg - m)
            denom = jnp.sum(e, axis=-1, keepdims=True) + jnp.exp(sink - m)
            probs.append((e / denom).astype(BF16))
        pv = jnp.dot(jnp.concatenate(probs, axis=0), vb, preferred_element_type=F32)
        o_ref[b * c:(b + 1) * c, :] = jnp.concatenate(
            [pv[gi * c:(gi + 1) * c] for gi in range(group)], axis=1).astype(BF16)


def _swa_core(qh, kvh, sinks, bias, n_kv, *, qb=4):
    nq, s, hd = qh.shape
    group = nq // n_kv
    nb = s // B_BLOCK
    qb = min(qb, nb)
    assert nb % qb == 0
    prev = lambda m: jnp.maximum(m * qb - 1, 0)
    prev_blk = (1, B_BLOCK, hd)
    cur_blk = (1, qb * B_BLOCK, hd)
    in_specs = [
        pl.BlockSpec(memory_space=pltpu.SMEM),
        pl.BlockSpec((group, qb * B_BLOCK, hd), lambda h, m: (h, m, 0)),
        pl.BlockSpec(prev_blk, lambda h, m: (h, prev(m), 0)),
        pl.BlockSpec(cur_blk, lambda h, m: (h, m, 0)),
        pl.BlockSpec(prev_blk, lambda h, m: (h + n_kv, prev(m), 0)),
        pl.BlockSpec(cur_blk, lambda h, m: (h + n_kv, m, 0)),
        pl.BlockSpec((group, B_BLOCK, 2 * B_BLOCK), lambda h, m: (h, 0, 0)),
    ]
    vmem = 2 * (_nbytes((group, B_BLOCK, 2 * B_BLOCK), F32)
                + (group + 3) * _nbytes((qb * B_BLOCK, V7X_LANES), BF16)
                + _nbytes((qb * B_BLOCK, group * hd), BF16)) \
        + 6 * qb * _nbytes((group * B_BLOCK, 2 * B_BLOCK), F32)
    return pl.pallas_call(
        functools.partial(_attn_body, group=group, qb=qb),
        grid=(n_kv, nb // qb),
        in_specs=in_specs,
        out_specs=pl.BlockSpec((qb * B_BLOCK, group * hd), lambda h, m: (m, h)),
        out_shape=jax.ShapeDtypeStruct((s, nq * hd), BF16),
        compiler_params=pltpu.CompilerParams(dimension_semantics=("parallel", "parallel"),
                                             vmem_limit_bytes=_vmem_limit(vmem)),
        name="swa_core",
    )(sinks, qh, kvh, kvh, kvh, kvh, bias)


def _swiglu_up(xb, w1, w3, layer, *, tm=MM_ROWS, tn=MM_COLS):
    s, d = xb.shape
    f = w1.shape[2]
    blk, imap = _tile(min(tm, s), min(tn, f))

    def ep(accs, extras, outs, rows):
        outs[0][rows, :] = (_silu(accs[0]) * accs[1]).astype(BF16)

    (h,) = _fused_matmul(xb, [(w1, layer, 0), (w3, layer, 0)], f, [],
                         [((s, f), BF16, blk, imap)], ep, tm=tm, tn=tn, name="swiglu_up")
    return h


def _routing_tables(gates, idx, tm):
    s, ne = gates.shape
    n_assign = s * TOP_K
    ntiles = n_assign // tm + ne
    e_flat = idx.reshape(n_assign)
    onehot = (e_flat[:, None] == jnp.arange(ne, dtype=jnp.int32)[None, :]).astype(jnp.int32)
    rank = jnp.take_along_axis(jnp.cumsum(onehot, axis=0) - onehot, e_flat[:, None], axis=1)[:, 0]
    counts = jnp.sum(onehot, axis=0)
    padded = ((counts + tm - 1) // tm) * tm
    ends = jnp.cumsum(padded)
    pos = (ends - padded)[e_flat] + rank
    token = jnp.arange(n_assign, dtype=jnp.int32) // TOP_K
    gate = jnp.take_along_axis(gates, idx, axis=1)
    row_token = jnp.zeros((ntiles * tm,), jnp.int32).at[pos].set(token)
    tile_start = jnp.arange(ntiles, dtype=jnp.int32) * tm
    tile_expert = jnp.minimum(jnp.searchsorted(ends, tile_start, side="right"), ne - 1).astype(jnp.int32)
    tile_valid = tile_start < ends[-1]
    first = tile_valid & jnp.concatenate(
        [jnp.ones((1,), bool), tile_expert[1:] != tile_expert[:-1]])
    slot = (jnp.cumsum(first.astype(jnp.int32)) - 1) % 2
    eid = jnp.arange(ne, dtype=jnp.int32)
    later = (counts > 0)[None, :] & (eid[None, :] > eid[:, None])
    next_used = jnp.min(jnp.where(later, eid[None, :], ne), axis=1)
    next_used = jnp.where(next_used == ne, -1, next_used)
    sched = jnp.stack([tile_expert, tile_valid.astype(jnp.int32), first.astype(jnp.int32),
                       slot.astype(jnp.int32), next_used[tile_expert]]).astype(jnp.int32)
    return row_token, gate, pos.reshape(s, TOP_K).astype(jnp.int32), sched, ntiles


GATHER_ISSUE_UNROLL = 8


def _gather_body(tv_ref, tok_ref, tok_next_ref, x_hbm, o_ref, buf, sem, *, tm, ntiles):
    i = pl.program_id(0)
    slot = i % 2

    def row_copy(tile_tok_ref, slot_, r):
        return pltpu.make_async_copy(x_hbm.at[pl.ds(tile_tok_ref[0, 0, r], 1)],
                                     buf.at[slot_, pl.ds(r, 1)], sem.at[slot_])

    def start_tile(tile_tok_ref, slot_):
        def body(g, carry):
            for u in range(GATHER_ISSUE_UNROLL):
                row_copy(tile_tok_ref, slot_, g * GATHER_ISSUE_UNROLL + u).start(priority=u % 2)
            return carry
        lax.fori_loop(0, tm // GATHER_ISSUE_UNROLL, body, 0)

    @pl.when((i == 0) & (tv_ref[0] > 0))
    def _():
        start_tile(tok_ref, slot)

    @pl.when((i + 1 < ntiles) & (tv_ref[jnp.minimum(i + 1, ntiles - 1)] > 0))
    def _():
        start_tile(tok_next_ref, 1 - slot)

    @pl.when(tv_ref[i] > 0)
    def _():
        def wait_body(r, carry):
            row_copy(tok_ref, slot, r).wait()
            return carry
        lax.fori_loop(0, tm, wait_body, 0, unroll=GATHER_ISSUE_UNROLL)
        o_ref[...] = buf[slot].astype(BF16)

    @pl.when(tv_ref[i] == 0)
    def _():
        o_ref[...] = jnp.zeros_like(o_ref)


def _moe_gather(x, row_token, tile_valid, ntiles, tm):
    s, d = x.shape
    vmem = 2 * _nbytes((tm, d), F32) + 2 * _nbytes((tm, d), BF16) + _nbytes((tm, d), F32)
    tok = row_token.reshape(ntiles, 1, tm)
    grid_spec = pltpu.PrefetchScalarGridSpec(
        num_scalar_prefetch=1,
        grid=(ntiles,),
        in_specs=[pl.BlockSpec((1, 1, tm), lambda i, tv: (i, 0, 0), memory_space=pltpu.SMEM),
                  pl.BlockSpec((1, 1, tm), lambda i, tv: (jnp.minimum(i + 1, ntiles - 1), 0, 0),
                               memory_space=pltpu.SMEM),
                  pl.BlockSpec(memory_space=pl.ANY)],
        out_specs=pl.BlockSpec((tm, d), lambda i, tv: (i, 0)),
        scratch_shapes=[pltpu.VMEM((2, tm, d), F32), pltpu.SemaphoreType.DMA((2,))],
    )
    return pl.pallas_call(
        functools.partial(_gather_body, tm=tm, ntiles=ntiles),
        grid_spec=grid_spec,
        out_shape=jax.ShapeDtypeStruct((ntiles * tm, d), BF16),
        compiler_params=pltpu.CompilerParams(dimension_semantics=("arbitrary",),
                                             vmem_limit_bytes=_vmem_limit(vmem)),
        name="moe_gather",
    )(tile_valid, tok, tok, x)


SCHED_EXPERT, SCHED_VALID, SCHED_FIRST, SCHED_SLOT, SCHED_NEXT = range(5)


def _grouped_body(sched_ref, x_ref, *rest, nw, layer, tcol, epilogue):
    w_hbm, o_ref, wbufs, sem = rest[:nw], rest[nw], rest[nw + 1:2 * nw + 1], rest[2 * nw + 1]
    j, i = pl.program_id(0), pl.program_id(1)
    expert, slot, nxt = sched_ref[SCHED_EXPERT, i], sched_ref[SCHED_SLOT, i], sched_ref[SCHED_NEXT, i]
    first = sched_ref[SCHED_FIRST, i] > 0
    col0 = pl.multiple_of(j * tcol, tcol)

    def copies(e, s):
        return [pltpu.make_async_copy(w.at[layer, e, :, pl.ds(col0, tcol)], wb.at[s], sem.at[s])
                for w, wb in zip(w_hbm, wbufs)]

    @pl.when(i == 0)
    def _():
        for c in copies(expert, slot):
            c.start()

    @pl.when(first & (nxt >= 0))
    def _():
        for c in copies(nxt, 1 - slot):
            c.start()

    @pl.when(first)
    def _():
        for c in copies(expert, slot):
            c.wait()

    @pl.when(sched_ref[SCHED_VALID, i] > 0)
    def _():
        wbs = [wb[slot].astype(BF16) for wb in wbufs]
        sub = min(x_ref.shape[0], MM_SUB_ROWS)
        for r0 in range(0, x_ref.shape[0], sub):
            xv = x_ref[r0:r0 + sub, :]
            o_ref[r0:r0 + sub, :] = epilogue(
                [jnp.dot(xv, wb, preferred_element_type=F32) for wb in wbs])

    @pl.when(sched_ref[SCHED_VALID, i] == 0)
    def _():
        o_ref[...] = jnp.zeros_like(o_ref)


def _grouped_matmul(x, ws, layer, sched, tm, tcol, out_dtype, epilogue, name):
    rows, kdim = x.shape
    n = ws[0].shape[3]
    tcol = min(tcol, n)
    ntiles = rows // tm
    vmem = 2 * _nbytes((tm, kdim), BF16) + 2 * _nbytes((tm, tcol), out_dtype) \
        + len(ws) * (2 * _nbytes((kdim, tcol), F32) + _nbytes((kdim, tcol), BF16)) \
        + (1 + len(ws)) * _nbytes((tm, tcol), F32)
    grid_spec = pltpu.PrefetchScalarGridSpec(
        num_scalar_prefetch=1,
        grid=(n // tcol, ntiles),
        in_specs=[pl.BlockSpec((tm, kdim), lambda j, i, sched: (i, 0))]
        + [pl.BlockSpec(memory_space=pl.ANY)] * len(ws),
        out_specs=pl.BlockSpec((tm, tcol), lambda j, i, sched: (i, j)),
        scratch_shapes=[pltpu.VMEM((2, kdim, tcol), F32) for _ in ws]
        + [pltpu.SemaphoreType.DMA((2,))],
    )
    return pl.pallas_call(
        functools.partial(_grouped_body, nw=len(ws), layer=layer, tcol=tcol, epilogue=epilogue),
        grid_spec=grid_spec,
        out_shape=jax.ShapeDtypeStruct((rows, n), out_dtype),
        compiler_params=pltpu.CompilerParams(dimension_semantics=("arbitrary", "arbitrary"),
                                             vmem_limit_bytes=_vmem_limit(vmem)),
        name=name,
    )(sched, x, *ws)


def _moe_up(xg, w1, w3, layer, sched, tm, *, tf=512):
    return _grouped_matmul(xg, [w1, w3], layer, sched, tm, tf, BF16,
                           lambda p: (_silu(p[0]) * p[1]).astype(BF16), "moe_up")


def _moe_down(h, w2, layer, sched, tm, *, tn=2048):
    return _grouped_matmul(h, [w2], layer, sched, tm, tn, F32, lambda p: p[0], "moe_down")


def _combine_body(pos_ref, pos_next_ref, y_hbm, x_ref, gate_ref, g_ref, b_ref, o32_ref, o16_ref,
                  buf, sem, *, tr, nt, alpha):
    i = pl.program_id(0)
    slot = i % 2

    def row_copy(tile_pos_ref, slot_, r, kk):
        return pltpu.make_async_copy(y_hbm.at[pl.ds(tile_pos_ref[0, kk, r], 1)],
                                     buf.at[slot_, kk, pl.ds(r, 1)], sem.at[slot_])

    def start_tile(tile_pos_ref, slot_):
        def body(r, carry):
            for kk in range(TOP_K):
                row_copy(tile_pos_ref, slot_, r, kk).start(priority=kk % 2)
            return carry
        lax.fori_loop(0, tr, body, 0, unroll=GATHER_ISSUE_UNROLL)

    @pl.when(i == 0)
    def _():
        start_tile(pos_ref, slot)

    @pl.when(i + 1 < nt)
    def _():
        start_tile(pos_next_ref, 1 - slot)

    def wait_body(r, carry):
        for kk in range(TOP_K):
            row_copy(pos_ref, slot, r, kk).wait()
        return carry
    lax.fori_loop(0, tr, wait_body, 0, unroll=GATHER_ISSUE_UNROLL)

    gate = gate_ref[...]
    f = buf[slot, 0] * gate[:, 0:1]
    for kk in range(1, TOP_K):
        f = f + buf[slot, kk] * gate[:, kk:kk + 1]
    out = _ln_rows(alpha * x_ref[...] + f, g_ref[...], b_ref[...])
    o32_ref[...] = out
    o16_ref[...] = out.astype(BF16)


def _moe_combine_ln(y, pos, gate, x, g, b, alpha, *, tr=256):
    s, d = x.shape
    tr = min(tr, s)
    nt = s // tr
    pos_t = pos.reshape(nt, tr, TOP_K).transpose(0, 2, 1)
    row = pl.BlockSpec((tr, d), lambda i: (i, 0))
    vec = pl.BlockSpec((1, d), lambda i: (0, 0))
    vmem = 2 * TOP_K * _nbytes((tr, d), F32) \
        + 2 * (2 * _nbytes((tr, d), F32) + _nbytes((tr, d), BF16)) + 3 * _nbytes((tr, d), F32)
    pos_blk = (1, TOP_K, tr)
    return pl.pallas_call(
        functools.partial(_combine_body, tr=tr, nt=nt, alpha=alpha),
        grid=(nt,),
        in_specs=[pl.BlockSpec(pos_blk, lambda i: (i, 0, 0), memory_space=pltpu.SMEM),
                  pl.BlockSpec(pos_blk, lambda i: (jnp.minimum(i + 1, nt - 1), 0, 0),
                               memory_space=pltpu.SMEM),
                  pl.BlockSpec(memory_space=pl.ANY), row,
                  pl.BlockSpec((tr, TOP_K), lambda i: (i, 0)), vec, vec],
        out_specs=[row, row],
        out_shape=[jax.ShapeDtypeStruct((s, d), F32), jax.ShapeDtypeStruct((s, d), BF16)],
        scratch_shapes=[pltpu.VMEM((2, TOP_K, tr, d), F32), pltpu.SemaphoreType.DMA((2,))],
        compiler_params=pltpu.CompilerParams(dimension_semantics=("arbitrary",),
                                             vmem_limit_bytes=_vmem_limit(vmem)),
        name="moe_combine_ln",
    )(pos_t, pos_t, y, x, gate, g.reshape(1, d), b.reshape(1, d))


def _ple(xb, x, p, w_gate, w_proj, layer, bi, *, tm=MM_ROWS, tn=MM_COLS):
    s, d = x.shape
    pdim = p.shape[-1]
    blk, imap = _tile(min(tm, s), min(tn, d))

    def ep(accs, extras, outs, rows):
        proj = jnp.dot(extras[1][rows, :].astype(BF16), extras[2][...].astype(BF16),
                       preferred_element_type=F32)
        out = extras[0][rows, :] + jax.nn.sigmoid(accs[0]) * proj
        outs[0][rows, :] = out
        outs[1][rows, :] = out.astype(BF16)

    extras = [(x, blk, imap),
              (p, (None, None, blk[0], pdim), lambda i, j: (layer, bi, i, 0)),
              (w_proj, (None, pdim, blk[1]), lambda i, j: (layer, 0, j))]
    outs = [((s, d), F32, blk, imap), ((s, d), BF16, blk, imap)]
    return _fused_matmul(xb, [(w_gate, layer, 0)], d, extras, outs, ep, tm=tm, tn=tn, name="ple")


MOE_TILE_ROWS = 512


def kernel(x, p, ln1_g, ln1_b, ln2_g, ln2_b, a_w_in, a_lb_logits, a_out_g, a_w_o, kv_w, b_w_q, b_sinks, b_w_o, rel_bias, ffn_w1, ffn_w3, ffn_w2, moe_router, moe_w1, moe_w3, moe_w2, ple_w_proj, ple_w_gate):
    batch, s, d = x.shape
    depth = p.shape[0]
    n_a = a_w_in.shape[0]
    kv_dim = kv_w.shape[1] // 2
    n_kv = kv_dim // B_HEAD_DIM
    alpha = (2.0 * depth) ** 0.25
    moe_tm = min(MOE_TILE_ROWS, s)
    bias_band = _rel_bias_band(rel_bias.astype(F32))
    lb_logits = a_lb_logits.astype(F32)

    outs = []
    for bi in range(batch):
        xf = x[bi].astype(F32)
        xb = xf.astype(BF16)
        kvh = None
        for i in range(depth):
            if i < n_a:
                q, k, v, log_f, g, chunk_decay = _hgrn_in_proj(xb, a_w_in, lb_logits, i)
                hb = _hgrn_core(q, k, v, log_f, g, a_out_g[i].astype(F32), chunk_decay)
                y = _residual_matmul(hb, a_w_o, i, xf, alpha, name="mixer_out")
            else:
                j = i - n_a
                qh = _head_split_matmul(xb, b_w_q, j, d, name="swa_q")
                hb = _swa_core(qh, kvh, b_sinks[j].astype(F32), bias_band, n_kv)
                y = _residual_matmul(hb, b_w_o, j, xf, alpha, name="mixer_out")
            li = i // 2
            if i % 2 == 0:
                xf, xb = _layer_norm(y, ln1_g[i].astype(F32), ln1_b[i].astype(F32))
                hmid = _swiglu_up(xb, ffn_w1, ffn_w3, li)
                y = _residual_matmul(hmid, ffn_w2, li, xf, alpha, name="ffn_down")
                xf, xb = _layer_norm(y, ln2_g[i].astype(F32), ln2_b[i].astype(F32))
            else:
                xf, xb, gates, idx = _layer_norm(y, ln1_g[i].astype(F32), ln1_b[i].astype(F32),
                                                 moe_router[li].astype(F32))
                row_token, gate, pos, sched, ntiles = _routing_tables(gates, idx, moe_tm)
                xg = _moe_gather(xf, row_token, sched[SCHED_VALID], ntiles, moe_tm)
                hmid = _moe_up(xg, moe_w1, moe_w3, li, sched, moe_tm)
                yg = _moe_down(hmid, moe_w2, li, sched, moe_tm)
                xf, xb = _moe_combine_ln(yg, pos, gate, xf, ln2_g[i].astype(F32),
                                         ln2_b[i].astype(F32), alpha)
            xf, xb = _ple(xb, xf, p, ple_w_gate, ple_w_proj, i, bi)
            if i == n_a - 1:
                kvh = _head_split_matmul(xb, kv_w, None, 2 * kv_dim, name="kv_proj")
        outs.append(xf)
    return jnp.stack(outs, axis=0).astype(x.dtype)
```

```python
import functools

import numpy as np
import jax
import jax.numpy as jnp
from jax import lax
from jax.experimental import pallas as pl
from jax.experimental.pallas import tpu as pltpu

F32 = jnp.float32
BF16 = jnp.bfloat16

V7X_LANES = 128
V7X_SCOPED_VMEM_BYTES = 60000 * 1024
COMPILER_SCRATCH_BYTES = 6 * 1024 * 1024

MM_ROWS = 2048
MM_COLS = 256
MM_COLS_BF16_OUT = 512
MM_SUB_ROWS = 128

A_HEAD_DIM = 128
A_CHUNK = 64
A_SUB = 16
A_SAFE_LOG2_DECAY = 100.0
B_HEAD_DIM = 64
B_LOGIT_SCALE = B_HEAD_DIM ** -0.5
assert B_LOGIT_SCALE == 2.0 ** -3
B_BLOCK = 128
WINDOW = 128
REL_BUCKETS = 32
REL_MAX_DISTANCE = 128
TOP_K = 2
LN_EPS = 1e-5
RMS_EPS = 1e-6
LOG2E = 1.4426950408889634


def _vmem_limit(block_bytes):
    return int(min(block_bytes + COMPILER_SCRATCH_BYTES, V7X_SCOPED_VMEM_BYTES))


def _nbytes(shape, dtype):
    return int(np.prod([n for n in shape if n is not None])) * jnp.dtype(dtype).itemsize


def _silu(x):
    return x * jax.nn.sigmoid(x)


def _mm_body(*refs, nw, ne, no, epilogue):
    x_ref = refs[0]
    w_refs = refs[1:1 + nw]
    extra_refs = refs[1 + nw:1 + nw + ne]
    out_refs = refs[1 + nw + ne:1 + nw + ne + no]
    wbs = [w_ref[...].astype(BF16) for w_ref in w_refs]
    tm = x_ref.shape[0]
    sub = min(tm, MM_SUB_ROWS)
    for r0 in range(0, tm, sub):
        rows = slice(r0, r0 + sub)
        xv = x_ref[rows, :]
        prods = [jnp.dot(xv, wb, preferred_element_type=F32) for wb in wbs]
        epilogue(prods, extra_refs, out_refs, rows)


def _fused_matmul(x, ws, n_cols, extras, outs, epilogue, *, tm, tn, name, tk=None, k_block=0,
                  x_buffers=2):
    m, kdim = x.shape
    tk = tk or kdim
    tm, tn = min(tm, m), min(tn, n_cols)
    assert m % tm == 0 and n_cols % tn == 0 and kdim % tk == 0
    x_mode = {} if x_buffers == 2 else {"pipeline_mode": pl.Buffered(x_buffers)}
    in_specs = [pl.BlockSpec((tm, tk), lambda i, j: (i, k_block), **x_mode)]
    operands = [x]
    vmem = x_buffers * _nbytes((tm, tk), x.dtype)
    for w, layer, off in ws:
        if layer is None:
            spec = pl.BlockSpec((tk, tn), lambda i, j, off=off: (k_block, j + off))
        else:
            spec = pl.BlockSpec((None, tk, tn), lambda i, j, off=off, layer=layer:
                                (layer, k_block, j + off))
        in_specs.append(spec)
        operands.append(w)
        vmem += 2 * _nbytes((tk, tn), w.dtype) + _nbytes((tk, tn), BF16)
    for arr, bshape, imap in extras:
        in_specs.append(pl.BlockSpec(bshape, imap))
        operands.append(arr)
        vmem += 2 * _nbytes(bshape, arr.dtype)
    out_shapes, out_specs = [], []
    for shape, dtype, bshape, imap in outs:
        out_shapes.append(jax.ShapeDtypeStruct(shape, dtype))
        out_specs.append(pl.BlockSpec(bshape, imap))
        vmem += 2 * _nbytes(bshape, dtype)
    vmem += (2 + len(ws)) * _nbytes((tm, tn), F32)
    body = functools.partial(_mm_body, nw=len(ws), ne=len(extras), no=len(outs), epilogue=epilogue)
    return pl.pallas_call(
        body,
        grid=(m // tm, n_cols // tn),
        in_specs=in_specs,
        out_specs=out_specs,
        out_shape=out_shapes,
        compiler_params=pltpu.CompilerParams(dimension_semantics=("parallel", "parallel"),
                                             vmem_limit_bytes=_vmem_limit(vmem)),
        name=name,
    )(*operands)


def _tile(tm, tn):
    return (tm, tn), (lambda i, j: (i, j))


def _ln_rows(y, g, b):
    mu = jnp.mean(y, axis=-1, keepdims=True)
    yc = y - mu
    var = jnp.mean(yc * yc, axis=-1, keepdims=True)
    return yc * lax.rsqrt(var + LN_EPS) * g + b


def _route_top2(x, w_router):
    logits = jnp.dot(x, w_router, preferred_element_type=F32, precision=lax.Precision.HIGHEST)
    ne = logits.shape[1]
    eid = lax.broadcasted_iota(jnp.int32, logits.shape, 1)
    m1 = jnp.max(logits, axis=-1, keepdims=True)
    i1 = jnp.min(jnp.where(logits == m1, eid, ne), axis=-1, keepdims=True)
    rest = jnp.where(eid == i1, -jnp.inf, logits)
    m2 = jnp.max(rest, axis=-1, keepdims=True)
    i2 = jnp.min(jnp.where(rest == m2, eid, ne), axis=-1, keepdims=True)
    e2 = jnp.exp(m2 - m1)
    w_top1 = 1.0 / (1.0 + e2)
    w_top2 = e2 / (1.0 + e2)
    gates = jnp.where(eid == i1, w_top1, 0.0) + jnp.where(eid == i2, w_top2, 0.0)
    return gates, jnp.where(eid == 0, i1, i2)[:, :TOP_K]


def _ln_body(*refs, route):
    if route:
        y_ref, g_ref, b_ref, wr_ref, o32_ref, o16_ref, gates_ref, idx_ref = refs
    else:
        y_ref, g_ref, b_ref, o32_ref, o16_ref = refs
    out = _ln_rows(y_ref[...], g_ref[...], b_ref[...])
    o32_ref[...] = out
    o16_ref[...] = out.astype(BF16)
    if route:
        gates_ref[...], idx_ref[...] = _route_top2(out, wr_ref[...])


def _layer_norm(y, g, b, w_router=None, *, tr=256):
    s, d = y.shape
    tr = min(tr, s)
    assert s % tr == 0
    row = pl.BlockSpec((tr, d), lambda i: (i, 0))
    vec = pl.BlockSpec((1, d), lambda i: (0, 0))
    in_specs, operands = [row, vec, vec], [y, g.reshape(1, d), b.reshape(1, d)]
    out_specs = [row, row]
    out_shape = [jax.ShapeDtypeStruct((s, d), F32), jax.ShapeDtypeStruct((s, d), BF16)]
    vmem = 2 * (2 * _nbytes((tr, d), F32) + _nbytes((tr, d), BF16)) + 2 * _nbytes((tr, d), F32)
    if w_router is not None:
        ne = w_router.shape[1]
        in_specs.append(pl.BlockSpec((d, ne), lambda i: (0, 0)))
        operands.append(w_router)
        out_specs += [pl.BlockSpec((tr, ne), lambda i: (i, 0)),
                      pl.BlockSpec((tr, TOP_K), lambda i: (i, 0))]
        out_shape += [jax.ShapeDtypeStruct((s, ne), F32), jax.ShapeDtypeStruct((s, TOP_K), jnp.int32)]
        vmem += 2 * _nbytes((d, V7X_LANES), F32) + 4 * _nbytes((tr, d), F32)
    return pl.pallas_call(
        functools.partial(_ln_body, route=w_router is not None),
        grid=(s // tr,),
        in_specs=in_specs,
        out_specs=out_specs,
        out_shape=out_shape,
        compiler_params=pltpu.CompilerParams(dimension_semantics=("parallel",),
                                             vmem_limit_bytes=_vmem_limit(vmem)),
        name="layer_norm_route" if w_router is not None else "layer_norm",
    )(*operands)


def _lower_bound(lbl, layer):
    mx = jnp.max(lbl, axis=0, keepdims=True)
    e = jnp.exp(lbl - mx)
    return jnp.sum(e[:layer + 1], axis=0, keepdims=True) / jnp.sum(e, axis=0, keepdims=True)


def _hgrn_in_proj(xb, w_in, lb_logits, layer, *, tm=MM_ROWS, tn=MM_COLS):
    s, d = xb.shape
    blk, imap = _tile(min(tm, s), min(tn, d))

    def ep_silu(accs, extras, outs, rows):
        outs[0][rows, :] = _silu(accs[0]).astype(BF16)

    def ep_id(accs, extras, outs, rows):
        outs[0][rows, :] = accs[0].astype(BF16)

    def ep_forget(accs, extras, outs, rows):
        lb = _lower_bound(extras[0][...], layer)
        f_raw = accs[0]
        e = jnp.exp(-jnp.abs(f_raw))
        log_sig = jnp.minimum(f_raw, 0.0) - jnp.log(1.0 + e)
        sig_neg = jnp.where(f_raw >= 0.0, e, 1.0) / (1.0 + e)
        a = jnp.log(lb)
        b = jnp.log1p(-lb) + log_sig
        log_f = jnp.maximum(a, b) + jnp.log(1.0 + jnp.exp(-jnp.abs(a - b)))
        outs[0][rows, :] = log_f
        outs[1][rows, :] = ((1.0 - lb) * sig_neg).astype(BF16)
        nrow, ncol = log_f.shape
        outs[2][rows.start // A_CHUNK:rows.stop // A_CHUNK, :] = jnp.sum(
            log_f.reshape(nrow // A_CHUNK, A_CHUNK, ncol), axis=1)

    def call(seg, epilogue, extras, out_dtypes, name, more_outs=(), cols=tn):
        cblk, cimap = _tile(blk[0], min(cols, d))
        outs = [((s, d), dt, cblk, cimap) for dt in out_dtypes] + list(more_outs)
        return _fused_matmul(xb, [(w_in, layer, seg * (d // cblk[1]))], d, extras, outs, epilogue,
                             tm=tm, tn=cols, name=name)

    nslot = lb_logits.shape[0]
    lb_extra = [(lb_logits, (nslot, blk[1]), lambda i, j: (0, j))]
    decay_out = ((s // A_CHUNK, d), F32, (blk[0] // A_CHUNK, blk[1]), imap)
    (q,) = call(0, ep_silu, [], [BF16], "hgrn_q", cols=MM_COLS_BF16_OUT)
    log_f, k, chunk_decay = call(1, ep_forget, lb_extra, [F32, BF16], "hgrn_f",
                                 more_outs=[decay_out])
    (v,) = call(2, ep_id, [], [BF16], "hgrn_v", cols=MM_COLS_BF16_OUT)
    (g,) = call(3, ep_silu, [], [BF16], "hgrn_g", cols=MM_COLS_BF16_OUT)
    return q, k, v, log_f, g, chunk_decay


def _chunk_cumsum(lf_ref, tc):
    x = lf_ref[...]
    row_in_chunk = lax.broadcasted_iota(jnp.int32, x.shape, 0) % A_CHUNK
    shift = 1
    while shift < A_CHUNK:
        x = x + jnp.where(row_in_chunk >= shift, pltpu.roll(x, shift, axis=0), 0.0)
        shift *= 2
    return x


def _hgrn_body(bounded_ref, q_ref, k_ref, v_ref, lf_ref, g_ref, og_ref, o_ref, st_ref, *, tc):
    @pl.when(pl.program_id(1) == 0)
    def _():
        st_ref[...] = jnp.zeros_like(st_ref)

    out_g = og_ref[...]
    half = A_SUB // 2
    lane = lax.broadcasted_iota(jnp.int32, (half, A_HEAD_DIM), 1)
    hrow = lax.broadcasted_iota(jnp.int32, (half, A_HEAD_DIM), 0)
    nsub = A_CHUNK // A_SUB
    nchunk = tc // A_CHUNK

    def finish(rows, o):
        o = o * lax.rsqrt(jnp.mean(o * o, axis=-1, keepdims=True) + RMS_EPS)
        o_ref[rows, :] = (o * out_g * g_ref[rows, :].astype(F32)).astype(BF16)

    def as_column(row):
        return jnp.transpose(jnp.broadcast_to(row, (8, row.shape[1])))[:, :1]

    def inter_chunk(ci, cum, st_t):
        rows = slice(ci * A_CHUNK, (ci + 1) * A_CHUNK)
        c2 = cum[rows] * LOG2E
        qc = q_ref[rows, :].astype(F32)
        kc = k_ref[rows, :].astype(F32)
        vc = v_ref[rows, :]
        last = c2[A_CHUNK - 1:A_CHUNK]
        qe = (qc * jnp.exp2(c2)).astype(BF16)
        o_inter = lax.dot_general(qe, st_t.astype(BF16), (((1,), (1,)), ((), ())),
                                  preferred_element_type=F32)
        kd = (kc * jnp.exp2(last - c2)).astype(BF16)
        upd = lax.dot_general(vc, kd, (((0,), (0,)), ((), ())), preferred_element_type=F32)
        return rows, c2, qc, kc, vc, qe, o_inter, st_t * jnp.exp2(last) + upd

    def run_factorised():
        c2 = _chunk_cumsum(lf_ref, tc) * LOG2E
        qe = (q_ref[...].astype(F32) * jnp.exp2(c2)).astype(BF16)
        kb32 = k_ref[...].astype(F32) * jnp.exp2(-c2)
        kb = kb32.astype(BF16)
        t_i = lax.broadcasted_iota(jnp.int32, (A_CHUNK, A_CHUNK), 0)
        s_i = lax.broadcasted_iota(jnp.int32, (A_CHUNK, A_CHUNK), 1)
        causal = t_i >= s_i
        st = st_ref[...]
        outs = []
        for ci in range(nchunk):
            rows = slice(ci * A_CHUNK, (ci + 1) * A_CHUNK)
            vc = v_ref[rows, :]
            grow = jnp.exp2(c2[(ci + 1) * A_CHUNK - 1:(ci + 1) * A_CHUNK])
            sc = lax.dot_general(qe[rows], kb[rows], (((1,), (1,)), ((), ())),
                                 preferred_element_type=F32)
            scores = jnp.where(causal, sc, 0.0).astype(BF16)
            outs.append(jnp.dot(jnp.concatenate([qe[rows], scores], axis=1),
                                jnp.concatenate([st.astype(BF16), vc], axis=0),
                                preferred_element_type=F32))
            kd = (kb32[rows] * grow).astype(BF16)
            upd = lax.dot_general(kd, vc, (((0,), (0,)), ((), ())), preferred_element_type=F32)
            st = st * as_column(grow) + upd
        st_ref[...] = st
        finish(slice(0, tc), jnp.concatenate(outs, axis=0))

    def direct_chunk(ci, cum, st):
        rows, c2, qc, kc, vc, qe, o_inter, st = inter_chunk(ci, cum, st)

        p_rows = []
        k_blocks, prev_ref = [], None
        for si in range(nsub):
            lo_r = si * A_SUB
            cs, qs, ks = c2[lo_r:lo_r + A_SUB], qc[lo_r:lo_r + A_SUB], kc[lo_r:lo_r + A_SUB]
            off = None
            if si > 0:
                ref = c2[lo_r - 1:lo_r]
                if prev_ref is not None:
                    rebase = jnp.exp2(ref - prev_ref)
                    k_blocks = [kb * rebase for kb in k_blocks]
                k_blocks.append(kc[lo_r - A_SUB:lo_r] * jnp.exp2(ref - c2[lo_r - A_SUB:lo_r]))
                prev_ref = ref
                k_t = jnp.concatenate(
                    k_blocks + [jnp.zeros((A_CHUNK - lo_r, A_HEAD_DIM), F32)], axis=0)
                q_t = (qs * jnp.exp2(cs - ref)).astype(BF16)
                off = lax.dot_general(q_t, k_t.astype(BF16), (((1,), (1,)), ((), ())),
                                      preferred_element_type=F32)
            c_h = (cs[:half], cs[half:])
            q_h = (qs[:half], qs[half:])
            p_h = [jnp.zeros((half, A_HEAD_DIM), F32), jnp.zeros((half, A_HEAD_DIM), F32)]
            for sj in range(A_SUB):
                for hh in range(sj // half, 2):
                    decay = jnp.exp2(c_h[hh] - cs[sj:sj + 1])
                    colv = jnp.sum(q_h[hh] * (ks[sj:sj + 1] * decay), axis=1, keepdims=True)
                    sel = lane == lo_r + sj
                    if sj // half == hh:
                        sel = sel & (hrow >= sj - hh * half)
                    p_h[hh] = jnp.where(sel, colv, p_h[hh])
            for hh in range(2):
                ph = p_h[hh][:, :A_CHUNK]
                p_rows.append(ph if off is None else ph + off[hh * half:(hh + 1) * half])
        scores = jnp.concatenate(p_rows, axis=0).astype(BF16)
        finish(rows, o_inter + jnp.dot(scores, vc, preferred_element_type=F32))
        return st

    def run_direct():
        cum = _chunk_cumsum(lf_ref, tc)
        st_t = st_ref[...].T
        for ci in range(nchunk):
            st_t = direct_chunk(ci, cum, st_t)
        st_ref[...] = st_t.T

    bounded = bounded_ref[pl.program_id(0), pl.program_id(1)] > 0
    pl.when(bounded)(run_factorised)
    pl.when(jnp.logical_not(bounded))(run_direct)


def _hgrn_core(q, k, v, log_f, g, out_g, chunk_decay, *, tc=1024):
    s, d = q.shape
    tc = min(tc, s)
    assert s % tc == 0 and tc % A_CHUNK == 0 and d % A_HEAD_DIM == 0
    nhead, nblk = d // A_HEAD_DIM, s // tc
    worst = jnp.min(chunk_decay.reshape(nblk, tc // A_CHUNK, nhead, A_HEAD_DIM), axis=(1, 3))
    bounded = (worst.T * LOG2E >= -A_SAFE_LOG2_DECAY).astype(jnp.int32)
    blk = pl.BlockSpec((tc, A_HEAD_DIM), lambda h, c, flags: (c, h))
    vmem = 2 * (4 * _nbytes((tc, A_HEAD_DIM), BF16) + 2 * _nbytes((tc, A_HEAD_DIM), F32)) \
        + 12 * _nbytes((tc, A_HEAD_DIM), F32)
    grid_spec = pltpu.PrefetchScalarGridSpec(
        num_scalar_prefetch=1,
        grid=(nhead, nblk),
        in_specs=[blk, blk, blk, blk, blk,
                  pl.BlockSpec((1, A_HEAD_DIM), lambda h, c, flags: (0, h))],
        out_specs=blk,
        scratch_shapes=[pltpu.VMEM((A_HEAD_DIM, A_HEAD_DIM), F32)],
    )
    return pl.pallas_call(
        functools.partial(_hgrn_body, tc=tc),
        grid_spec=grid_spec,
        out_shape=jax.ShapeDtypeStruct((s, d), BF16),
        compiler_params=pltpu.CompilerParams(dimension_semantics=("parallel", "arbitrary"),
                                             vmem_limit_bytes=_vmem_limit(vmem)),
        name="hgrn_core",
    )(bounded, q, k, v, log_f, g, out_g.reshape(1, d))


def _residual_matmul(hb, w, layer, resid, alpha, *, name, tm=MM_ROWS, tn=MM_COLS):
    s, d = resid.shape
    kdim = hb.shape[1]
    tk = min(kdim, d)
    blk, imap = _tile(min(tm, s), min(tn, d))
    y = resid
    for kb in range(kdim // tk):
        scale = alpha if kb == 0 else 1.0

        def ep(accs, extras, outs, rows, scale=scale):
            outs[0][rows, :] = scale * extras[0][rows, :] + accs[0]

        (y,) = _fused_matmul(hb, [(w, layer, 0)], d, [(y, blk, imap)], [((s, d), F32, blk, imap)],
                             ep, tm=tm, tn=tn, tk=tk, k_block=kb, name=name)
    return y


def _rel_bucket_band():
    i = np.arange(B_BLOCK)[:, None]
    j = np.arange(2 * B_BLOCK)[None, :]
    d = np.clip(B_BLOCK + i - j, 0, None)
    max_exact = REL_BUCKETS // 2
    large = max_exact + (np.log(np.maximum(d, 1) / max_exact)
                         / np.log(REL_MAX_DISTANCE / max_exact)
                         * (REL_BUCKETS - max_exact)).astype(np.int32)
    large = np.minimum(large, REL_BUCKETS - 1)
    return np.where(d < max_exact, d, large).astype(np.int32)


def _bias_body(rb_ref, bucket_ref, o_ref, *, heads_per_step):
    h0 = pl.program_id(0) * heads_per_step
    bucket = bucket_ref[...]
    qi = lax.broadcasted_iota(jnp.int32, bucket.shape, 0)
    kj = lax.broadcasted_iota(jnp.int32, bucket.shape, 1)
    dist = B_BLOCK + qi - kj
    in_window = (dist >= 0) & (dist < WINDOW)
    for hh in range(heads_per_step):
        acc = jnp.zeros(bucket.shape, F32)
        for b in range(REL_BUCKETS):
            acc = jnp.where(bucket == b, rb_ref[b, h0 + hh], acc)
        o_ref[hh] = jnp.where(in_window, acc, -jnp.inf)


def _rel_bias_band(rel_bias, *, heads_per_step=8):
    nh = rel_bias.shape[1]
    bucket = jnp.asarray(_rel_bucket_band())
    return pl.pallas_call(
        functools.partial(_bias_body, heads_per_step=heads_per_step),
        grid=(nh // heads_per_step,),
        in_specs=[pl.BlockSpec(memory_space=pltpu.SMEM),
                  pl.BlockSpec((B_BLOCK, 2 * B_BLOCK), lambda i: (0, 0))],
        out_specs=pl.BlockSpec((heads_per_step, B_BLOCK, 2 * B_BLOCK), lambda i: (i, 0, 0)),
        out_shape=jax.ShapeDtypeStruct((nh, B_BLOCK, 2 * B_BLOCK), F32),
        compiler_params=pltpu.CompilerParams(dimension_semantics=("parallel",)),
        name="rel_bias_band",
    )(rel_bias, bucket)


def _head_split_matmul(xb, w, layer, n_cols, *, name, tm=MM_ROWS, tn=MM_COLS):
    s, kdim = xb.shape
    tm, tn = min(tm, s), min(tn, n_cols)
    hpt = tn // B_HEAD_DIM

    def ep(accs, extras, outs, rows):
        for c in range(hpt):
            outs[0][c, rows, :] = accs[0][:, c * B_HEAD_DIM:(c + 1) * B_HEAD_DIM].astype(BF16)

    outs = [((n_cols // B_HEAD_DIM, s, B_HEAD_DIM), BF16, (hpt, tm, B_HEAD_DIM),
             lambda i, j: (j, i, 0))]
    (out,) = _fused_matmul(xb, [(w, layer, 0)], n_cols, [], outs, ep, tm=tm, tn=tn, name=name)
    return out


def _attn_body(sink_ref, q_ref, kp_ref, kc_ref, vp_ref, vc_ref, bias_ref, o_ref, *, group, qb):
    h = pl.program_id(0)
    m_blk = pl.program_id(1)
    c = B_BLOCK
    k_all = jnp.concatenate([kp_ref[0], kc_ref[0]], axis=0)
    v_all = jnp.concatenate([vp_ref[0], vc_ref[0]], axis=0)
    kj = lax.broadcasted_iota(jnp.int32, (c, 2 * c), 1)
    for b in range(qb):
        q = q_ref[:, b * c:(b + 1) * c, :].reshape(group * c, B_HEAD_DIM) * B_LOGIT_SCALE
        kb = k_all[b * c:(b + 2) * c]
        vb = v_all[b * c:(b + 2) * c]
        logits = lax.dot_general(q, kb, (((1,), (1,)), ((), ())), preferred_element_type=F32)
        probs = []
        for gi in range(group):
            lg = logits[gi * c:(gi + 1) * c] + bias_ref[gi]
            if b == 0:
                lg = jnp.where((m_blk > 0) | (kj >= c), lg, -jnp.inf)
            sink = sink_ref[h * group + gi]
            m = jnp.maximum(jnp.max(lg, axis=-1, keepdims=True), sink)
            e = jnp.exp(lg - m)
            denom = jnp.sum(e, axis=-1, keepdims=True) + jnp.exp(sink - m)
            probs.append((e / denom).astype(BF16))
        pv = jnp.dot(jnp.concatenate(probs, axis=0), vb, preferred_element_type=F32)
        o_ref[b * c:(b + 1) * c, :] = jnp.concatenate(
            [pv[gi * c:(gi + 1) * c] for gi in range(group)], axis=1).astype(BF16)


def _swa_core(qh, kvh, sinks, bias, n_kv, *, qb=4):
    nq, s, hd = qh.shape
    group = nq // n_kv
    nb = s // B_BLOCK
    qb = min(qb, nb)
    assert nb % qb == 0
    prev = lambda m: jnp.maximum(m * qb - 1, 0)
    prev_blk = (1, B_BLOCK, hd)
    cur_blk = (1, qb * B_BLOCK, hd)
    in_specs = [
        pl.BlockSpec(memory_space=pltpu.SMEM),
        pl.BlockSpec((group, qb * B_BLOCK, hd), lambda h, m: (h, m, 0)),
        pl.BlockSpec(prev_blk, lambda h, m: (h, prev(m), 0)),
        pl.BlockSpec(cur_blk, lambda h, m: (h, m, 0)),
        pl.BlockSpec(prev_blk, lambda h, m: (h + n_kv, prev(m), 0)),
        pl.BlockSpec(cur_blk, lambda h, m: (h + n_kv, m, 0)),
        pl.BlockSpec((group, B_BLOCK, 2 * B_BLOCK), lambda h, m: (h, 0, 0)),
    ]
    vmem = 2 * (_nbytes((group, B_BLOCK, 2 * B_BLOCK), F32)
                + (group + 3) * _nbytes((qb * B_BLOCK, V7X_LANES), BF16)
                + _nbytes((qb * B_BLOCK, group * hd), BF16)) \
        + 6 * qb * _nbytes((group * B_BLOCK, 2 * B_BLOCK), F32)
    return pl.pallas_call(
        functools.partial(_attn_body, group=group, qb=qb),
        grid=(n_kv, nb // qb),
        in_specs=in_specs,
        out_specs=pl.BlockSpec((qb * B_BLOCK, group * hd), lambda h, m: (m, h)),
        out_shape=jax.ShapeDtypeStruct((s, nq * hd), BF16),
        compiler_params=pltpu.CompilerParams(dimension_semantics=("parallel", "parallel"),
                                             vmem_limit_bytes=_vmem_limit(vmem)),
        name="swa_core",
    )(sinks, qh, kvh, kvh, kvh, kvh, bias)


def _swiglu_up(xb, w1, w3, layer, *, tm=MM_ROWS, tn=MM_COLS):
    s, d = xb.shape
    f = w1.shape[2]
    blk, imap = _tile(min(tm, s), min(tn, f))

    def ep(accs, extras, outs, rows):
        outs[0][rows, :] = (_silu(accs[0]) * accs[1]).astype(BF16)

    (h,) = _fused_matmul(xb, [(w1, layer, 0), (w3, layer, 0)], f, [],
                         [((s, f), BF16, blk, imap)], ep, tm=tm, tn=tn, name="swiglu_up")
    return h


def _routing_tables(gates, idx, tm):
    s, ne = gates.shape
    n_assign = s * TOP_K
    ntiles = n_assign // tm + ne
    e_flat = idx.reshape(n_assign)
    onehot = (e_flat[:, None] == jnp.arange(ne, dtype=jnp.int32)[None, :]).astype(jnp.int32)
    rank = jnp.take_along_axis(jnp.cumsum(onehot, axis=0) - onehot, e_flat[:, None], axis=1)[:, 0]
    counts = jnp.sum(onehot, axis=0)
    padded = ((counts + tm - 1) // tm) * tm
    ends = jnp.cumsum(padded)
    pos = (ends - padded)[e_flat] + rank
    token = jnp.arange(n_assign, dtype=jnp.int32) // TOP_K
    gate = jnp.take_along_axis(gates, idx, axis=1)
    row_token = jnp.zeros((ntiles * tm,), jnp.int32).at[pos].set(token)
    tile_start = jnp.arange(ntiles, dtype=jnp.int32) * tm
    tile_expert = jnp.minimum(jnp.searchsorted(ends, tile_start, side="right"), ne - 1).astype(jnp.int32)
    tile_valid = tile_start < ends[-1]
    first = tile_valid & jnp.concatenate(
        [jnp.ones((1,), bool), tile_expert[1:] != tile_expert[:-1]])
    slot = (jnp.cumsum(first.astype(jnp.int32)) - 1) % 2
    eid = jnp.arange(ne, dtype=jnp.int32)
    later = (counts > 0)[None, :] & (eid[None, :] > eid[:, None])
    next_used = jnp.min(jnp.where(later, eid[None, :], ne), axis=1)
    next_used = jnp.where(next_used == ne, -1, next_used)
    sched = jnp.stack([tile_expert, tile_valid.astype(jnp.int32), first.astype(jnp.int32),
                       slot.astype(jnp.int32), next_used[tile_expert]]).astype(jnp.int32)
    return row_token, gate, pos.reshape(s, TOP_K).astype(jnp.int32), sched, ntiles


GATHER_ISSUE_UNROLL = 8


def _gather_body(tv_ref, tok_ref, tok_next_ref, x_hbm, o_ref, buf, sem, *, tm, ntiles):
    i = pl.program_id(0)
    slot = i % 2

    def row_copy(tile_tok_ref, slot_, r):
        return pltpu.make_async_copy(x_hbm.at[pl.ds(tile_tok_ref[0, 0, r], 1)],
                                     buf.at[slot_, pl.ds(r, 1)], sem.at[slot_])

    def start_tile(tile_tok_ref, slot_):
        def body(r, carry):
            row_copy(tile_tok_ref, slot_, r).start()
            return carry
        lax.fori_loop(0, tm, body, 0, unroll=GATHER_ISSUE_UNROLL)

    @pl.when((i == 0) & (tv_ref[0] > 0))
    def _():
        start_tile(tok_ref, slot)

    @pl.when((i + 1 < ntiles) & (tv_ref[jnp.minimum(i + 1, ntiles - 1)] > 0))
    def _():
        start_tile(tok_next_ref, 1 - slot)

    @pl.when(tv_ref[i] > 0)
    def _():
        def wait_body(r, carry):
            row_copy(tok_ref, slot, r).wait()
            return carry
        lax.fori_loop(0, tm, wait_body, 0, unroll=GATHER_ISSUE_UNROLL)
        o_ref[...] = buf[slot].astype(BF16)

    @pl.when(tv_ref[i] == 0)
    def _():
        o_ref[...] = jnp.zeros_like(o_ref)


def _moe_gather(x, row_token, tile_valid, ntiles, tm):
    s, d = x.shape
    vmem = 2 * _nbytes((tm, d), F32) + 2 * _nbytes((tm, d), BF16) + _nbytes((tm, d), F32)
    tok = row_token.reshape(ntiles, 1, tm)
    grid_spec = pltpu.PrefetchScalarGridSpec(
        num_scalar_prefetch=1,
        grid=(ntiles,),
        in_specs=[pl.BlockSpec((1, 1, tm), lambda i, tv: (i, 0, 0), memory_space=pltpu.SMEM),
                  pl.BlockSpec((1, 1, tm), lambda i, tv: (jnp.minimum(i + 1, ntiles - 1), 0, 0),
                               memory_space=pltpu.SMEM),
                  pl.BlockSpec(memory_space=pl.ANY)],
        out_specs=pl.BlockSpec((tm, d), lambda i, tv: (i, 0)),
        scratch_shapes=[pltpu.VMEM((2, tm, d), F32), pltpu.SemaphoreType.DMA((2,))],
    )
    return pl.pallas_call(
        functools.partial(_gather_body, tm=tm, ntiles=ntiles),
        grid_spec=grid_spec,
        out_shape=jax.ShapeDtypeStruct((ntiles * tm, d), BF16),
        compiler_params=pltpu.CompilerParams(dimension_semantics=("arbitrary",),
                                             vmem_limit_bytes=_vmem_limit(vmem)),
        name="moe_gather",
    )(tile_valid, tok, tok, x)


SCHED_EXPERT, SCHED_VALID, SCHED_FIRST, SCHED_SLOT, SCHED_NEXT = range(5)


def _grouped_body(sched_ref, x_ref, *rest, nw, layer, tcol, epilogue):
    w_hbm, o_ref, wbufs, sem = rest[:nw], rest[nw], rest[nw + 1:2 * nw + 1], rest[2 * nw + 1]
    j, i = pl.program_id(0), pl.program_id(1)
    expert, slot, nxt = sched_ref[SCHED_EXPERT, i], sched_ref[SCHED_SLOT, i], sched_ref[SCHED_NEXT, i]
    first = sched_ref[SCHED_FIRST, i] > 0
    col0 = pl.multiple_of(j * tcol, tcol)

    def copies(e, s):
        return [pltpu.make_async_copy(w.at[layer, e, :, pl.ds(col0, tcol)], wb.at[s], sem.at[s])
                for w, wb in zip(w_hbm, wbufs)]

    @pl.when(i == 0)
    def _():
        for c in copies(expert, slot):
            c.start()

    @pl.when(first & (nxt >= 0))
    def _():
        for c in copies(nxt, 1 - slot):
            c.start()

    @pl.when(first)
    def _():
        for c in copies(expert, slot):
            c.wait()

    @pl.when(sched_ref[SCHED_VALID, i] > 0)
    def _():
        wbs = [wb[slot].astype(BF16) for wb in wbufs]
        sub = min(x_ref.shape[0], MM_SUB_ROWS)
        for r0 in range(0, x_ref.shape[0], sub):
            xv = x_ref[r0:r0 + sub, :]
            o_ref[r0:r0 + sub, :] = epilogue(
                [jnp.dot(xv, wb, preferred_element_type=F32) for wb in wbs])

    @pl.when(sched_ref[SCHED_VALID, i] == 0)
    def _():
        o_ref[...] = jnp.zeros_like(o_ref)


def _grouped_matmul(x, ws, layer, sched, tm, tcol, out_dtype, epilogue, name):
    rows, kdim = x.shape
    n = ws[0].shape[3]
    tcol = min(tcol, n)
    ntiles = rows // tm
    vmem = 2 * _nbytes((tm, kdim), BF16) + 2 * _nbytes((tm, tcol), out_dtype) \
        + len(ws) * (2 * _nbytes((kdim, tcol), F32) + _nbytes((kdim, tcol), BF16)) \
        + (1 + len(ws)) * _nbytes((tm, tcol), F32)
    grid_spec = pltpu.PrefetchScalarGridSpec(
        num_scalar_prefetch=1,
        grid=(n // tcol, ntiles),
        in_specs=[pl.BlockSpec((tm, kdim), lambda j, i, sched: (i, 0))]
        + [pl.BlockSpec(memory_space=pl.ANY)] * len(ws),
        out_specs=pl.BlockSpec((tm, tcol), lambda j, i, sched: (i, j)),
        scratch_shapes=[pltpu.VMEM((2, kdim, tcol), F32) for _ in ws]
        + [pltpu.SemaphoreType.DMA((2,))],
    )
    return pl.pallas_call(
        functools.partial(_grouped_body, nw=len(ws), layer=layer, tcol=tcol, epilogue=epilogue),
        grid_spec=grid_spec,
        out_shape=jax.ShapeDtypeStruct((rows, n), out_dtype),
        compiler_params=pltpu.CompilerParams(dimension_semantics=("arbitrary", "arbitrary"),
                                             vmem_limit_bytes=_vmem_limit(vmem)),
        name=name,
    )(sched, x, *ws)


def _moe_up(xg, w1, w3, layer, sched, tm, *, tf=512):
    return _grouped_matmul(xg, [w1, w3], layer, sched, tm, tf, BF16,
                           lambda p: (_silu(p[0]) * p[1]).astype(BF16), "moe_up")


def _moe_down(h, w2, layer, sched, tm, *, tn=2048):
    return _grouped_matmul(h, [w2], layer, sched, tm, tn, F32, lambda p: p[0], "moe_down")


def _combine_body(pos_ref, pos_next_ref, y_hbm, x_ref, gate_ref, g_ref, b_ref, o32_ref, o16_ref,
                  buf, sem, *, tr, nt, alpha):
    i = pl.program_id(0)
    slot = i % 2

    def row_copy(tile_pos_ref, slot_, r, kk):
        return pltpu.make_async_copy(y_hbm.at[pl.ds(tile_pos_ref[0, kk, r], 1)],
                                     buf.at[slot_, kk, pl.ds(r, 1)], sem.at[slot_])

    def start_tile(tile_pos_ref, slot_):
        def body(r, carry):
            for kk in range(TOP_K):
                row_copy(tile_pos_ref, slot_, r, kk).start()
            return carry
        lax.fori_loop(0, tr, body, 0, unroll=GATHER_ISSUE_UNROLL)

    @pl.when(i == 0)
    def _():
        start_tile(pos_ref, slot)

    @pl.when(i + 1 < nt)
    def _():
        start_tile(pos_next_ref, 1 - slot)

    def wait_body(r, carry):
        for kk in range(TOP_K):
            row_copy(pos_ref, slot, r, kk).wait()
        return carry
    lax.fori_loop(0, tr, wait_body, 0, unroll=GATHER_ISSUE_UNROLL)

    gate = gate_ref[...]
    f = buf[slot, 0] * gate[:, 0:1]
    for kk in range(1, TOP_K):
        f = f + buf[slot, kk] * gate[:, kk:kk + 1]
    out = _ln_rows(alpha * x_ref[...] + f, g_ref[...], b_ref[...])
    o32_ref[...] = out
    o16_ref[...] = out.astype(BF16)


def _moe_combine_ln(y, pos, gate, x, g, b, alpha, *, tr=256):
    s, d = x.shape
    tr = min(tr, s)
    nt = s // tr
    pos_t = pos.reshape(nt, tr, TOP_K).transpose(0, 2, 1)
    row = pl.BlockSpec((tr, d), lambda i: (i, 0))
    vec = pl.BlockSpec((1, d), lambda i: (0, 0))
    vmem = 2 * TOP_K * _nbytes((tr, d), F32) \
        + 2 * (2 * _nbytes((tr, d), F32) + _nbytes((tr, d), BF16)) + 3 * _nbytes((tr, d), F32)
    pos_blk = (1, TOP_K, tr)
    return pl.pallas_call(
        functools.partial(_combine_body, tr=tr, nt=nt, alpha=alpha),
        grid=(nt,),
        in_specs=[pl.BlockSpec(pos_blk, lambda i: (i, 0, 0), memory_space=pltpu.SMEM),
                  pl.BlockSpec(pos_blk, lambda i: (jnp.minimum(i + 1, nt - 1), 0, 0),
                               memory_space=pltpu.SMEM),
                  pl.BlockSpec(memory_space=pl.ANY), row,
                  pl.BlockSpec((tr, TOP_K), lambda i: (i, 0)), vec, vec],
        out_specs=[row, row],
        out_shape=[jax.ShapeDtypeStruct((s, d), F32), jax.ShapeDtypeStruct((s, d), BF16)],
        scratch_shapes=[pltpu.VMEM((2, TOP_K, tr, d), F32), pltpu.SemaphoreType.DMA((2,))],
        compiler_params=pltpu.CompilerParams(dimension_semantics=("arbitrary",),
                                             vmem_limit_bytes=_vmem_limit(vmem)),
        name="moe_combine_ln",
    )(pos_t, pos_t, y, x, gate, g.reshape(1, d), b.reshape(1, d))


def _ple(xb, x, p, w_gate, w_proj, layer, bi, *, tm=MM_ROWS, tn=MM_COLS):
    s, d = x.shape
    pdim = p.shape[-1]
    blk, imap = _tile(min(tm, s), min(tn, d))

    def ep(accs, extras, outs, rows):
        proj = jnp.dot(extras[1][rows, :].astype(BF16), extras[2][...].astype(BF16),
                       preferred_element_type=F32)
        out = extras[0][rows, :] + jax.nn.sigmoid(accs[0]) * proj
        outs[0][rows, :] = out
        outs[1][rows, :] = out.astype(BF16)

    extras = [(x, blk, imap),
              (p, (None, None, blk[0], pdim), lambda i, j: (layer, bi, i, 0)),
              (w_proj, (None, pdim, blk[1]), lambda i, j: (layer, 0, j))]
    outs = [((s, d), F32, blk, imap), ((s, d), BF16, blk, imap)]
    return _fused_matmul(xb, [(w_gate, layer, 0)], d, extras, outs, ep, tm=tm, tn=tn, name="ple")


MOE_TILE_ROWS = 512


def kernel(x, p, ln1_g, ln1_b, ln2_g, ln2_b, a_w_in, a_lb_logits, a_out_g, a_w_o, kv_w, b_w_q, b_sinks, b_w_o, rel_bias, ffn_w1, ffn_w3, ffn_w2, moe_router, moe_w1, moe_w3, moe_w2, ple_w_proj, ple_w_gate):
    batch, s, d = x.shape
    depth = p.shape[0]
    n_a = a_w_in.shape[0]
    kv_dim = kv_w.shape[1] // 2
    n_kv = kv_dim // B_HEAD_DIM
    alpha = (2.0 * depth) ** 0.25
    moe_tm = min(MOE_TILE_ROWS, s)
    bias_band = _rel_bias_band(rel_bias.astype(F32))
    lb_logits = a_lb_logits.astype(F32)

    outs = []
    for bi in range(batch):
        xf = x[bi].astype(F32)
        xb = xf.astype(BF16)
        kvh = None
        for i in range(depth):
            if i < n_a:
                q, k, v, log_f, g, chunk_decay = _hgrn_in_proj(xb, a_w_in, lb_logits, i)
                hb = _hgrn_core(q, k, v, log_f, g, a_out_g[i].astype(F32), chunk_decay)
                y = _residual_matmul(hb, a_w_o, i, xf, alpha, name="mixer_out")
            else:
                j = i - n_a
                qh = _head_split_matmul(xb, b_w_q, j, d, name="swa_q")
                hb = _swa_core(qh, kvh, b_sinks[j].astype(F32), bias_band, n_kv)
                y = _residual_matmul(hb, b_w_o, j, xf, alpha, name="mixer_out")
            li = i // 2
            if i % 2 == 0:
                xf, xb = _layer_norm(y, ln1_g[i].astype(F32), ln1_b[i].astype(F32))
                hmid = _swiglu_up(xb, ffn_w1, ffn_w3, li)
                y = _residual_matmul(hmid, ffn_w2, li, xf, alpha, name="ffn_down")
                xf, xb = _layer_norm(y, ln2_g[i].astype(F32), ln2_b[i].astype(F32))
            else:
                xf, xb, gates, idx = _layer_norm(y, ln1_g[i].astype(F32), ln1_b[i].astype(F32),
                                                 moe_router[li].astype(F32))
                row_token, gate, pos, sched, ntiles = _routing_tables(gates, idx, moe_tm)
                xg = _moe_gather(xf, row_token, sched[SCHED_VALID], ntiles, moe_tm)
                hmid = _moe_up(xg, moe_w1, moe_w3, li, sched, moe_tm)
                yg = _moe_down(hmid, moe_w2, li, sched, moe_tm)
                xf, xb = _moe_combine_ln(yg, pos, gate, xf, ln2_g[i].astype(F32),
                                         ln2_b[i].astype(F32), alpha)
            xf, xb = _ple(xb, xf, p, ple_w_gate, ple_w_proj, i, bi)
            if i == n_a - 1:
                kvh = _head_split_matmul(xb, kv_w, None, 2 * kv_dim, name="kv_proj")
        outs.append(xf)
    return jnp.stack(outs, axis=0).astype(x.dtype)
```

```python
import functools

import numpy as np
import jax
import jax.numpy as jnp
from jax import lax
from jax.experimental import pallas as pl
from jax.experimental.pallas import tpu as pltpu

F32 = jnp.float32
BF16 = jnp.bfloat16

V7X_LANES = 128
V7X_SUBLANES = 8
V7X_SCOPED_VMEM_BYTES = 60000 * 1024
COMPILER_SCRATCH_BYTES = 6 * 1024 * 1024

MM_ROWS = 2048
MM_COLS = 256
MM_COLS_BF16_OUT = 512
MM_SUB_ROWS = 128

A_HEAD_DIM = 128
A_CHUNK = 64
A_SUB = 16
A_SAFE_LOG2_DECAY = 100.0
B_HEAD_DIM = 64
B_LOGIT_SCALE = B_HEAD_DIM ** -0.5
assert B_LOGIT_SCALE == 2.0 ** -3
B_BLOCK = 128
WINDOW = 128
REL_BUCKETS = 32
REL_MAX_DISTANCE = 128
TOP_K = 2
LN_EPS = 1e-5
RMS_EPS = 1e-6
LOG2E = 1.4426950408889634


def _vmem_limit(block_bytes):
    return int(min(block_bytes + COMPILER_SCRATCH_BYTES, V7X_SCOPED_VMEM_BYTES))


def _nbytes(shape, dtype):
    return int(np.prod([n for n in shape if n is not None])) * jnp.dtype(dtype).itemsize


def _silu(x):
    return x * jax.nn.sigmoid(x)


def _mm_body(*refs, nw, ne, no, epilogue):
    x_ref = refs[0]
    w_refs = refs[1:1 + nw]
    extra_refs = refs[1 + nw:1 + nw + ne]
    out_refs = refs[1 + nw + ne:1 + nw + ne + no]
    wbs = [w_ref[...].astype(BF16) for w_ref in w_refs]
    tm = x_ref.shape[0]
    sub = min(tm, MM_SUB_ROWS)
    for r0 in range(0, tm, sub):
        rows = slice(r0, r0 + sub)
        xv = x_ref[rows, :]
        prods = [jnp.dot(xv, wb, preferred_element_type=F32) for wb in wbs]
        epilogue(prods, extra_refs, out_refs, rows)


def _fused_matmul(x, ws, n_cols, extras, outs, epilogue, *, tm, tn, name, tk=None, k_block=0):
    m, kdim = x.shape
    tk = tk or kdim
    tm, tn = min(tm, m), min(tn, n_cols)
    assert m % tm == 0 and n_cols % tn == 0 and kdim % tk == 0
    in_specs = [pl.BlockSpec((tm, tk), lambda i, j: (i, k_block))]
    operands = [x]
    vmem = 2 * _nbytes((tm, tk), x.dtype)
    for w, layer, off in ws:
        if layer is None:
            spec = pl.BlockSpec((tk, tn), lambda i, j, off=off: (k_block, j + off))
        else:
            spec = pl.BlockSpec((None, tk, tn), lambda i, j, off=off, layer=layer:
                                (layer, k_block, j + off))
        in_specs.append(spec)
        operands.append(w)
        vmem += 2 * _nbytes((tk, tn), w.dtype) + _nbytes((tk, tn), BF16)
    for arr, bshape, imap in extras:
        in_specs.append(pl.BlockSpec(bshape, imap))
        operands.append(arr)
        vmem += 2 * _nbytes(bshape, arr.dtype)
    out_shapes, out_specs = [], []
    for shape, dtype, bshape, imap in outs:
        out_shapes.append(jax.ShapeDtypeStruct(shape, dtype))
        out_specs.append(pl.BlockSpec(bshape, imap))
        vmem += 2 * _nbytes(bshape, dtype)
    vmem += (2 + len(ws)) * _nbytes((tm, tn), F32)
    body = functools.partial(_mm_body, nw=len(ws), ne=len(extras), no=len(outs), epilogue=epilogue)
    return pl.pallas_call(
        body,
        grid=(m // tm, n_cols // tn),
        in_specs=in_specs,
        out_specs=out_specs,
        out_shape=out_shapes,
        compiler_params=pltpu.CompilerParams(dimension_semantics=("parallel", "parallel"),
                                             vmem_limit_bytes=_vmem_limit(vmem)),
        name=name,
    )(*operands)


def _tile(tm, tn):
    return (tm, tn), (lambda i, j: (i, j))


def _ln_rows(y, g, b):
    mu = jnp.mean(y, axis=-1, keepdims=True)
    yc = y - mu
    var = jnp.mean(yc * yc, axis=-1, keepdims=True)
    return yc * lax.rsqrt(var + LN_EPS) * g + b


def _route_top2(x, w_router):
    logits = jnp.dot(x, w_router, preferred_element_type=F32, precision=lax.Precision.HIGHEST)
    ne = logits.shape[1]
    eid = lax.broadcasted_iota(jnp.int32, logits.shape, 1)
    m1 = jnp.max(logits, axis=-1, keepdims=True)
    i1 = jnp.min(jnp.where(logits == m1, eid, ne), axis=-1, keepdims=True)
    rest = jnp.where(eid == i1, -jnp.inf, logits)
    m2 = jnp.max(rest, axis=-1, keepdims=True)
    i2 = jnp.min(jnp.where(rest == m2, eid, ne), axis=-1, keepdims=True)
    e2 = jnp.exp(m2 - m1)
    w_top1 = 1.0 / (1.0 + e2)
    w_top2 = e2 / (1.0 + e2)
    gates = jnp.where(eid == i1, w_top1, 0.0) + jnp.where(eid == i2, w_top2, 0.0)
    return gates, jnp.where(eid == 0, i1, i2)[:, :TOP_K]


def _ln_body(*refs, route):
    if route:
        y_ref, g_ref, b_ref, wr_ref, o32_ref, o16_ref, gates_ref, idx_ref = refs
    else:
        y_ref, g_ref, b_ref, o32_ref, o16_ref = refs
    out = _ln_rows(y_ref[...], g_ref[...], b_ref[...])
    o32_ref[...] = out
    o16_ref[...] = out.astype(BF16)
    if route:
        gates_ref[...], idx_ref[...] = _route_top2(out, wr_ref[...])


def _layer_norm(y, g, b, w_router=None, *, tr=256):
    s, d = y.shape
    tr = min(tr, s)
    assert s % tr == 0
    row = pl.BlockSpec((tr, d), lambda i: (i, 0))
    vec = pl.BlockSpec((1, d), lambda i: (0, 0))
    in_specs, operands = [row, vec, vec], [y, g.reshape(1, d), b.reshape(1, d)]
    out_specs = [row, row]
    out_shape = [jax.ShapeDtypeStruct((s, d), F32), jax.ShapeDtypeStruct((s, d), BF16)]
    vmem = 2 * (2 * _nbytes((tr, d), F32) + _nbytes((tr, d), BF16)) + 2 * _nbytes((tr, d), F32)
    if w_router is not None:
        ne = w_router.shape[1]
        in_specs.append(pl.BlockSpec((d, ne), lambda i: (0, 0)))
        operands.append(w_router)
        out_specs += [pl.BlockSpec((tr, ne), lambda i: (i, 0)),
                      pl.BlockSpec((tr, TOP_K), lambda i: (i, 0))]
        out_shape += [jax.ShapeDtypeStruct((s, ne), F32), jax.ShapeDtypeStruct((s, TOP_K), jnp.int32)]
        vmem += 2 * _nbytes((d, V7X_LANES), F32) + 4 * _nbytes((tr, d), F32)
    return pl.pallas_call(
        functools.partial(_ln_body, route=w_router is not None),
        grid=(s // tr,),
        in_specs=in_specs,
        out_specs=out_specs,
        out_shape=out_shape,
        compiler_params=pltpu.CompilerParams(dimension_semantics=("parallel",),
                                             vmem_limit_bytes=_vmem_limit(vmem)),
        name="layer_norm_route" if w_router is not None else "layer_norm",
    )(*operands)


def _lower_bound(lbl, layer):
    mx = jnp.max(lbl, axis=0, keepdims=True)
    e = jnp.exp(lbl - mx)
    return jnp.sum(e[:layer + 1], axis=0, keepdims=True) / jnp.sum(e, axis=0, keepdims=True)


def _hgrn_in_proj(xb, w_in, lb_logits, layer, *, tm=MM_ROWS, tn=MM_COLS):
    s, d = xb.shape
    blk, imap = _tile(min(tm, s), min(tn, d))

    def ep_silu(accs, extras, outs, rows):
        outs[0][rows, :] = _silu(accs[0]).astype(BF16)

    def ep_id(accs, extras, outs, rows):
        outs[0][rows, :] = accs[0].astype(BF16)

    def ep_forget(accs, extras, outs, rows):
        lb = _lower_bound(extras[0][...], layer)
        f_raw = accs[0]
        e = jnp.exp(-jnp.abs(f_raw))
        log_sig = jnp.minimum(f_raw, 0.0) - jnp.log(1.0 + e)
        sig_neg = jnp.where(f_raw >= 0.0, e, 1.0) / (1.0 + e)
        a = jnp.log(lb)
        b = jnp.log1p(-lb) + log_sig
        log_f = jnp.maximum(a, b) + jnp.log(1.0 + jnp.exp(-jnp.abs(a - b)))
        outs[0][rows, :] = log_f
        outs[1][rows, :] = ((1.0 - lb) * sig_neg).astype(BF16)
        nrow, ncol = log_f.shape
        outs[2][rows.start // A_CHUNK:rows.stop // A_CHUNK, :] = jnp.sum(
            log_f.reshape(nrow // A_CHUNK, A_CHUNK, ncol), axis=1)

    def call(seg, epilogue, extras, out_dtypes, name, more_outs=(), cols=tn):
        cblk, cimap = _tile(blk[0], min(cols, d))
        outs = [((s, d), dt, cblk, cimap) for dt in out_dtypes] + list(more_outs)
        return _fused_matmul(xb, [(w_in, layer, seg * (d // cblk[1]))], d, extras, outs, epilogue,
                             tm=tm, tn=cols, name=name)

    nslot = lb_logits.shape[0]
    lb_extra = [(lb_logits, (nslot, blk[1]), lambda i, j: (0, j))]
    decay_out = ((s // A_CHUNK, d), F32, (blk[0] // A_CHUNK, blk[1]), imap)
    (q,) = call(0, ep_silu, [], [BF16], "hgrn_q", cols=MM_COLS_BF16_OUT)
    log_f, k, chunk_decay = call(1, ep_forget, lb_extra, [F32, BF16], "hgrn_f",
                                 more_outs=[decay_out])
    (v,) = call(2, ep_id, [], [BF16], "hgrn_v", cols=MM_COLS_BF16_OUT)
    (g,) = call(3, ep_silu, [], [BF16], "hgrn_g", cols=MM_COLS_BF16_OUT)
    return q, k, v, log_f, g, chunk_decay


def _chunk_cumsum(lf_ref, tc):
    x = lf_ref[...]
    row_in_chunk = lax.broadcasted_iota(jnp.int32, x.shape, 0) % A_CHUNK
    shift = 1
    while shift < A_CHUNK:
        x = x + jnp.where(row_in_chunk >= shift, pltpu.roll(x, shift, axis=0), 0.0)
        shift *= 2
    return x


def _hgrn_body(bounded_ref, q_ref, k_ref, v_ref, lf_ref, g_ref, og_ref, o_ref, st_ref, *, tc):
    @pl.when(pl.program_id(1) == 0)
    def _():
        st_ref[...] = jnp.zeros_like(st_ref)

    out_g = og_ref[...]
    half = A_SUB // 2
    lane = lax.broadcasted_iota(jnp.int32, (half, A_HEAD_DIM), 1)
    hrow = lax.broadcasted_iota(jnp.int32, (half, A_HEAD_DIM), 0)
    nsub = A_CHUNK // A_SUB
    nchunk = tc // A_CHUNK

    def finish(rows, o):
        o = o * lax.rsqrt(jnp.mean(o * o, axis=-1, keepdims=True) + RMS_EPS)
        o_ref[rows, :] = (o * out_g * g_ref[rows, :].astype(F32)).astype(BF16)

    def as_column(row):
        return jnp.transpose(jnp.broadcast_to(row, (V7X_SUBLANES, row.shape[1])))[:, :1]

    def inter_chunk(ci, cum, st_t):
        rows = slice(ci * A_CHUNK, (ci + 1) * A_CHUNK)
        c2 = cum[rows] * LOG2E
        qc = q_ref[rows, :].astype(F32)
        kc = k_ref[rows, :].astype(F32)
        vc = v_ref[rows, :]
        last = c2[A_CHUNK - 1:A_CHUNK]
        qe = (qc * jnp.exp2(c2)).astype(BF16)
        o_inter = lax.dot_general(qe, st_t.astype(BF16), (((1,), (1,)), ((), ())),
                                  preferred_element_type=F32)
        kd = (kc * jnp.exp2(last - c2)).astype(BF16)
        upd = lax.dot_general(vc, kd, (((0,), (0,)), ((), ())), preferred_element_type=F32)
        return rows, c2, qc, kc, vc, qe, o_inter, st_t * jnp.exp2(last) + upd

    def run_factorised():
        c2 = _chunk_cumsum(lf_ref, tc) * LOG2E
        qe = (q_ref[...].astype(F32) * jnp.exp2(c2)).astype(BF16)
        kb32 = k_ref[...].astype(F32) * jnp.exp2(-c2)
        kb = kb32.astype(BF16)
        t_i = lax.broadcasted_iota(jnp.int32, (A_CHUNK, A_CHUNK), 0)
        s_i = lax.broadcasted_iota(jnp.int32, (A_CHUNK, A_CHUNK), 1)
        causal = t_i >= s_i
        st = st_ref[...]
        outs = []
        for ci in range(nchunk):
            rows = slice(ci * A_CHUNK, (ci + 1) * A_CHUNK)
            vc = v_ref[rows, :]
            grow = jnp.exp2(c2[(ci + 1) * A_CHUNK - 1:(ci + 1) * A_CHUNK])
            sc = lax.dot_general(qe[rows], kb[rows], (((1,), (1,)), ((), ())),
                                 preferred_element_type=F32)
            scores = jnp.where(causal, sc, 0.0).astype(BF16)
            outs.append(jnp.dot(jnp.concatenate([qe[rows], scores], axis=1),
                                jnp.concatenate([st.astype(BF16), vc], axis=0),
                                preferred_element_type=F32))
            kd = (kb32[rows] * grow).astype(BF16)
            upd = lax.dot_general(kd, vc, (((0,), (0,)), ((), ())), preferred_element_type=F32)
            st = st * as_column(grow) + upd
        st_ref[...] = st
        finish(slice(0, tc), jnp.concatenate(outs, axis=0))

    def direct_chunk(ci, cum, st):
        rows, c2, qc, kc, vc, qe, o_inter, st = inter_chunk(ci, cum, st)

        p_rows = []
        k_blocks, prev_ref = [], None
        for si in range(nsub):
            lo_r = si * A_SUB
            cs, qs, ks = c2[lo_r:lo_r + A_SUB], qc[lo_r:lo_r + A_SUB], kc[lo_r:lo_r + A_SUB]
            off = None
            if si > 0:
                ref = c2[lo_r - 1:lo_r]
                if prev_ref is not None:
                    rebase = jnp.exp2(ref - prev_ref)
                    k_blocks = [kb * rebase for kb in k_blocks]
                k_blocks.append(kc[lo_r - A_SUB:lo_r] * jnp.exp2(ref - c2[lo_r - A_SUB:lo_r]))
                prev_ref = ref
                k_t = jnp.concatenate(
                    k_blocks + [jnp.zeros((A_CHUNK - lo_r, A_HEAD_DIM), F32)], axis=0)
                q_t = (qs * jnp.exp2(cs - ref)).astype(BF16)
                off = lax.dot_general(q_t, k_t.astype(BF16), (((1,), (1,)), ((), ())),
                                      preferred_element_type=F32)
            c_h = (cs[:half], cs[half:])
            q_h = (qs[:half], qs[half:])
            p_h = [jnp.zeros((half, A_HEAD_DIM), F32), jnp.zeros((half, A_HEAD_DIM), F32)]
            for sj in range(A_SUB):
                for hh in range(sj // half, 2):
                    decay = jnp.exp2(c_h[hh] - cs[sj:sj + 1])
                    colv = jnp.sum(q_h[hh] * (ks[sj:sj + 1] * decay), axis=1, keepdims=True)
                    sel = lane == lo_r + sj
                    if sj // half == hh:
                        sel = sel & (hrow >= sj - hh * half)
                    p_h[hh] = jnp.where(sel, colv, p_h[hh])
            for hh in range(2):
                ph = p_h[hh][:, :A_CHUNK]
                p_rows.append(ph if off is None else ph + off[hh * half:(hh + 1) * half])
        scores = jnp.concatenate(p_rows, axis=0).astype(BF16)
        finish(rows, o_inter + jnp.dot(scores, vc, preferred_element_type=F32))
        return st

    def run_direct():
        cum = _chunk_cumsum(lf_ref, tc)
        st_t = st_ref[...].T
        for ci in range(nchunk):
            st_t = direct_chunk(ci, cum, st_t)
        st_ref[...] = st_t.T

    bounded = bounded_ref[pl.program_id(0), pl.program_id(1)] > 0
    pl.when(bounded)(run_factorised)
    pl.when(jnp.logical_not(bounded))(run_direct)


def _hgrn_core(q, k, v, log_f, g, out_g, chunk_decay, *, tc=1024):
    s, d = q.shape
    tc = min(tc, s)
    assert s % tc == 0 and tc % A_CHUNK == 0 and d % A_HEAD_DIM == 0
    nhead, nblk = d // A_HEAD_DIM, s // tc
    worst = jnp.min(chunk_decay.reshape(nblk, tc // A_CHUNK, nhead, A_HEAD_DIM), axis=(1, 3))
    bounded = (worst.T * LOG2E >= -A_SAFE_LOG2_DECAY).astype(jnp.int32)
    blk = pl.BlockSpec((tc, A_HEAD_DIM), lambda h, c, flags: (c, h))
    vmem = 2 * (4 * _nbytes((tc, A_HEAD_DIM), BF16) + 2 * _nbytes((tc, A_HEAD_DIM), F32)) \
        + 12 * _nbytes((tc, A_HEAD_DIM), F32)
    grid_spec = pltpu.PrefetchScalarGridSpec(
        num_scalar_prefetch=1,
        grid=(nhead, nblk),
        in_specs=[blk, blk, blk, blk, blk,
                  pl.BlockSpec((1, A_HEAD_DIM), lambda h, c, flags: (0, h))],
        out_specs=blk,
        scratch_shapes=[pltpu.VMEM((A_HEAD_DIM, A_HEAD_DIM), F32)],
    )
    return pl.pallas_call(
        functools.partial(_hgrn_body, tc=tc),
        grid_spec=grid_spec,
        out_shape=jax.ShapeDtypeStruct((s, d), BF16),
        compiler_params=pltpu.CompilerParams(dimension_semantics=("parallel", "arbitrary"),
                                             vmem_limit_bytes=_vmem_limit(vmem)),
        name="hgrn_core",
    )(bounded, q, k, v, log_f, g, out_g.reshape(1, d))


def _residual_matmul(hb, w, layer, resid, alpha, *, name, tm=MM_ROWS, tn=MM_COLS):
    s, d = resid.shape
    kdim = hb.shape[1]
    tk = min(kdim, d)
    blk, imap = _tile(min(tm, s), min(tn, d))
    y = resid
    for kb in range(kdim // tk):
        scale = alpha if kb == 0 else 1.0

        def ep(accs, extras, outs, rows, scale=scale):
            outs[0][rows, :] = scale * extras[0][rows, :] + accs[0]

        (y,) = _fused_matmul(hb, [(w, layer, 0)], d, [(y, blk, imap)], [((s, d), F32, blk, imap)],
                             ep, tm=tm, tn=tn, tk=tk, k_block=kb, name=name)
    return y


def _rel_bucket_band():
    i = np.arange(B_BLOCK)[:, None]
    j = np.arange(2 * B_BLOCK)[None, :]
    d = np.clip(B_BLOCK + i - j, 0, None)
    max_exact = REL_BUCKETS // 2
    large = max_exact + (np.log(np.maximum(d, 1) / max_exact)
                         / np.log(REL_MAX_DISTANCE / max_exact)
                         * (REL_BUCKETS - max_exact)).astype(np.int32)
    large = np.minimum(large, REL_BUCKETS - 1)
    return np.where(d < max_exact, d, large).astype(np.int32)


def _bias_body(rb_ref, bucket_ref, o_ref, *, heads_per_step):
    h0 = pl.program_id(0) * heads_per_step
    bucket = bucket_ref[...]
    qi = lax.broadcasted_iota(jnp.int32, bucket.shape, 0)
    kj = lax.broadcasted_iota(jnp.int32, bucket.shape, 1)
    dist = B_BLOCK + qi - kj
    in_window = (dist >= 0) & (dist < WINDOW)
    for hh in range(heads_per_step):
        acc = jnp.zeros(bucket.shape, F32)
        for b in range(REL_BUCKETS):
            acc = jnp.where(bucket == b, rb_ref[b, h0 + hh], acc)
        o_ref[hh] = jnp.where(in_window, acc, -jnp.inf)


def _rel_bias_band(rel_bias, *, heads_per_step=8):
    nh = rel_bias.shape[1]
    bucket = jnp.asarray(_rel_bucket_band())
    return pl.pallas_call(
        functools.partial(_bias_body, heads_per_step=heads_per_step),
        grid=(nh // heads_per_step,),
        in_specs=[pl.BlockSpec(memory_space=pltpu.SMEM),
                  pl.BlockSpec((B_BLOCK, 2 * B_BLOCK), lambda i: (0, 0))],
        out_specs=pl.BlockSpec((heads_per_step, B_BLOCK, 2 * B_BLOCK), lambda i: (i, 0, 0)),
        out_shape=jax.ShapeDtypeStruct((nh, B_BLOCK, 2 * B_BLOCK), F32),
        compiler_params=pltpu.CompilerParams(dimension_semantics=("parallel",)),
        name="rel_bias_band",
    )(rel_bias, bucket)


def _head_split_matmul(xb, w, layer, n_cols, *, name, tm=MM_ROWS, tn=MM_COLS):
    s, kdim = xb.shape
    tm, tn = min(tm, s), min(tn, n_cols)
    hpt = tn // B_HEAD_DIM

    def ep(accs, extras, outs, rows):
        for c in range(hpt):
            outs[0][c, rows, :] = accs[0][:, c * B_HEAD_DIM:(c + 1) * B_HEAD_DIM].astype(BF16)

    outs = [((n_cols // B_HEAD_DIM, s, B_HEAD_DIM), BF16, (hpt, tm, B_HEAD_DIM),
             lambda i, j: (j, i, 0))]
    (out,) = _fused_matmul(xb, [(w, layer, 0)], n_cols, [], outs, ep, tm=tm, tn=tn, name=name)
    return out


def _attn_body(sink_ref, q_ref, kp_ref, kc_ref, vp_ref, vc_ref, bias_ref, o_ref, *, group, qb):
    h = pl.program_id(0)
    m_blk = pl.program_id(1)
    c = B_BLOCK
    k_all = jnp.concatenate([kp_ref[0], kc_ref[0]], axis=0)
    v_all = jnp.concatenate([vp_ref[0], vc_ref[0]], axis=0)
    kj = lax.broadcasted_iota(jnp.int32, (c, 2 * c), 1)
    for b in range(qb):
        q = q_ref[:, b * c:(b + 1) * c, :].reshape(group * c, B_HEAD_DIM) * B_LOGIT_SCALE
        kb = k_all[b * c:(b + 2) * c]
        vb = v_all[b * c:(b + 2) * c]
        logits = lax.dot_general(q, kb, (((1,), (1,)), ((), ())), preferred_element_type=F32)
        probs = []
        for gi in range(group):
            lg = logits[gi * c:(gi + 1) * c] + bias_ref[gi]
            if b == 0:
                lg = jnp.where((m_blk > 0) | (kj >= c), lg, -jnp.inf)
            sink = sink_ref[h * group + gi]
            m = jnp.maximum(jnp.max(lg, axis=-1, keepdims=True), sink)
            e = jnp.exp(lg - m)
            denom = jnp.sum(e, axis=-1, keepdims=True) + jnp.exp(sink - m)
            probs.append((e / denom).astype(BF16))
        pv = jnp.dot(jnp.concatenate(probs, axis=0), vb, preferred_element_type=F32)
        o_ref[b * c:(b + 1) * c, :] = jnp.concatenate(
            [pv[gi * c:(gi + 1) * c] for gi in range(group)], axis=1).astype(BF16)


def _swa_core(qh, kvh, sinks, bias, n_kv, *, qb=4):
    nq, s, hd = qh.shape
    group = nq // n_kv
    nb = s // B_BLOCK
    qb = min(qb, nb)
    assert nb % qb == 0
    prev = lambda m: jnp.maximum(m * qb - 1, 0)
    prev_blk = (1, B_BLOCK, hd)
    cur_blk = (1, qb * B_BLOCK, hd)
    in_specs = [
        pl.BlockSpec(memory_space=pltpu.SMEM),
        pl.BlockSpec((group, qb * B_BLOCK, hd), lambda h, m: (h, m, 0)),
        pl.BlockSpec(prev_blk, lambda h, m: (h, prev(m), 0)),
        pl.BlockSpec(cur_blk, lambda h, m: (h, m, 0)),
        pl.BlockSpec(prev_blk, lambda h, m: (h + n_kv, prev(m), 0)),
        pl.BlockSpec(cur_blk, lambda h, m: (h + n_kv, m, 0)),
        pl.BlockSpec((group, B_BLOCK, 2 * B_BLOCK), lambda h, m: (h, 0, 0)),
    ]
    vmem = 2 * (_nbytes((group, B_BLOCK, 2 * B_BLOCK), F32)
                + (group + 3) * _nbytes((qb * B_BLOCK, V7X_LANES), BF16)
                + _nbytes((qb * B_BLOCK, group * hd), BF16)) \
        + 6 * qb * _nbytes((group * B_BLOCK, 2 * B_BLOCK), F32)
    return pl.pallas_call(
        functools.partial(_attn_body, group=group, qb=qb),
        grid=(n_kv, nb // qb),
        in_specs=in_specs,
        out_specs=pl.BlockSpec((qb * B_BLOCK, group * hd), lambda h, m: (m, h)),
        out_shape=jax.ShapeDtypeStruct((s, nq * hd), BF16),
        compiler_params=pltpu.CompilerParams(dimension_semantics=("parallel", "parallel"),
                                             vmem_limit_bytes=_vmem_limit(vmem)),
        name="swa_core",
    )(sinks, qh, kvh, kvh, kvh, kvh, bias)


def _swiglu_up(xb, w1, w3, layer, *, tm=MM_ROWS, tn=MM_COLS):
    s, d = xb.shape
    f = w1.shape[2]
    blk, imap = _tile(min(tm, s), min(tn, f))

    def ep(accs, extras, outs, rows):
        outs[0][rows, :] = (_silu(accs[0]) * accs[1]).astype(BF16)

    (h,) = _fused_matmul(xb, [(w1, layer, 0), (w3, layer, 0)], f, [],
                         [((s, f), BF16, blk, imap)], ep, tm=tm, tn=tn, name="swiglu_up")
    return h


def _routing_tables(gates, idx, tm):
    s, ne = gates.shape
    n_assign = s * TOP_K
    ntiles = n_assign // tm + ne
    e_flat = idx.reshape(n_assign)
    onehot = (e_flat[:, None] == jnp.arange(ne, dtype=jnp.int32)[None, :]).astype(jnp.int32)
    rank = jnp.take_along_axis(jnp.cumsum(onehot, axis=0) - onehot, e_flat[:, None], axis=1)[:, 0]
    counts = jnp.sum(onehot, axis=0)
    padded = ((counts + tm - 1) // tm) * tm
    ends = jnp.cumsum(padded)
    pos = (ends - padded)[e_flat] + rank
    token = jnp.arange(n_assign, dtype=jnp.int32) // TOP_K
    gate = jnp.take_along_axis(gates, idx, axis=1)
    row_token = jnp.zeros((ntiles * tm,), jnp.int32).at[pos].set(token)
    tile_start = jnp.arange(ntiles, dtype=jnp.int32) * tm
    tile_expert = jnp.minimum(jnp.searchsorted(ends, tile_start, side="right"), ne - 1).astype(jnp.int32)
    tile_valid = tile_start < ends[-1]
    first = tile_valid & jnp.concatenate(
        [jnp.ones((1,), bool), tile_expert[1:] != tile_expert[:-1]])
    slot = (jnp.cumsum(first.astype(jnp.int32)) - 1) % 2
    eid = jnp.arange(ne, dtype=jnp.int32)
    later = (counts > 0)[None, :] & (eid[None, :] > eid[:, None])
    next_used = jnp.min(jnp.where(later, eid[None, :], ne), axis=1)
    next_used = jnp.where(next_used == ne, -1, next_used)
    sched = jnp.stack([tile_expert, tile_valid.astype(jnp.int32), first.astype(jnp.int32),
                       slot.astype(jnp.int32), next_used[tile_expert]]).astype(jnp.int32)
    return row_token, gate, pos.reshape(s, TOP_K).astype(jnp.int32), sched, ntiles


GATHER_ISSUE_UNROLL = 8


def _gather_body(tv_ref, tok_ref, tok_next_ref, x_hbm, o_ref, buf, sem, *, tm, ntiles):
    i = pl.program_id(0)
    slot = i % 2

    def row_copy(tile_tok_ref, slot_, r):
        return pltpu.make_async_copy(x_hbm.at[pl.ds(tile_tok_ref[0, 0, r], 1)],
                                     buf.at[slot_, pl.ds(r, 1)], sem.at[slot_])

    def start_tile(tile_tok_ref, slot_):
        def body(r, carry):
            row_copy(tile_tok_ref, slot_, r).start()
            return carry
        lax.fori_loop(0, tm, body, 0, unroll=GATHER_ISSUE_UNROLL)

    @pl.when((i == 0) & (tv_ref[0] > 0))
    def _():
        start_tile(tok_ref, slot)

    @pl.when((i + 1 < ntiles) & (tv_ref[jnp.minimum(i + 1, ntiles - 1)] > 0))
    def _():
        start_tile(tok_next_ref, 1 - slot)

    @pl.when(tv_ref[i] > 0)
    def _():
        def wait_body(r, carry):
            row_copy(tok_ref, slot, r).wait()
            return carry
        lax.fori_loop(0, tm, wait_body, 0, unroll=GATHER_ISSUE_UNROLL)
        o_ref[...] = buf[slot].astype(BF16)

    @pl.when(tv_ref[i] == 0)
    def _():
        o_ref[...] = jnp.zeros_like(o_ref)


def _moe_gather(x, row_token, tile_valid, ntiles, tm):
    s, d = x.shape
    vmem = 2 * _nbytes((tm, d), F32) + 2 * _nbytes((tm, d), BF16) + _nbytes((tm, d), F32)
    tok = row_token.reshape(ntiles, 1, tm)
    grid_spec = pltpu.PrefetchScalarGridSpec(
        num_scalar_prefetch=1,
        grid=(ntiles,),
        in_specs=[pl.BlockSpec((1, 1, tm), lambda i, tv: (i, 0, 0), memory_space=pltpu.SMEM),
                  pl.BlockSpec((1, 1, tm), lambda i, tv: (jnp.minimum(i + 1, ntiles - 1), 0, 0),
                               memory_space=pltpu.SMEM),
                  pl.BlockSpec(memory_space=pl.ANY)],
        out_specs=pl.BlockSpec((tm, d), lambda i, tv: (i, 0)),
        scratch_shapes=[pltpu.VMEM((2, tm, d), F32), pltpu.SemaphoreType.DMA((2,))],
    )
    return pl.pallas_call(
        functools.partial(_gather_body, tm=tm, ntiles=ntiles),
        grid_spec=grid_spec,
        out_shape=jax.ShapeDtypeStruct((ntiles * tm, d), BF16),
        compiler_params=pltpu.CompilerParams(dimension_semantics=("arbitrary",),
                                             vmem_limit_bytes=_vmem_limit(vmem)),
        name="moe_gather",
    )(tile_valid, tok, tok, x)


SCHED_EXPERT, SCHED_VALID, SCHED_FIRST, SCHED_SLOT, SCHED_NEXT = range(5)


def _grouped_body(sched_ref, x_ref, *rest, nw, layer, tcol, epilogue):
    w_hbm, o_ref, wbufs, sem = rest[:nw], rest[nw], rest[nw + 1:2 * nw + 1], rest[2 * nw + 1]
    j, i = pl.program_id(0), pl.program_id(1)
    expert, slot, nxt = sched_ref[SCHED_EXPERT, i], sched_ref[SCHED_SLOT, i], sched_ref[SCHED_NEXT, i]
    first = sched_ref[SCHED_FIRST, i] > 0
    col0 = pl.multiple_of(j * tcol, tcol)

    def copies(e, s):
        return [pltpu.make_async_copy(w.at[layer, e, :, pl.ds(col0, tcol)], wb.at[s], sem.at[s])
                for w, wb in zip(w_hbm, wbufs)]

    @pl.when(i == 0)
    def _():
        for c in copies(expert, slot):
            c.start()

    @pl.when(first & (nxt >= 0))
    def _():
        for c in copies(nxt, 1 - slot):
            c.start()

    @pl.when(first)
    def _():
        for c in copies(expert, slot):
            c.wait()

    @pl.when(sched_ref[SCHED_VALID, i] > 0)
    def _():
        wbs = [wb[slot].astype(BF16) for wb in wbufs]
        sub = min(x_ref.shape[0], MM_SUB_ROWS)
        for r0 in range(0, x_ref.shape[0], sub):
            xv = x_ref[r0:r0 + sub, :]
            o_ref[r0:r0 + sub, :] = epilogue(
                [jnp.dot(xv, wb, preferred_element_type=F32) for wb in wbs])

    @pl.when(sched_ref[SCHED_VALID, i] == 0)
    def _():
        o_ref[...] = jnp.zeros_like(o_ref)


def _grouped_matmul(x, ws, layer, sched, tm, tcol, out_dtype, epilogue, name):
    rows, kdim = x.shape
    n = ws[0].shape[3]
    tcol = min(tcol, n)
    ntiles = rows // tm
    vmem = 2 * _nbytes((tm, kdim), BF16) + 2 * _nbytes((tm, tcol), out_dtype) \
        + len(ws) * (2 * _nbytes((kdim, tcol), F32) + _nbytes((kdim, tcol), BF16)) \
        + (1 + len(ws)) * _nbytes((tm, tcol), F32)
    grid_spec = pltpu.PrefetchScalarGridSpec(
        num_scalar_prefetch=1,
        grid=(n // tcol, ntiles),
        in_specs=[pl.BlockSpec((tm, kdim), lambda j, i, sched: (i, 0))]
        + [pl.BlockSpec(memory_space=pl.ANY)] * len(ws),
        out_specs=pl.BlockSpec((tm, tcol), lambda j, i, sched: (i, j)),
        scratch_shapes=[pltpu.VMEM((2, kdim, tcol), F32) for _ in ws]
        + [pltpu.SemaphoreType.DMA((2,))],
    )
    return pl.pallas_call(
        functools.partial(_grouped_body, nw=len(ws), layer=layer, tcol=tcol, epilogue=epilogue),
        grid_spec=grid_spec,
        out_shape=jax.ShapeDtypeStruct((rows, n), out_dtype),
        compiler_params=pltpu.CompilerParams(dimension_semantics=("arbitrary", "arbitrary"),
                                             vmem_limit_bytes=_vmem_limit(vmem)),
        name=name,
    )(sched, x, *ws)


def _moe_up(xg, w1, w3, layer, sched, tm, *, tf=512):
    return _grouped_matmul(xg, [w1, w3], layer, sched, tm, tf, BF16,
                           lambda p: (_silu(p[0]) * p[1]).astype(BF16), "moe_up")


def _moe_down(h, w2, layer, sched, tm, *, tn=2048):
    return _grouped_matmul(h, [w2], layer, sched, tm, tn, F32, lambda p: p[0], "moe_down")


def _combine_body(pos_ref, pos_next_ref, y_hbm, x_ref, gate_ref, g_ref, b_ref, o32_ref, o16_ref,
                  buf, sem, *, tr, nt, alpha):
    i = pl.program_id(0)
    slot = i % 2

    def row_copy(tile_pos_ref, slot_, r, kk):
        return pltpu.make_async_copy(y_hbm.at[pl.ds(tile_pos_ref[0, kk, r], 1)],
                                     buf.at[slot_, kk, pl.ds(r, 1)], sem.at[slot_])

    def start_tile(tile_pos_ref, slot_):
        def body(r, carry):
            for kk in range(TOP_K):
                row_copy(tile_pos_ref, slot_, r, kk).start()
            return carry
        lax.fori_loop(0, tr, body, 0, unroll=GATHER_ISSUE_UNROLL)

    @pl.when(i == 0)
    def _():
        start_tile(pos_ref, slot)

    @pl.when(i + 1 < nt)
    def _():
        start_tile(pos_next_ref, 1 - slot)

    def wait_body(r, carry):
        for kk in range(TOP_K):
            row_copy(pos_ref, slot, r, kk).wait()
        return carry
    lax.fori_loop(0, tr, wait_body, 0, unroll=GATHER_ISSUE_UNROLL)

    gate = gate_ref[...]
    f = buf[slot, 0] * gate[:, 0:1]
    for kk in range(1, TOP_K):
        f = f + buf[slot, kk] * gate[:, kk:kk + 1]
    out = _ln_rows(alpha * x_ref[...] + f, g_ref[...], b_ref[...])
    o32_ref[...] = out
    o16_ref[...] = out.astype(BF16)


def _moe_combine_ln(y, pos, gate, x, g, b, alpha, *, tr=256):
    s, d = x.shape
    tr = min(tr, s)
    nt = s // tr
    pos_t = pos.reshape(nt, tr, TOP_K).transpose(0, 2, 1)
    row = pl.BlockSpec((tr, d), lambda i: (i, 0))
    vec = pl.BlockSpec((1, d), lambda i: (0, 0))
    vmem = 2 * TOP_K * _nbytes((tr, d), F32) \
        + 2 * (2 * _nbytes((tr, d), F32) + _nbytes((tr, d), BF16)) + 3 * _nbytes((tr, d), F32)
    pos_blk = (1, TOP_K, tr)
    return pl.pallas_call(
        functools.partial(_combine_body, tr=tr, nt=nt, alpha=alpha),
        grid=(nt,),
        in_specs=[pl.BlockSpec(pos_blk, lambda i: (i, 0, 0), memory_space=pltpu.SMEM),
                  pl.BlockSpec(pos_blk, lambda i: (jnp.minimum(i + 1, nt - 1), 0, 0),
                               memory_space=pltpu.SMEM),
                  pl.BlockSpec(memory_space=pl.ANY), row,
                  pl.BlockSpec((tr, TOP_K), lambda i: (i, 0)), vec, vec],
        out_specs=[row, row],
        out_shape=[jax.ShapeDtypeStruct((s, d), F32), jax.ShapeDtypeStruct((s, d), BF16)],
        scratch_shapes=[pltpu.VMEM((2, TOP_K, tr, d), F32), pltpu.SemaphoreType.DMA((2,))],
        compiler_params=pltpu.CompilerParams(dimension_semantics=("arbitrary",),
                                             vmem_limit_bytes=_vmem_limit(vmem)),
        name="moe_combine_ln",
    )(pos_t, pos_t, y, x, gate, g.reshape(1, d), b.reshape(1, d))


def _ple(xb, x, p, w_gate, w_proj, layer, bi, *, tm=MM_ROWS, tn=MM_COLS):
    s, d = x.shape
    pdim = p.shape[-1]
    blk, imap = _tile(min(tm, s), min(tn, d))

    def ep(accs, extras, outs, rows):
        proj = jnp.dot(extras[1][rows, :].astype(BF16), extras[2][...].astype(BF16),
                       preferred_element_type=F32)
        out = extras[0][rows, :] + jax.nn.sigmoid(accs[0]) * proj
        outs[0][rows, :] = out
        outs[1][rows, :] = out.astype(BF16)

    extras = [(x, blk, imap),
              (p, (None, None, blk[0], pdim), lambda i, j: (layer, bi, i, 0)),
              (w_proj, (None, pdim, blk[1]), lambda i, j: (layer, 0, j))]
    outs = [((s, d), F32, blk, imap), ((s, d), BF16, blk, imap)]
    return _fused_matmul(xb, [(w_gate, layer, 0)], d, extras, outs, ep, tm=tm, tn=tn, name="ple")


MOE_TILE_ROWS = 512


def kernel(x, p, ln1_g, ln1_b, ln2_g, ln2_b, a_w_in, a_lb_logits, a_out_g, a_w_o, kv_w, b_w_q, b_sinks, b_w_o, rel_bias, ffn_w1, ffn_w3, ffn_w2, moe_router, moe_w1, moe_w3, moe_w2, ple_w_proj, ple_w_gate):
    batch, s, d = x.shape
    depth = p.shape[0]
    n_a = a_w_in.shape[0]
    kv_dim = kv_w.shape[1] // 2
    n_kv = kv_dim // B_HEAD_DIM
    alpha = (2.0 * depth) ** 0.25
    moe_tm = min(MOE_TILE_ROWS, s)
    bias_band = _rel_bias_band(rel_bias.astype(F32))
    lb_logits = a_lb_logits.astype(F32)

    outs = []
    for bi in range(batch):
        xf = x[bi].astype(F32)
        xb = xf.astype(BF16)
        kvh = None
        for i in range(depth):
            if i < n_a:
                q, k, v, log_f, g, chunk_decay = _hgrn_in_proj(xb, a_w_in, lb_logits, i)
                hb = _hgrn_core(q, k, v, log_f, g, a_out_g[i].astype(F32), chunk_decay)
                y = _residual_matmul(hb, a_w_o, i, xf, alpha, name="mixer_out")
            else:
                j = i - n_a
                qh = _head_split_matmul(xb, b_w_q, j, d, name="swa_q")
                hb = _swa_core(qh, kvh, b_sinks[j].astype(F32), bias_band, n_kv)
                y = _residual_matmul(hb, b_w_o, j, xf, alpha, name="mixer_out")
            li = i // 2
            if i % 2 == 0:
                xf, xb = _layer_norm(y, ln1_g[i].astype(F32), ln1_b[i].astype(F32))
                hmid = _swiglu_up(xb, ffn_w1, ffn_w3, li)
                y = _residual_matmul(hmid, ffn_w2, li, xf, alpha, name="ffn_down")
                xf, xb = _layer_norm(y, ln2_g[i].astype(F32), ln2_b[i].astype(F32))
            else:
                xf, xb, gates, idx = _layer_norm(y, ln1_g[i].astype(F32), ln1_b[i].astype(F32),
                                                 moe_router[li].astype(F32))
                row_token, gate, pos, sched, ntiles = _routing_tables(gates, idx, moe_tm)
                xg = _moe_gather(xf, row_token, sched[SCHED_VALID], ntiles, moe_tm)
                hmid = _moe_up(xg, moe_w1, moe_w3, li, sched, moe_tm)
                yg = _moe_down(hmid, moe_w2, li, sched, moe_tm)
                xf, xb = _moe_combine_ln(yg, pos, gate, xf, ln2_g[i].astype(F32),
                                         ln2_b[i].astype(F32), alpha)
            xf, xb = _ple(xb, xf, p, ple_w_gate, ple_w_proj, i, bi)
            if i == n_a - 1:
                kvh = _head_split_matmul(xb, kv_w, None, 2 * kv_dim, name="kv_proj")
        outs.append(xf)
    return jnp.stack(outs, axis=0).astype(x.dtype)
```

```python
import functools

import numpy as np
import jax
import jax.numpy as jnp
from jax import lax
from jax.experimental import pallas as pl
from jax.experimental.pallas import tpu as pltpu

F32 = jnp.float32
BF16 = jnp.bfloat16

V7X_LANES = 128
V7X_SUBLANES = 8
V7X_SCOPED_VMEM_BYTES = 60000 * 1024
COMPILER_SCRATCH_BYTES = 6 * 1024 * 1024

MM_ROWS = 2048
MM_COLS = 256
MM_COLS_BF16_OUT = 512
MM_SUB_ROWS = 128

A_HEAD_DIM = 128
A_CHUNK = 64
A_SUB = 16
A_SAFE_LOG2_DECAY = 100.0
B_HEAD_DIM = 64
B_LOGIT_SCALE = B_HEAD_DIM ** -0.5
assert B_LOGIT_SCALE == 2.0 ** -3
B_BLOCK = 128
WINDOW = 128
REL_BUCKETS = 32
REL_MAX_DISTANCE = 128
TOP_K = 2
LN_EPS = 1e-5
RMS_EPS = 1e-6
LOG2E = 1.4426950408889634


def _vmem_limit(block_bytes):
    return int(min(block_bytes + COMPILER_SCRATCH_BYTES, V7X_SCOPED_VMEM_BYTES))


def _nbytes(shape, dtype):
    return int(np.prod([n for n in shape if n is not None])) * jnp.dtype(dtype).itemsize


def _silu(x):
    return x * jax.nn.sigmoid(x)


def _mm_body(*refs, nw, ne, no, epilogue):
    x_ref = refs[0]
    w_refs = refs[1:1 + nw]
    extra_refs = refs[1 + nw:1 + nw + ne]
    out_refs = refs[1 + nw + ne:1 + nw + ne + no]
    wbs = [w_ref[...].astype(BF16) for w_ref in w_refs]
    tm = x_ref.shape[0]
    sub = min(tm, MM_SUB_ROWS)
    for r0 in range(0, tm, sub):
        rows = slice(r0, r0 + sub)
        xv = x_ref[rows, :]
        prods = [jnp.dot(xv, wb, preferred_element_type=F32) for wb in wbs]
        epilogue(prods, extra_refs, out_refs, rows)


def _fused_matmul(x, ws, n_cols, extras, outs, epilogue, *, tm, tn, name, tk=None, k_block=0):
    m, kdim = x.shape
    tk = tk or kdim
    tm, tn = min(tm, m), min(tn, n_cols)
    assert m % tm == 0 and n_cols % tn == 0 and kdim % tk == 0
    in_specs = [pl.BlockSpec((tm, tk), lambda i, j: (i, k_block))]
    operands = [x]
    vmem = 2 * _nbytes((tm, tk), x.dtype)
    for w, layer, off in ws:
        if layer is None:
            spec = pl.BlockSpec((tk, tn), lambda i, j, off=off: (k_block, j + off))
        else:
            spec = pl.BlockSpec((None, tk, tn), lambda i, j, off=off, layer=layer:
                                (layer, k_block, j + off))
        in_specs.append(spec)
        operands.append(w)
        vmem += 2 * _nbytes((tk, tn), w.dtype) + _nbytes((tk, tn), BF16)
    for arr, bshape, imap in extras:
        in_specs.append(pl.BlockSpec(bshape, imap))
        operands.append(arr)
        vmem += 2 * _nbytes(bshape, arr.dtype)
    out_shapes, out_specs = [], []
    for shape, dtype, bshape, imap in outs:
        out_shapes.append(jax.ShapeDtypeStruct(shape, dtype))
        out_specs.append(pl.BlockSpec(bshape, imap))
        vmem += 2 * _nbytes(bshape, dtype)
    vmem += (2 + len(ws)) * _nbytes((tm, tn), F32)
    body = functools.partial(_mm_body, nw=len(ws), ne=len(extras), no=len(outs), epilogue=epilogue)
    return pl.pallas_call(
        body,
        grid=(m // tm, n_cols // tn),
        in_specs=in_specs,
        out_specs=out_specs,
        out_shape=out_shapes,
        compiler_params=pltpu.CompilerParams(dimension_semantics=("parallel", "parallel"),
                                             vmem_limit_bytes=_vmem_limit(vmem)),
        name=name,
    )(*operands)


def _tile(tm, tn):
    return (tm, tn), (lambda i, j: (i, j))


def _ln_rows(y, g, b):
    mu = jnp.mean(y, axis=-1, keepdims=True)
    yc = y - mu
    var = jnp.mean(yc * yc, axis=-1, keepdims=True)
    return yc * lax.rsqrt(var + LN_EPS) * g + b


def _route_top2(x, w_router):
    logits = jnp.dot(x, w_router, preferred_element_type=F32, precision=lax.Precision.HIGHEST)
    ne = logits.shape[1]
    eid = lax.broadcasted_iota(jnp.int32, logits.shape, 1)
    m1 = jnp.max(logits, axis=-1, keepdims=True)
    i1 = jnp.min(jnp.where(logits == m1, eid, ne), axis=-1, keepdims=True)
    rest = jnp.where(eid == i1, -jnp.inf, logits)
    m2 = jnp.max(rest, axis=-1, keepdims=True)
    i2 = jnp.min(jnp.where(rest == m2, eid, ne), axis=-1, keepdims=True)
    e2 = jnp.exp(m2 - m1)
    w_top1 = 1.0 / (1.0 + e2)
    w_top2 = e2 / (1.0 + e2)
    gates = jnp.where(eid == i1, w_top1, 0.0) + jnp.where(eid == i2, w_top2, 0.0)
    return gates, jnp.where(eid == 0, i1, i2)[:, :TOP_K]


def _ln_body(*refs, route):
    if route:
        y_ref, g_ref, b_ref, wr_ref, o32_ref, o16_ref, gates_ref, idx_ref = refs
    else:
        y_ref, g_ref, b_ref, o32_ref, o16_ref = refs
    out = _ln_rows(y_ref[...], g_ref[...], b_ref[...])
    o32_ref[...] = out
    o16_ref[...] = out.astype(BF16)
    if route:
        gates_ref[...], idx_ref[...] = _route_top2(out, wr_ref[...])


def _layer_norm(y, g, b, w_router=None, *, tr=256):
    s, d = y.shape
    tr = min(tr, s)
    assert s % tr == 0
    row = pl.BlockSpec((tr, d), lambda i: (i, 0))
    vec = pl.BlockSpec((1, d), lambda i: (0, 0))
    in_specs, operands = [row, vec, vec], [y, g.reshape(1, d), b.reshape(1, d)]
    out_specs = [row, row]
    out_shape = [jax.ShapeDtypeStruct((s, d), F32), jax.ShapeDtypeStruct((s, d), BF16)]
    vmem = 2 * (2 * _nbytes((tr, d), F32) + _nbytes((tr, d), BF16)) + 2 * _nbytes((tr, d), F32)
    if w_router is not None:
        ne = w_router.shape[1]
        in_specs.append(pl.BlockSpec((d, ne), lambda i: (0, 0)))
        operands.append(w_router)
        out_specs += [pl.BlockSpec((tr, ne), lambda i: (i, 0)),
                      pl.BlockSpec((tr, TOP_K), lambda i: (i, 0))]
        out_shape += [jax.ShapeDtypeStruct((s, ne), F32), jax.ShapeDtypeStruct((s, TOP_K), jnp.int32)]
        vmem += 2 * _nbytes((d, V7X_LANES), F32) + 4 * _nbytes((tr, d), F32)
    return pl.pallas_call(
        functools.partial(_ln_body, route=w_router is not None),
        grid=(s // tr,),
        in_specs=in_specs,
        out_specs=out_specs,
        out_shape=out_shape,
        compiler_params=pltpu.CompilerParams(dimension_semantics=("parallel",),
                                             vmem_limit_bytes=_vmem_limit(vmem)),
        name="layer_norm_route" if w_router is not None else "layer_norm",
    )(*operands)


def _lower_bound(lbl, layer):
    mx = jnp.max(lbl, axis=0, keepdims=True)
    e = jnp.exp(lbl - mx)
    return jnp.sum(e[:layer + 1], axis=0, keepdims=True) / jnp.sum(e, axis=0, keepdims=True)


def _hgrn_in_proj(xb, w_in, lb_logits, layer, *, tm=MM_ROWS, tn=MM_COLS):
    s, d = xb.shape
    blk, imap = _tile(min(tm, s), min(tn, d))

    def ep_silu(accs, extras, outs, rows):
        outs[0][rows, :] = _silu(accs[0]).astype(BF16)

    def ep_id(accs, extras, outs, rows):
        outs[0][rows, :] = accs[0].astype(BF16)

    def ep_forget(accs, extras, outs, rows):
        lb = _lower_bound(extras[0][...], layer)
        f_raw = accs[0]
        e = jnp.exp(-jnp.abs(f_raw))
        log_sig = jnp.minimum(f_raw, 0.0) - jnp.log(1.0 + e)
        sig_neg = jnp.where(f_raw >= 0.0, e, 1.0) / (1.0 + e)
        a = jnp.log(lb)
        b = jnp.log1p(-lb) + log_sig
        log_f = jnp.maximum(a, b) + jnp.log(1.0 + jnp.exp(-jnp.abs(a - b)))
        outs[0][rows, :] = log_f
        outs[1][rows, :] = ((1.0 - lb) * sig_neg).astype(BF16)
        nrow, ncol = log_f.shape
        outs[2][rows.start // A_CHUNK:rows.stop // A_CHUNK, :] = jnp.sum(
            log_f.reshape(nrow // A_CHUNK, A_CHUNK, ncol), axis=1)

    def call(seg, epilogue, extras, out_dtypes, name, more_outs=(), cols=tn):
        cblk, cimap = _tile(blk[0], min(cols, d))
        outs = [((s, d), dt, cblk, cimap) for dt in out_dtypes] + list(more_outs)
        return _fused_matmul(xb, [(w_in, layer, seg * (d // cblk[1]))], d, extras, outs, epilogue,
                             tm=tm, tn=cols, name=name)

    nslot = lb_logits.shape[0]
    lb_extra = [(lb_logits, (nslot, blk[1]), lambda i, j: (0, j))]
    decay_out = ((s // A_CHUNK, d), F32, (blk[0] // A_CHUNK, blk[1]), imap)
    (q,) = call(0, ep_silu, [], [BF16], "hgrn_q", cols=MM_COLS_BF16_OUT)
    log_f, k, chunk_decay = call(1, ep_forget, lb_extra, [F32, BF16], "hgrn_f",
                                 more_outs=[decay_out])
    (v,) = call(2, ep_id, [], [BF16], "hgrn_v", cols=MM_COLS_BF16_OUT)
    (g,) = call(3, ep_silu, [], [BF16], "hgrn_g", cols=MM_COLS_BF16_OUT)
    return q, k, v, log_f, g, chunk_decay


def _chunk_cumsum(lf_ref, tc):
    x = lf_ref[...]
    row_in_chunk = lax.broadcasted_iota(jnp.int32, x.shape, 0) % A_CHUNK
    shift = 1
    while shift < A_CHUNK:
        x = x + jnp.where(row_in_chunk >= shift, pltpu.roll(x, shift, axis=0), 0.0)
        shift *= 2
    return x


def _hgrn_body(bounded_ref, q_ref, k_ref, v_ref, lf_ref, g_ref, og_ref, o_ref, st_ref, *, tc):
    @pl.when(pl.program_id(1) == 0)
    def _():
        st_ref[...] = jnp.zeros_like(st_ref)

    out_g = og_ref[...]
    half = A_SUB // 2
    lane = lax.broadcasted_iota(jnp.int32, (half, A_HEAD_DIM), 1)
    hrow = lax.broadcasted_iota(jnp.int32, (half, A_HEAD_DIM), 0)
    nsub = A_CHUNK // A_SUB
    nchunk = tc // A_CHUNK

    def finish(rows, o):
        o = o * lax.rsqrt(jnp.mean(o * o, axis=-1, keepdims=True) + RMS_EPS)
        o_ref[rows, :] = (o * out_g * g_ref[rows, :].astype(F32)).astype(BF16)

    def as_column(row):
        return jnp.transpose(jnp.broadcast_to(row, (V7X_SUBLANES, row.shape[1])))[:, :1]

    def inter_chunk(ci, cum, st_t):
        rows = slice(ci * A_CHUNK, (ci + 1) * A_CHUNK)
        c2 = cum[rows] * LOG2E
        qc = q_ref[rows, :].astype(F32)
        kc = k_ref[rows, :].astype(F32)
        vc = v_ref[rows, :]
        last = c2[A_CHUNK - 1:A_CHUNK]
        qe = (qc * jnp.exp2(c2)).astype(BF16)
        o_inter = lax.dot_general(qe, st_t.astype(BF16), (((1,), (1,)), ((), ())),
                                  preferred_element_type=F32)
        kd = (kc * jnp.exp2(last - c2)).astype(BF16)
        upd = lax.dot_general(vc, kd, (((0,), (0,)), ((), ())), preferred_element_type=F32)
        return rows, c2, qc, kc, vc, qe, o_inter, st_t * jnp.exp2(last) + upd

    def run_factorised():
        c2 = _chunk_cumsum(lf_ref, tc) * LOG2E
        qe = (q_ref[...].astype(F32) * jnp.exp2(c2)).astype(BF16)
        kb32 = k_ref[...].astype(F32) * jnp.exp2(-c2)
        kb = kb32.astype(BF16)
        t_i = lax.broadcasted_iota(jnp.int32, (A_CHUNK, A_CHUNK), 0)
        s_i = lax.broadcasted_iota(jnp.int32, (A_CHUNK, A_CHUNK), 1)
        causal = t_i >= s_i
        st = st_ref[...]
        outs = []
        for ci in range(nchunk):
            rows = slice(ci * A_CHUNK, (ci + 1) * A_CHUNK)
            vc = v_ref[rows, :]
            grow = jnp.exp2(c2[(ci + 1) * A_CHUNK - 1:(ci + 1) * A_CHUNK])
            sc = lax.dot_general(qe[rows], kb[rows], (((1,), (1,)), ((), ())),
                                 preferred_element_type=F32)
            scores = jnp.where(causal, sc, 0.0).astype(BF16)
            outs.append(jnp.dot(jnp.concatenate([qe[rows], scores], axis=1),
                                jnp.concatenate([st.astype(BF16), vc], axis=0),
                                preferred_element_type=F32))
            kd = (kb32[rows] * grow).astype(BF16)
            upd = lax.dot_general(kd, vc, (((0,), (0,)), ((), ())), preferred_element_type=F32)
            st = st * as_column(grow) + upd
        st_ref[...] = st
        finish(slice(0, tc), jnp.concatenate(outs, axis=0))

    def direct_chunk(ci, cum, st):
        rows, c2, qc, kc, vc, qe, o_inter, st = inter_chunk(ci, cum, st)

        p_rows = []
        k_blocks, prev_ref = [], None
        for si in range(nsub):
            lo_r = si * A_SUB
            cs, qs, ks = c2[lo_r:lo_r + A_SUB], qc[lo_r:lo_r + A_SUB], kc[lo_r:lo_r + A_SUB]
            off = None
            if si > 0:
                ref = c2[lo_r - 1:lo_r]
                if prev_ref is not None:
                    rebase = jnp.exp2(ref - prev_ref)
                    k_blocks = [kb * rebase for kb in k_blocks]
                k_blocks.append(kc[lo_r - A_SUB:lo_r] * jnp.exp2(ref - c2[lo_r - A_SUB:lo_r]))
                prev_ref = ref
                k_t = jnp.concatenate(
                    k_blocks + [jnp.zeros((A_CHUNK - lo_r, A_HEAD_DIM), F32)], axis=0)
                q_t = (qs * jnp.exp2(cs - ref)).astype(BF16)
                off = lax.dot_general(q_t, k_t.astype(BF16), (((1,), (1,)), ((), ())),
                                      preferred_element_type=F32)
            c_h = (cs[:half], cs[half:])
            q_h = (qs[:half], qs[half:])
            p_h = [jnp.zeros((half, A_HEAD_DIM), F32), jnp.zeros((half, A_HEAD_DIM), F32)]
            for sj in range(A_SUB):
                for hh in range(sj // half, 2):
                    decay = jnp.exp2(c_h[hh] - cs[sj:sj + 1])
                    colv = jnp.sum(q_h[hh] * (ks[sj:sj + 1] * decay), axis=1, keepdims=True)
                    sel = lane == lo_r + sj
                    if sj // half == hh:
                        sel = sel & (hrow >= sj - hh * half)
                    p_h[hh] = jnp.where(sel, colv, p_h[hh])
            for hh in range(2):
                ph = p_h[hh][:, :A_CHUNK]
                p_rows.append(ph if off is None else ph + off[hh * half:(hh + 1) * half])
        scores = jnp.concatenate(p_rows, axis=0).astype(BF16)
        finish(rows, o_inter + jnp.dot(scores, vc, preferred_element_type=F32))
        return st

    def run_direct():
        cum = _chunk_cumsum(lf_ref, tc)
        st_t = st_ref[...].T
        for ci in range(nchunk):
            st_t = direct_chunk(ci, cum, st_t)
        st_ref[...] = st_t.T

    bounded = bounded_ref[pl.program_id(0), pl.program_id(1)] > 0
    pl.when(bounded)(run_factorised)
    pl.when(jnp.logical_not(bounded))(run_direct)


def _hgrn_core(q, k, v, log_f, g, out_g, chunk_decay, *, tc=2048):
    s, d = q.shape
    tc = min(tc, s)
    assert s % tc == 0 and tc % A_CHUNK == 0 and d % A_HEAD_DIM == 0
    nhead, nblk = d // A_HEAD_DIM, s // tc
    worst = jnp.min(chunk_decay.reshape(nblk, tc // A_CHUNK, nhead, A_HEAD_DIM), axis=(1, 3))
    bounded = (worst.T * LOG2E >= -A_SAFE_LOG2_DECAY).astype(jnp.int32)
    blk = pl.BlockSpec((tc, A_HEAD_DIM), lambda h, c, flags: (c, h))
    vmem = 2 * (4 * _nbytes((tc, A_HEAD_DIM), BF16) + 2 * _nbytes((tc, A_HEAD_DIM), F32)) \
        + 12 * _nbytes((tc, A_HEAD_DIM), F32)
    grid_spec = pltpu.PrefetchScalarGridSpec(
        num_scalar_prefetch=1,
        grid=(nhead, nblk),
        in_specs=[blk, blk, blk, blk, blk,
                  pl.BlockSpec((1, A_HEAD_DIM), lambda h, c, flags: (0, h))],
        out_specs=blk,
        scratch_shapes=[pltpu.VMEM((A_HEAD_DIM, A_HEAD_DIM), F32)],
    )
    return pl.pallas_call(
        functools.partial(_hgrn_body, tc=tc),
        grid_spec=grid_spec,
        out_shape=jax.ShapeDtypeStruct((s, d), BF16),
        compiler_params=pltpu.CompilerParams(dimension_semantics=("parallel", "arbitrary"),
                                             vmem_limit_bytes=_vmem_limit(vmem)),
        name="hgrn_core",
    )(bounded, q, k, v, log_f, g, out_g.reshape(1, d))


def _residual_matmul(hb, w, layer, resid, alpha, *, name, tm=MM_ROWS, tn=MM_COLS):
    s, d = resid.shape
    kdim = hb.shape[1]
    tk = min(kdim, d)
    blk, imap = _tile(min(tm, s), min(tn, d))
    y = resid
    for kb in range(kdim // tk):
        scale = alpha if kb == 0 else 1.0

        def ep(accs, extras, outs, rows, scale=scale):
            outs[0][rows, :] = scale * extras[0][rows, :] + accs[0]

        (y,) = _fused_matmul(hb, [(w, layer, 0)], d, [(y, blk, imap)], [((s, d), F32, blk, imap)],
                             ep, tm=tm, tn=tn, tk=tk, k_block=kb, name=name)
    return y


def _rel_bucket_band():
    i = np.arange(B_BLOCK)[:, None]
    j = np.arange(2 * B_BLOCK)[None, :]
    d = np.clip(B_BLOCK + i - j, 0, None)
    max_exact = REL_BUCKETS // 2
    large = max_exact + (np.log(np.maximum(d, 1) / max_exact)
                         / np.log(REL_MAX_DISTANCE / max_exact)
                         * (REL_BUCKETS - max_exact)).astype(np.int32)
    large = np.minimum(large, REL_BUCKETS - 1)
    return np.where(d < max_exact, d, large).astype(np.int32)


def _bias_body(rb_ref, bucket_ref, o_ref, *, heads_per_step):
    h0 = pl.program_id(0) * heads_per_step
    bucket = bucket_ref[...]
    qi = lax.broadcasted_iota(jnp.int32, bucket.shape, 0)
    kj = lax.broadcasted_iota(jnp.int32, bucket.shape, 1)
    dist = B_BLOCK + qi - kj
    in_window = (dist >= 0) & (dist < WINDOW)
    for hh in range(heads_per_step):
        acc = jnp.zeros(bucket.shape, F32)
        for b in range(REL_BUCKETS):
            acc = jnp.where(bucket == b, rb_ref[b, h0 + hh], acc)
        o_ref[hh] = jnp.where(in_window, acc, -jnp.inf)


def _rel_bias_band(rel_bias, *, heads_per_step=8):
    nh = rel_bias.shape[1]
    bucket = jnp.asarray(_rel_bucket_band())
    return pl.pallas_call(
        functools.partial(_bias_body, heads_per_step=heads_per_step),
        grid=(nh // heads_per_step,),
        in_specs=[pl.BlockSpec(memory_space=pltpu.SMEM),
                  pl.BlockSpec((B_BLOCK, 2 * B_BLOCK), lambda i: (0, 0))],
        out_specs=pl.BlockSpec((heads_per_step, B_BLOCK, 2 * B_BLOCK), lambda i: (i, 0, 0)),
        out_shape=jax.ShapeDtypeStruct((nh, B_BLOCK, 2 * B_BLOCK), F32),
        compiler_params=pltpu.CompilerParams(dimension_semantics=("parallel",)),
        name="rel_bias_band",
    )(rel_bias, bucket)


def _head_split_matmul(xb, w, layer, n_cols, *, name, tm=MM_ROWS, tn=MM_COLS):
    s, kdim = xb.shape
    tm, tn = min(tm, s), min(tn, n_cols)
    hpt = tn // B_HEAD_DIM

    def ep(accs, extras, outs, rows):
        for c in range(hpt):
            outs[0][c, rows, :] = accs[0][:, c * B_HEAD_DIM:(c + 1) * B_HEAD_DIM].astype(BF16)

    outs = [((n_cols // B_HEAD_DIM, s, B_HEAD_DIM), BF16, (hpt, tm, B_HEAD_DIM),
             lambda i, j: (j, i, 0))]
    (out,) = _fused_matmul(xb, [(w, layer, 0)], n_cols, [], outs, ep, tm=tm, tn=tn, name=name)
    return out


def _attn_body(sink_ref, q_ref, kp_ref, kc_ref, vp_ref, vc_ref, bias_ref, o_ref, *, group, qb):
    h = pl.program_id(0)
    m_blk = pl.program_id(1)
    c = B_BLOCK
    k_all = jnp.concatenate([kp_ref[0], kc_ref[0]], axis=0)
    v_all = jnp.concatenate([vp_ref[0], vc_ref[0]], axis=0)
    kj = lax.broadcasted_iota(jnp.int32, (c, 2 * c), 1)
    for b in range(qb):
        q = q_ref[:, b * c:(b + 1) * c, :].reshape(group * c, B_HEAD_DIM) * B_LOGIT_SCALE
        kb = k_all[b * c:(b + 2) * c]
        vb = v_all[b * c:(b + 2) * c]
        logits = lax.dot_general(q, kb, (((1,), (1,)), ((), ())), preferred_element_type=F32)
        probs = []
        for gi in range(group):
            lg = logits[gi * c:(gi + 1) * c] + bias_ref[gi]
            if b == 0:
                lg = jnp.where((m_blk > 0) | (kj >= c), lg, -jnp.inf)
            sink = sink_ref[h * group + gi]
            m = jnp.maximum(jnp.max(lg, axis=-1, keepdims=True), sink)
            e = jnp.exp(lg - m)
            denom = jnp.sum(e, axis=-1, keepdims=True) + jnp.exp(sink - m)
            probs.append((e / denom).astype(BF16))
        pv = jnp.dot(jnp.concatenate(probs, axis=0), vb, preferred_element_type=F32)
        o_ref[b * c:(b + 1) * c, :] = jnp.concatenate(
            [pv[gi * c:(gi + 1) * c] for gi in range(group)], axis=1).astype(BF16)


def _swa_core(qh, kvh, sinks, bias, n_kv, *, qb=4):
    nq, s, hd = qh.shape
    group = nq // n_kv
    nb = s // B_BLOCK
    qb = min(qb, nb)
    assert nb % qb == 0
    prev = lambda m: jnp.maximum(m * qb - 1, 0)
    prev_blk = (1, B_BLOCK, hd)
    cur_blk = (1, qb * B_BLOCK, hd)
    in_specs = [
        pl.BlockSpec(memory_space=pltpu.SMEM),
        pl.BlockSpec((group, qb * B_BLOCK, hd), lambda h, m: (h, m, 0)),
        pl.BlockSpec(prev_blk, lambda h, m: (h, prev(m), 0)),
        pl.BlockSpec(cur_blk, lambda h, m: (h, m, 0)),
        pl.BlockSpec(prev_blk, lambda h, m: (h + n_kv, prev(m), 0)),
        pl.BlockSpec(cur_blk, lambda h, m: (h + n_kv, m, 0)),
        pl.BlockSpec((group, B_BLOCK, 2 * B_BLOCK), lambda h, m: (h, 0, 0)),
    ]
    vmem = 2 * (_nbytes((group, B_BLOCK, 2 * B_BLOCK), F32)
                + (group + 3) * _nbytes((qb * B_BLOCK, V7X_LANES), BF16)
                + _nbytes((qb * B_BLOCK, group * hd), BF16)) \
        + 6 * qb * _nbytes((group * B_BLOCK, 2 * B_BLOCK), F32)
    return pl.pallas_call(
        functools.partial(_attn_body, group=group, qb=qb),
        grid=(n_kv, nb // qb),
        in_specs=in_specs,
        out_specs=pl.BlockSpec((qb * B_BLOCK, group * hd), lambda h, m: (m, h)),
        out_shape=jax.ShapeDtypeStruct((s, nq * hd), BF16),
        compiler_params=pltpu.CompilerParams(dimension_semantics=("parallel", "parallel"),
                                             vmem_limit_bytes=_vmem_limit(vmem)),
        name="swa_core",
    )(sinks, qh, kvh, kvh, kvh, kvh, bias)


def _swiglu_up(xb, w1, w3, layer, *, tm=MM_ROWS, tn=MM_COLS):
    s, d = xb.shape
    f = w1.shape[2]
    blk, imap = _tile(min(tm, s), min(tn, f))

    def ep(accs, extras, outs, rows):
        outs[0][rows, :] = (_silu(accs[0]) * accs[1]).astype(BF16)

    (h,) = _fused_matmul(xb, [(w1, layer, 0), (w3, layer, 0)], f, [],
                         [((s, f), BF16, blk, imap)], ep, tm=tm, tn=tn, name="swiglu_up")
    return h


def _routing_tables(gates, idx, tm):
    s, ne = gates.shape
    n_assign = s * TOP_K
    ntiles = n_assign // tm + ne
    e_flat = idx.reshape(n_assign)
    onehot = (e_flat[:, None] == jnp.arange(ne, dtype=jnp.int32)[None, :]).astype(jnp.int32)
    rank = jnp.take_along_axis(jnp.cumsum(onehot, axis=0) - onehot, e_flat[:, None], axis=1)[:, 0]
    counts = jnp.sum(onehot, axis=0)
    padded = ((counts + tm - 1) // tm) * tm
    ends = jnp.cumsum(padded)
    pos = (ends - padded)[e_flat] + rank
    token = jnp.arange(n_assign, dtype=jnp.int32) // TOP_K
    gate = jnp.take_along_axis(gates, idx, axis=1)
    row_token = jnp.zeros((ntiles * tm,), jnp.int32).at[pos].set(token)
    tile_start = jnp.arange(ntiles, dtype=jnp.int32) * tm
    tile_expert = jnp.minimum(jnp.searchsorted(ends, tile_start, side="right"), ne - 1).astype(jnp.int32)
    tile_valid = tile_start < ends[-1]
    first = tile_valid & jnp.concatenate(
        [jnp.ones((1,), bool), tile_expert[1:] != tile_expert[:-1]])
    slot = (jnp.cumsum(first.astype(jnp.int32)) - 1) % 2
    eid = jnp.arange(ne, dtype=jnp.int32)
    later = (counts > 0)[None, :] & (eid[None, :] > eid[:, None])
    next_used = jnp.min(jnp.where(later, eid[None, :], ne), axis=1)
    next_used = jnp.where(next_used == ne, -1, next_used)
    sched = jnp.stack([tile_expert, tile_valid.astype(jnp.int32), first.astype(jnp.int32),
                       slot.astype(jnp.int32), next_used[tile_expert]]).astype(jnp.int32)
    return row_token, gate, pos.reshape(s, TOP_K).astype(jnp.int32), sched, ntiles


GATHER_ISSUE_UNROLL = 8


def _gather_body(tv_ref, tok_ref, tok_next_ref, x_hbm, o_ref, buf, sem, *, tm, ntiles):
    i = pl.program_id(0)
    slot = i % 2

    def row_copy(tile_tok_ref, slot_, r):
        return pltpu.make_async_copy(x_hbm.at[pl.ds(tile_tok_ref[0, 0, r], 1)],
                                     buf.at[slot_, pl.ds(r, 1)], sem.at[slot_])

    def start_tile(tile_tok_ref, slot_):
        def body(r, carry):
            row_copy(tile_tok_ref, slot_, r).start()
            return carry
        lax.fori_loop(0, tm, body, 0, unroll=GATHER_ISSUE_UNROLL)

    @pl.when((i == 0) & (tv_ref[0] > 0))
    def _():
        start_tile(tok_ref, slot)

    @pl.when((i + 1 < ntiles) & (tv_ref[jnp.minimum(i + 1, ntiles - 1)] > 0))
    def _():
        start_tile(tok_next_ref, 1 - slot)

    @pl.when(tv_ref[i] > 0)
    def _():
        def wait_body(r, carry):
            row_copy(tok_ref, slot, r).wait()
            return carry
        lax.fori_loop(0, tm, wait_body, 0, unroll=GATHER_ISSUE_UNROLL)
        o_ref[...] = buf[slot].astype(BF16)

    @pl.when(tv_ref[i] == 0)
    def _():
        o_ref[...] = jnp.zeros_like(o_ref)


def _moe_gather(x, row_token, tile_valid, ntiles, tm):
    s, d = x.shape
    vmem = 2 * _nbytes((tm, d), F32) + 2 * _nbytes((tm, d), BF16) + _nbytes((tm, d), F32)
    tok = row_token.reshape(ntiles, 1, tm)
    grid_spec = pltpu.PrefetchScalarGridSpec(
        num_scalar_prefetch=1,
        grid=(ntiles,),
        in_specs=[pl.BlockSpec((1, 1, tm), lambda i, tv: (i, 0, 0), memory_space=pltpu.SMEM),
                  pl.BlockSpec((1, 1, tm), lambda i, tv: (jnp.minimum(i + 1, ntiles - 1), 0, 0),
                               memory_space=pltpu.SMEM),
                  pl.BlockSpec(memory_space=pl.ANY)],
        out_specs=pl.BlockSpec((tm, d), lambda i, tv: (i, 0)),
        scratch_shapes=[pltpu.VMEM((2, tm, d), F32), pltpu.SemaphoreType.DMA((2,))],
    )
    return pl.pallas_call(
        functools.partial(_gather_body, tm=tm, ntiles=ntiles),
        grid_spec=grid_spec,
        out_shape=jax.ShapeDtypeStruct((ntiles * tm, d), BF16),
        compiler_params=pltpu.CompilerParams(dimension_semantics=("arbitrary",),
                                             vmem_limit_bytes=_vmem_limit(vmem)),
        name="moe_gather",
    )(tile_valid, tok, tok, x)


SCHED_EXPERT, SCHED_VALID, SCHED_FIRST, SCHED_SLOT, SCHED_NEXT = range(5)


def _grouped_body(sched_ref, x_ref, *rest, nw, layer, tcol, epilogue):
    w_hbm, o_ref, wbufs, sem = rest[:nw], rest[nw], rest[nw + 1:2 * nw + 1], rest[2 * nw + 1]
    j, i = pl.program_id(0), pl.program_id(1)
    expert, slot, nxt = sched_ref[SCHED_EXPERT, i], sched_ref[SCHED_SLOT, i], sched_ref[SCHED_NEXT, i]
    first = sched_ref[SCHED_FIRST, i] > 0
    col0 = pl.multiple_of(j * tcol, tcol)

    def copies(e, s):
        return [pltpu.make_async_copy(w.at[layer, e, :, pl.ds(col0, tcol)], wb.at[s], sem.at[s])
                for w, wb in zip(w_hbm, wbufs)]

    @pl.when(i == 0)
    def _():
        for c in copies(expert, slot):
            c.start()

    @pl.when(first & (nxt >= 0))
    def _():
        for c in copies(nxt, 1 - slot):
            c.start()

    @pl.when(first)
    def _():
        for c in copies(expert, slot):
            c.wait()

    @pl.when(sched_ref[SCHED_VALID, i] > 0)
    def _():
        wbs = [wb[slot].astype(BF16) for wb in wbufs]
        sub = min(x_ref.shape[0], MM_SUB_ROWS)
        for r0 in range(0, x_ref.shape[0], sub):
            xv = x_ref[r0:r0 + sub, :]
            o_ref[r0:r0 + sub, :] = epilogue(
                [jnp.dot(xv, wb, preferred_element_type=F32) for wb in wbs])

    @pl.when(sched_ref[SCHED_VALID, i] == 0)
    def _():
        o_ref[...] = jnp.zeros_like(o_ref)


def _grouped_matmul(x, ws, layer, sched, tm, tcol, out_dtype, epilogue, name):
    rows, kdim = x.shape
    n = ws[0].shape[3]
    tcol = min(tcol, n)
    ntiles = rows // tm
    vmem = 2 * _nbytes((tm, kdim), BF16) + 2 * _nbytes((tm, tcol), out_dtype) \
        + len(ws) * (2 * _nbytes((kdim, tcol), F32) + _nbytes((kdim, tcol), BF16)) \
        + (1 + len(ws)) * _nbytes((tm, tcol), F32)
    grid_spec = pltpu.PrefetchScalarGridSpec(
        num_scalar_prefetch=1,
        grid=(n // tcol, ntiles),
        in_specs=[pl.BlockSpec((tm, kdim), lambda j, i, sched: (i, 0))]
        + [pl.BlockSpec(memory_space=pl.ANY)] * len(ws),
        out_specs=pl.BlockSpec((tm, tcol), lambda j, i, sched: (i, j)),
        scratch_shapes=[pltpu.VMEM((2, kdim, tcol), F32) for _ in ws]
        + [pltpu.SemaphoreType.DMA((2,))],
    )
    return pl.pallas_call(
        functools.partial(_grouped_body, nw=len(ws), layer=layer, tcol=tcol, epilogue=epilogue),
        grid_spec=grid_spec,
        out_shape=jax.ShapeDtypeStruct((rows, n), out_dtype),
        compiler_params=pltpu.CompilerParams(dimension_semantics=("arbitrary", "arbitrary"),
                                             vmem_limit_bytes=_vmem_limit(vmem)),
        name=name,
    )(sched, x, *ws)


def _moe_up(xg, w1, w3, layer, sched, tm, *, tf=512):
    return _grouped_matmul(xg, [w1, w3], layer, sched, tm, tf, BF16,
                           lambda p: (_silu(p[0]) * p[1]).astype(BF16), "moe_up")


def _moe_down(h, w2, layer, sched, tm, *, tn=2048):
    return _grouped_matmul(h, [w2], layer, sched, tm, tn, F32, lambda p: p[0], "moe_down")


def _combine_body(pos_ref, pos_next_ref, y_hbm, x_ref, gate_ref, g_ref, b_ref, o32_ref, o16_ref,
                  buf, sem, *, tr, nt, alpha):
    i = pl.program_id(0)
    slot = i % 2

    def row_copy(tile_pos_ref, slot_, r, kk):
        return pltpu.make_async_copy(y_hbm.at[pl.ds(tile_pos_ref[0, kk, r], 1)],
                                     buf.at[slot_, kk, pl.ds(r, 1)], sem.at[slot_])

    def start_tile(tile_pos_ref, slot_):
        def body(r, carry):
            for kk in range(TOP_K):
                row_copy(tile_pos_ref, slot_, r, kk).start()
            return carry
        lax.fori_loop(0, tr, body, 0, unroll=GATHER_ISSUE_UNROLL)

    @pl.when(i == 0)
    def _():
        start_tile(pos_ref, slot)

    @pl.when(i + 1 < nt)
    def _():
        start_tile(pos_next_ref, 1 - slot)

    def wait_body(r, carry):
        for kk in range(TOP_K):
            row_copy(pos_ref, slot, r, kk).wait()
        return carry
    lax.fori_loop(0, tr, wait_body, 0, unroll=GATHER_ISSUE_UNROLL)

    gate = gate_ref[...]
    f = buf[slot, 0] * gate[:, 0:1]
    for kk in range(1, TOP_K):
        f = f + buf[slot, kk] * gate[:, kk:kk + 1]
    out = _ln_rows(alpha * x_ref[...] + f, g_ref[...], b_ref[...])
    o32_ref[...] = out
    o16_ref[...] = out.astype(BF16)


def _moe_combine_ln(y, pos, gate, x, g, b, alpha, *, tr=256):
    s, d = x.shape
    tr = min(tr, s)
    nt = s // tr
    pos_t = pos.reshape(nt, tr, TOP_K).transpose(0, 2, 1)
    row = pl.BlockSpec((tr, d), lambda i: (i, 0))
    vec = pl.BlockSpec((1, d), lambda i: (0, 0))
    vmem = 2 * TOP_K * _nbytes((tr, d), F32) \
        + 2 * (2 * _nbytes((tr, d), F32) + _nbytes((tr, d), BF16)) + 3 * _nbytes((tr, d), F32)
    pos_blk = (1, TOP_K, tr)
    return pl.pallas_call(
        functools.partial(_combine_body, tr=tr, nt=nt, alpha=alpha),
        grid=(nt,),
        in_specs=[pl.BlockSpec(pos_blk, lambda i: (i, 0, 0), memory_space=pltpu.SMEM),
                  pl.BlockSpec(pos_blk, lambda i: (jnp.minimum(i + 1, nt - 1), 0, 0),
                               memory_space=pltpu.SMEM),
                  pl.BlockSpec(memory_space=pl.ANY), row,
                  pl.BlockSpec((tr, TOP_K), lambda i: (i, 0)), vec, vec],
        out_specs=[row, row],
        out_shape=[jax.ShapeDtypeStruct((s, d), F32), jax.ShapeDtypeStruct((s, d), BF16)],
        scratch_shapes=[pltpu.VMEM((2, TOP_K, tr, d), F32), pltpu.SemaphoreType.DMA((2,))],
        compiler_params=pltpu.CompilerParams(dimension_semantics=("arbitrary",),
                                             vmem_limit_bytes=_vmem_limit(vmem)),
        name="moe_combine_ln",
    )(pos_t, pos_t, y, x, gate, g.reshape(1, d), b.reshape(1, d))


def _ple(xb, x, p, w_gate, w_proj, layer, bi, *, tm=MM_ROWS, tn=MM_COLS):
    s, d = x.shape
    pdim = p.shape[-1]
    blk, imap = _tile(min(tm, s), min(tn, d))

    def ep(accs, extras, outs, rows):
        proj = jnp.dot(extras[1][rows, :].astype(BF16), extras[2][...].astype(BF16),
                       preferred_element_type=F32)
        out = extras[0][rows, :] + jax.nn.sigmoid(accs[0]) * proj
        outs[0][rows, :] = out
        outs[1][rows, :] = out.astype(BF16)

    extras = [(x, blk, imap),
              (p, (None, None, blk[0], pdim), lambda i, j: (layer, bi, i, 0)),
              (w_proj, (None, pdim, blk[1]), lambda i, j: (layer, 0, j))]
    outs = [((s, d), F32, blk, imap), ((s, d), BF16, blk, imap)]
    return _fused_matmul(xb, [(w_gate, layer, 0)], d, extras, outs, ep, tm=tm, tn=tn, name="ple")


MOE_TILE_ROWS = 512


def kernel(x, p, ln1_g, ln1_b, ln2_g, ln2_b, a_w_in, a_lb_logits, a_out_g, a_w_o, kv_w, b_w_q, b_sinks, b_w_o, rel_bias, ffn_w1, ffn_w3, ffn_w2, moe_router, moe_w1, moe_w3, moe_w2, ple_w_proj, ple_w_gate):
    batch, s, d = x.shape
    depth = p.shape[0]
    n_a = a_w_in.shape[0]
    kv_dim = kv_w.shape[1] // 2
    n_kv = kv_dim // B_HEAD_DIM
    alpha = (2.0 * depth) ** 0.25
    moe_tm = min(MOE_TILE_ROWS, s)
    bias_band = _rel_bias_band(rel_bias.astype(F32))
    lb_logits = a_lb_logits.astype(F32)

    outs = []
    for bi in range(batch):
        xf = x[bi].astype(F32)
        xb = xf.astype(BF16)
        kvh = None
        for i in range(depth):
            if i < n_a:
                q, k, v, log_f, g, chunk_decay = _hgrn_in_proj(xb, a_w_in, lb_logits, i)
                hb = _hgrn_core(q, k, v, log_f, g, a_out_g[i].astype(F32), chunk_decay)
                y = _residual_matmul(hb, a_w_o, i, xf, alpha, name="mixer_out")
            else:
                j = i - n_a
                qh = _head_split_matmul(xb, b_w_q, j, d, name="swa_q")
                hb = _swa_core(qh, kvh, b_sinks[j].astype(F32), bias_band, n_kv)
                y = _residual_matmul(hb, b_w_o, j, xf, alpha, name="mixer_out")
            li = i // 2
            if i % 2 == 0:
                xf, xb = _layer_norm(y, ln1_g[i].astype(F32), ln1_b[i].astype(F32))
                hmid = _swiglu_up(xb, ffn_w1, ffn_w3, li)
                y = _residual_matmul(hmid, ffn_w2, li, xf, alpha, name="ffn_down")
                xf, xb = _layer_norm(y, ln2_g[i].astype(F32), ln2_b[i].astype(F32))
            else:
                xf, xb, gates, idx = _layer_norm(y, ln1_g[i].astype(F32), ln1_b[i].astype(F32),
                                                 moe_router[li].astype(F32))
                row_token, gate, pos, sched, ntiles = _routing_tables(gates, idx, moe_tm)
                xg = _moe_gather(xf, row_token, sched[SCHED_VALID], ntiles, moe_tm)
                hmid = _moe_up(xg, moe_w1, moe_w3, li, sched, moe_tm)
                yg = _moe_down(hmid, moe_w2, li, sched, moe_tm)
                xf, xb = _moe_combine_ln(yg, pos, gate, xf, ln2_g[i].astype(F32),
                                         ln2_b[i].astype(F32), alpha)
            xf, xb = _ple(xb, xf, p, ple_w_gate, ple_w_proj, i, bi)
            if i == n_a - 1:
                kvh = _head_split_matmul(xb, kv_w, None, 2 * kv_dim, name="kv_proj")
        outs.append(xf)
    return jnp.stack(outs, axis=0).astype(x.dtype)
```

```python
import functools

import numpy as np
import jax
import jax.numpy as jnp
from jax import lax
from jax.experimental import pallas as pl
from jax.experimental.pallas import tpu as pltpu

F32 = jnp.float32
BF16 = jnp.bfloat16

V7X_LANES = 128
V7X_SUBLANES = 8
V7X_SCOPED_VMEM_BYTES = 60000 * 1024
COMPILER_SCRATCH_BYTES = 6 * 1024 * 1024

MM_ROWS = 2048
MM_COLS = 256
MM_COLS_BF16_OUT = 512
MM_SUB_ROWS = 128

A_HEAD_DIM = 128
A_CHUNK = 64
A_SUB = 16
A_SAFE_LOG2_DECAY = 100.0
B_HEAD_DIM = 64
B_LOGIT_SCALE = B_HEAD_DIM ** -0.5
assert B_LOGIT_SCALE == 2.0 ** -3
B_BLOCK = 128
WINDOW = 128
REL_BUCKETS = 32
REL_MAX_DISTANCE = 128
TOP_K = 2
LN_EPS = 1e-5
RMS_EPS = 1e-6
LOG2E = 1.4426950408889634


def _vmem_limit(block_bytes):
    return int(min(block_bytes + COMPILER_SCRATCH_BYTES, V7X_SCOPED_VMEM_BYTES))


def _nbytes(shape, dtype):
    return int(np.prod([n for n in shape if n is not None])) * jnp.dtype(dtype).itemsize


def _silu(x):
    return x * jax.nn.sigmoid(x)


def _mm_body(*refs, nw, ne, no, epilogue):
    x_ref = refs[0]
    w_refs = refs[1:1 + nw]
    extra_refs = refs[1 + nw:1 + nw + ne]
    out_refs = refs[1 + nw + ne:1 + nw + ne + no]
    wbs = [w_ref[...].astype(BF16) for w_ref in w_refs]
    tm = x_ref.shape[0]
    sub = min(tm, MM_SUB_ROWS)
    for r0 in range(0, tm, sub):
        rows = slice(r0, r0 + sub)
        xv = x_ref[rows, :]
        prods = [jnp.dot(xv, wb, preferred_element_type=F32) for wb in wbs]
        epilogue(prods, extra_refs, out_refs, rows)


def _fused_matmul(x, ws, n_cols, extras, outs, epilogue, *, tm, tn, name, tk=None, k_block=0):
    m, kdim = x.shape
    tk = tk or kdim
    tm, tn = min(tm, m), min(tn, n_cols)
    assert m % tm == 0 and n_cols % tn == 0 and kdim % tk == 0
    in_specs = [pl.BlockSpec((tm, tk), lambda i, j: (i, k_block))]
    operands = [x]
    vmem = 2 * _nbytes((tm, tk), x.dtype)
    for w, layer, off in ws:
        if layer is None:
            spec = pl.BlockSpec((tk, tn), lambda i, j, off=off: (k_block, j + off))
        else:
            spec = pl.BlockSpec((None, tk, tn), lambda i, j, off=off, layer=layer:
                                (layer, k_block, j + off))
        in_specs.append(spec)
        operands.append(w)
        vmem += 2 * _nbytes((tk, tn), w.dtype) + _nbytes((tk, tn), BF16)
    for arr, bshape, imap in extras:
        in_specs.append(pl.BlockSpec(bshape, imap))
        operands.append(arr)
        vmem += 2 * _nbytes(bshape, arr.dtype)
    out_shapes, out_specs = [], []
    for shape, dtype, bshape, imap in outs:
        out_shapes.append(jax.ShapeDtypeStruct(shape, dtype))
        out_specs.append(pl.BlockSpec(bshape, imap))
        vmem += 2 * _nbytes(bshape, dtype)
    vmem += (2 + len(ws)) * _nbytes((tm, tn), F32)
    body = functools.partial(_mm_body, nw=len(ws), ne=len(extras), no=len(outs), epilogue=epilogue)
    return pl.pallas_call(
        body,
        grid=(m // tm, n_cols // tn),
        in_specs=in_specs,
        out_specs=out_specs,
        out_shape=out_shapes,
        compiler_params=pltpu.CompilerParams(dimension_semantics=("parallel", "parallel"),
                                             vmem_limit_bytes=_vmem_limit(vmem)),
        name=name,
    )(*operands)


def _tile(tm, tn):
    return (tm, tn), (lambda i, j: (i, j))


def _ln_rows(y, g, b):
    mu = jnp.mean(y, axis=-1, keepdims=True)
    yc = y - mu
    var = jnp.mean(yc * yc, axis=-1, keepdims=True)
    return yc * lax.rsqrt(var + LN_EPS) * g + b


def _route_top2(x, w_router):
    logits = jnp.dot(x, w_router, preferred_element_type=F32, precision=lax.Precision.HIGHEST)
    ne = logits.shape[1]
    eid = lax.broadcasted_iota(jnp.int32, logits.shape, 1)
    m1 = jnp.max(logits, axis=-1, keepdims=True)
    i1 = jnp.min(jnp.where(logits == m1, eid, ne), axis=-1, keepdims=True)
    rest = jnp.where(eid == i1, -jnp.inf, logits)
    m2 = jnp.max(rest, axis=-1, keepdims=True)
    i2 = jnp.min(jnp.where(rest == m2, eid, ne), axis=-1, keepdims=True)
    e2 = jnp.exp(m2 - m1)
    w_top1 = 1.0 / (1.0 + e2)
    w_top2 = e2 / (1.0 + e2)
    gates = jnp.where(eid == i1, w_top1, 0.0) + jnp.where(eid == i2, w_top2, 0.0)
    return gates, jnp.where(eid == 0, i1, i2)[:, :TOP_K]


def _ln_body(*refs, route):
    if route:
        y_ref, g_ref, b_ref, wr_ref, o32_ref, o16_ref, gates_ref, idx_ref = refs
    else:
        y_ref, g_ref, b_ref, o32_ref, o16_ref = refs
    out = _ln_rows(y_ref[...], g_ref[...], b_ref[...])
    o32_ref[...] = out
    o16_ref[...] = out.astype(BF16)
    if route:
        gates_ref[...], idx_ref[...] = _route_top2(out, wr_ref[...])


def _layer_norm(y, g, b, w_router=None, *, tr=256):
    s, d = y.shape
    tr = min(tr, s)
    assert s % tr == 0
    row = pl.BlockSpec((tr, d), lambda i: (i, 0))
    vec = pl.BlockSpec((1, d), lambda i: (0, 0))
    in_specs, operands = [row, vec, vec], [y, g.reshape(1, d), b.reshape(1, d)]
    out_specs = [row, row]
    out_shape = [jax.ShapeDtypeStruct((s, d), F32), jax.ShapeDtypeStruct((s, d), BF16)]
    vmem = 2 * (2 * _nbytes((tr, d), F32) + _nbytes((tr, d), BF16)) + 2 * _nbytes((tr, d), F32)
    if w_router is not None:
        ne = w_router.shape[1]
        in_specs.append(pl.BlockSpec((d, ne), lambda i: (0, 0)))
        operands.append(w_router)
        out_specs += [pl.BlockSpec((tr, ne), lambda i: (i, 0)),
                      pl.BlockSpec((tr, TOP_K), lambda i: (i, 0))]
        out_shape += [jax.ShapeDtypeStruct((s, ne), F32), jax.ShapeDtypeStruct((s, TOP_K), jnp.int32)]
        vmem += 2 * _nbytes((d, V7X_LANES), F32) + 4 * _nbytes((tr, d), F32)
    return pl.pallas_call(
        functools.partial(_ln_body, route=w_router is not None),
        grid=(s // tr,),
        in_specs=in_specs,
        out_specs=out_specs,
        out_shape=out_shape,
        compiler_params=pltpu.CompilerParams(dimension_semantics=("parallel",),
                                             vmem_limit_bytes=_vmem_limit(vmem)),
        name="layer_norm_route" if w_router is not None else "layer_norm",
    )(*operands)


def _lower_bound(lbl, layer):
    mx = jnp.max(lbl, axis=0, keepdims=True)
    e = jnp.exp(lbl - mx)
    return jnp.sum(e[:layer + 1], axis=0, keepdims=True) / jnp.sum(e, axis=0, keepdims=True)


def _hgrn_in_proj(xb, w_in, lb_logits, layer, *, tm=MM_ROWS, tn=MM_COLS):
    s, d = xb.shape
    blk, imap = _tile(min(tm, s), min(tn, d))

    def ep_silu(accs, extras, outs, rows):
        outs[0][rows, :] = _silu(accs[0]).astype(BF16)

    def ep_id(accs, extras, outs, rows):
        outs[0][rows, :] = accs[0].astype(BF16)

    def ep_forget(accs, extras, outs, rows):
        lb = _lower_bound(extras[0][...], layer)
        f_raw = accs[0]
        e = jnp.exp(-jnp.abs(f_raw))
        log_sig = jnp.minimum(f_raw, 0.0) - jnp.log(1.0 + e)
        a = jnp.log(lb)
        b = jnp.log1p(-lb) + log_sig
        log_f = jnp.maximum(a, b) + jnp.log(1.0 + jnp.exp(-jnp.abs(a - b)))
        outs[0][rows, :] = log_f
        nrow, ncol = log_f.shape
        outs[1][rows.start // A_CHUNK:rows.stop // A_CHUNK, :] = jnp.sum(
            log_f.reshape(nrow // A_CHUNK, A_CHUNK, ncol), axis=1)

    def call(seg, epilogue, extras, out_dtypes, name, more_outs=(), cols=tn):
        cblk, cimap = _tile(blk[0], min(cols, d))
        outs = [((s, d), dt, cblk, cimap) for dt in out_dtypes] + list(more_outs)
        return _fused_matmul(xb, [(w_in, layer, seg * (d // cblk[1]))], d, extras, outs, epilogue,
                             tm=tm, tn=cols, name=name)

    nslot = lb_logits.shape[0]
    lb_extra = [(lb_logits, (nslot, blk[1]), lambda i, j: (0, j))]
    decay_out = ((s // A_CHUNK, d), F32, (blk[0] // A_CHUNK, blk[1]), imap)
    (q,) = call(0, ep_silu, [], [BF16], "hgrn_q", cols=MM_COLS_BF16_OUT)
    log_f, chunk_decay = call(1, ep_forget, lb_extra, [F32], "hgrn_f", more_outs=[decay_out])
    (v,) = call(2, ep_id, [], [BF16], "hgrn_v", cols=MM_COLS_BF16_OUT)
    (g,) = call(3, ep_silu, [], [BF16], "hgrn_g", cols=MM_COLS_BF16_OUT)
    return q, v, log_f, g, chunk_decay


def _chunk_cumsum(lf_ref, tc):
    x = lf_ref[...]
    row_in_chunk = lax.broadcasted_iota(jnp.int32, x.shape, 0) % A_CHUNK
    shift = 1
    while shift < A_CHUNK:
        x = x + jnp.where(row_in_chunk >= shift, pltpu.roll(x, shift, axis=0), 0.0)
        shift *= 2
    return x


def _hgrn_body(bounded_ref, q_ref, v_ref, lf_ref, g_ref, og_ref, o_ref, st_ref, *, tc):
    @pl.when(pl.program_id(1) == 0)
    def _():
        st_ref[...] = jnp.zeros_like(st_ref)

    out_g = og_ref[...]
    half = A_SUB // 2
    lane = lax.broadcasted_iota(jnp.int32, (half, A_HEAD_DIM), 1)
    hrow = lax.broadcasted_iota(jnp.int32, (half, A_HEAD_DIM), 0)
    nsub = A_CHUNK // A_SUB
    nchunk = tc // A_CHUNK

    def finish(rows, o):
        o = o * lax.rsqrt(jnp.mean(o * o, axis=-1, keepdims=True) + RMS_EPS)
        o_ref[rows, :] = (o * out_g * g_ref[rows, :].astype(F32)).astype(BF16)

    def as_column(row):
        return jnp.transpose(jnp.broadcast_to(row, (V7X_SUBLANES, row.shape[1])))[:, :1]

    def inter_chunk(ci, cum, st_t):
        rows = slice(ci * A_CHUNK, (ci + 1) * A_CHUNK)
        c2 = cum[rows] * LOG2E
        qc = q_ref[rows, :].astype(F32)
        kc = 1.0 - jnp.exp2(lf_ref[rows, :] * LOG2E)
        vc = v_ref[rows, :]
        last = c2[A_CHUNK - 1:A_CHUNK]
        qe = (qc * jnp.exp2(c2)).astype(BF16)
        o_inter = lax.dot_general(qe, st_t.astype(BF16), (((1,), (1,)), ((), ())),
                                  preferred_element_type=F32)
        kd = (kc * jnp.exp2(last - c2)).astype(BF16)
        upd = lax.dot_general(vc, kd, (((0,), (0,)), ((), ())), preferred_element_type=F32)
        return rows, c2, qc, kc, vc, qe, o_inter, st_t * jnp.exp2(last) + upd

    def run_factorised():
        c2 = _chunk_cumsum(lf_ref, tc) * LOG2E
        qe = (q_ref[...].astype(F32) * jnp.exp2(c2)).astype(BF16)
        kb32 = (1.0 - jnp.exp2(lf_ref[...] * LOG2E)) * jnp.exp2(-c2)
        kb = kb32.astype(BF16)
        t_i = lax.broadcasted_iota(jnp.int32, (A_CHUNK, A_CHUNK), 0)
        s_i = lax.broadcasted_iota(jnp.int32, (A_CHUNK, A_CHUNK), 1)
        causal = t_i >= s_i
        st = st_ref[...]
        outs = []
        for ci in range(nchunk):
            rows = slice(ci * A_CHUNK, (ci + 1) * A_CHUNK)
            vc = v_ref[rows, :]
            grow = jnp.exp2(c2[(ci + 1) * A_CHUNK - 1:(ci + 1) * A_CHUNK])
            sc = lax.dot_general(qe[rows], kb[rows], (((1,), (1,)), ((), ())),
                                 preferred_element_type=F32)
            scores = jnp.where(causal, sc, 0.0).astype(BF16)
            outs.append(jnp.dot(jnp.concatenate([qe[rows], scores], axis=1),
                                jnp.concatenate([st.astype(BF16), vc], axis=0),
                                preferred_element_type=F32))
            kd = (kb32[rows] * grow).astype(BF16)
            upd = lax.dot_general(kd, vc, (((0,), (0,)), ((), ())), preferred_element_type=F32)
            st = st * as_column(grow) + upd
        st_ref[...] = st
        finish(slice(0, tc), jnp.concatenate(outs, axis=0))

    def direct_chunk(ci, cum, st):
        rows, c2, qc, kc, vc, qe, o_inter, st = inter_chunk(ci, cum, st)

        p_rows = []
        k_blocks, prev_ref = [], None
        for si in range(nsub):
            lo_r = si * A_SUB
            cs, qs, ks = c2[lo_r:lo_r + A_SUB], qc[lo_r:lo_r + A_SUB], kc[lo_r:lo_r + A_SUB]
            off = None
            if si > 0:
                ref = c2[lo_r - 1:lo_r]
                if prev_ref is not None:
                    rebase = jnp.exp2(ref - prev_ref)
                    k_blocks = [kb * rebase for kb in k_blocks]
                k_blocks.append(kc[lo_r - A_SUB:lo_r] * jnp.exp2(ref - c2[lo_r - A_SUB:lo_r]))
                prev_ref = ref
                k_t = jnp.concatenate(
                    k_blocks + [jnp.zeros((A_CHUNK - lo_r, A_HEAD_DIM), F32)], axis=0)
                q_t = (qs * jnp.exp2(cs - ref)).astype(BF16)
                off = lax.dot_general(q_t, k_t.astype(BF16), (((1,), (1,)), ((), ())),
                                      preferred_element_type=F32)
            c_h = (cs[:half], cs[half:])
            q_h = (qs[:half], qs[half:])
            p_h = [jnp.zeros((half, A_HEAD_DIM), F32), jnp.zeros((half, A_HEAD_DIM), F32)]
            for sj in range(A_SUB):
                for hh in range(sj // half, 2):
                    decay = jnp.exp2(c_h[hh] - cs[sj:sj + 1])
                    colv = jnp.sum(q_h[hh] * (ks[sj:sj + 1] * decay), axis=1, keepdims=True)
                    sel = lane == lo_r + sj
                    if sj // half == hh:
                        sel = sel & (hrow >= sj - hh * half)
                    p_h[hh] = jnp.where(sel, colv, p_h[hh])
            for hh in range(2):
                ph = p_h[hh][:, :A_CHUNK]
                p_rows.append(ph if off is None else ph + off[hh * half:(hh + 1) * half])
        scores = jnp.concatenate(p_rows, axis=0).astype(BF16)
        finish(rows, o_inter + jnp.dot(scores, vc, preferred_element_type=F32))
        return st

    def run_direct():
        cum = _chunk_cumsum(lf_ref, tc)
        st_t = st_ref[...].T
        for ci in range(nchunk):
            st_t = direct_chunk(ci, cum, st_t)
        st_ref[...] = st_t.T

    bounded = bounded_ref[pl.program_id(0), pl.program_id(1)] > 0
    pl.when(bounded)(run_factorised)
    pl.when(jnp.logical_not(bounded))(run_direct)


def _hgrn_core(q, v, log_f, g, out_g, chunk_decay, *, tc=2048):
    s, d = q.shape
    tc = min(tc, s)
    assert s % tc == 0 and tc % A_CHUNK == 0 and d % A_HEAD_DIM == 0
    nhead, nblk = d // A_HEAD_DIM, s // tc
    worst = jnp.min(chunk_decay.reshape(nblk, tc // A_CHUNK, nhead, A_HEAD_DIM), axis=(1, 3))
    bounded = (worst.T * LOG2E >= -A_SAFE_LOG2_DECAY).astype(jnp.int32)
    blk = pl.BlockSpec((tc, A_HEAD_DIM), lambda h, c, flags: (c, h))
    vmem = 2 * (4 * _nbytes((tc, A_HEAD_DIM), BF16) + _nbytes((tc, A_HEAD_DIM), F32)) \
        + 12 * _nbytes((tc, A_HEAD_DIM), F32)
    grid_spec = pltpu.PrefetchScalarGridSpec(
        num_scalar_prefetch=1,
        grid=(nhead, nblk),
        in_specs=[blk, blk, blk, blk,
                  pl.BlockSpec((1, A_HEAD_DIM), lambda h, c, flags: (0, h))],
        out_specs=blk,
        scratch_shapes=[pltpu.VMEM((A_HEAD_DIM, A_HEAD_DIM), F32)],
    )
    return pl.pallas_call(
        functools.partial(_hgrn_body, tc=tc),
        grid_spec=grid_spec,
        out_shape=jax.ShapeDtypeStruct((s, d), BF16),
        compiler_params=pltpu.CompilerParams(dimension_semantics=("parallel", "arbitrary"),
                                             vmem_limit_bytes=_vmem_limit(vmem)),
        name="hgrn_core",
    )(bounded, q, v, log_f, g, out_g.reshape(1, d))


def _residual_matmul(hb, w, layer, resid, alpha, *, name, tm=MM_ROWS, tn=MM_COLS):
    s, d = resid.shape
    kdim = hb.shape[1]
    tk = min(kdim, d)
    blk, imap = _tile(min(tm, s), min(tn, d))
    y = resid
    for kb in range(kdim // tk):
        scale = alpha if kb == 0 else 1.0

        def ep(accs, extras, outs, rows, scale=scale):
            outs[0][rows, :] = scale * extras[0][rows, :] + accs[0]

        (y,) = _fused_matmul(hb, [(w, layer, 0)], d, [(y, blk, imap)], [((s, d), F32, blk, imap)],
                             ep, tm=tm, tn=tn, tk=tk, k_block=kb, name=name)
    return y


def _rel_bucket_band():
    i = np.arange(B_BLOCK)[:, None]
    j = np.arange(2 * B_BLOCK)[None, :]
    d = np.clip(B_BLOCK + i - j, 0, None)
    max_exact = REL_BUCKETS // 2
    large = max_exact + (np.log(np.maximum(d, 1) / max_exact)
                         / np.log(REL_MAX_DISTANCE / max_exact)
                         * (REL_BUCKETS - max_exact)).astype(np.int32)
    large = np.minimum(large, REL_BUCKETS - 1)
    return np.where(d < max_exact, d, large).astype(np.int32)


def _bias_body(rb_ref, bucket_ref, o_ref, *, heads_per_step):
    h0 = pl.program_id(0) * heads_per_step
    bucket = bucket_ref[...]
    qi = lax.broadcasted_iota(jnp.int32, bucket.shape, 0)
    kj = lax.broadcasted_iota(jnp.int32, bucket.shape, 1)
    dist = B_BLOCK + qi - kj
    in_window = (dist >= 0) & (dist < WINDOW)
    for hh in range(heads_per_step):
        acc = jnp.zeros(bucket.shape, F32)
        for b in range(REL_BUCKETS):
            acc = jnp.where(bucket == b, rb_ref[b, h0 + hh], acc)
        o_ref[hh] = jnp.where(in_window, acc, -jnp.inf)


def _rel_bias_band(rel_bias, *, heads_per_step=8):
    nh = rel_bias.shape[1]
    bucket = jnp.asarray(_rel_bucket_band())
    return pl.pallas_call(
        functools.partial(_bias_body, heads_per_step=heads_per_step),
        grid=(nh // heads_per_step,),
        in_specs=[pl.BlockSpec(memory_space=pltpu.SMEM),
                  pl.BlockSpec((B_BLOCK, 2 * B_BLOCK), lambda i: (0, 0))],
        out_specs=pl.BlockSpec((heads_per_step, B_BLOCK, 2 * B_BLOCK), lambda i: (i, 0, 0)),
        out_shape=jax.ShapeDtypeStruct((nh, B_BLOCK, 2 * B_BLOCK), F32),
        compiler_params=pltpu.CompilerParams(dimension_semantics=("parallel",)),
        name="rel_bias_band",
    )(rel_bias, bucket)


def _head_split_matmul(xb, w, layer, n_cols, *, name, tm=MM_ROWS, tn=MM_COLS):
    s, kdim = xb.shape
    tm, tn = min(tm, s), min(tn, n_cols)
    hpt = tn // B_HEAD_DIM

    def ep(accs, extras, outs, rows):
        for c in range(hpt):
            outs[0][c, rows, :] = accs[0][:, c * B_HEAD_DIM:(c + 1) * B_HEAD_DIM].astype(BF16)

    outs = [((n_cols // B_HEAD_DIM, s, B_HEAD_DIM), BF16, (hpt, tm, B_HEAD_DIM),
             lambda i, j: (j, i, 0))]
    (out,) = _fused_matmul(xb, [(w, layer, 0)], n_cols, [], outs, ep, tm=tm, tn=tn, name=name)
    return out


def _attn_body(sink_ref, q_ref, kp_ref, kc_ref, vp_ref, vc_ref, bias_ref, o_ref, *, group, qb):
    h = pl.program_id(0)
    m_blk = pl.program_id(1)
    c = B_BLOCK
    k_all = jnp.concatenate([kp_ref[0], kc_ref[0]], axis=0)
    v_all = jnp.concatenate([vp_ref[0], vc_ref[0]], axis=0)
    kj = lax.broadcasted_iota(jnp.int32, (c, 2 * c), 1)
    for b in range(qb):
        q = q_ref[:, b * c:(b + 1) * c, :].reshape(group * c, B_HEAD_DIM) * B_LOGIT_SCALE
        kb = k_all[b * c:(b + 2) * c]
        vb = v_all[b * c:(b + 2) * c]
        logits = lax.dot_general(q, kb, (((1,), (1,)), ((), ())), preferred_element_type=F32)
        probs = []
        for gi in range(group):
            lg = logits[gi * c:(gi + 1) * c] + bias_ref[gi]
            if b == 0:
                lg = jnp.where((m_blk > 0) | (kj >= c), lg, -jnp.inf)
            sink = sink_ref[h * group + gi]
            m = jnp.maximum(jnp.max(lg, axis=-1, keepdims=True), sink)
            e = jnp.exp(lg - m)
            denom = jnp.sum(e, axis=-1, keepdims=True) + jnp.exp(sink - m)
            probs.append((e / denom).astype(BF16))
        pv = jnp.dot(jnp.concatenate(probs, axis=0), vb, preferred_element_type=F32)
        o_ref[b * c:(b + 1) * c, :] = jnp.concatenate(
            [pv[gi * c:(gi + 1) * c] for gi in range(group)], axis=1).astype(BF16)


def _swa_core(qh, kvh, sinks, bias, n_kv, *, qb=4):
    nq, s, hd = qh.shape
    group = nq // n_kv
    nb = s // B_BLOCK
    qb = min(qb, nb)
    assert nb % qb == 0
    prev = lambda m: jnp.maximum(m * qb - 1, 0)
    prev_blk = (1, B_BLOCK, hd)
    cur_blk = (1, qb * B_BLOCK, hd)
    in_specs = [
        pl.BlockSpec(memory_space=pltpu.SMEM),
        pl.BlockSpec((group, qb * B_BLOCK, hd), lambda h, m: (h, m, 0)),
        pl.BlockSpec(prev_blk, lambda h, m: (h, prev(m), 0)),
        pl.BlockSpec(cur_blk, lambda h, m: (h, m, 0)),
        pl.BlockSpec(prev_blk, lambda h, m: (h + n_kv, prev(m), 0)),
        pl.BlockSpec(cur_blk, lambda h, m: (h + n_kv, m, 0)),
        pl.BlockSpec((group, B_BLOCK, 2 * B_BLOCK), lambda h, m: (h, 0, 0)),
    ]
    vmem = 2 * (_nbytes((group, B_BLOCK, 2 * B_BLOCK), F32)
                + (group + 3) * _nbytes((qb * B_BLOCK, V7X_LANES), BF16)
                + _nbytes((qb * B_BLOCK, group * hd), BF16)) \
        + 6 * qb * _nbytes((group * B_BLOCK, 2 * B_BLOCK), F32)
    return pl.pallas_call(
        functools.partial(_attn_body, group=group, qb=qb),
        grid=(n_kv, nb // qb),
        in_specs=in_specs,
        out_specs=pl.BlockSpec((qb * B_BLOCK, group * hd), lambda h, m: (m, h)),
        out_shape=jax.ShapeDtypeStruct((s, nq * hd), BF16),
        compiler_params=pltpu.CompilerParams(dimension_semantics=("parallel", "parallel"),
                                             vmem_limit_bytes=_vmem_limit(vmem)),
        name="swa_core",
    )(sinks, qh, kvh, kvh, kvh, kvh, bias)


def _swiglu_up(xb, w1, w3, layer, *, tm=MM_ROWS, tn=MM_COLS):
    s, d = xb.shape
    f = w1.shape[2]
    blk, imap = _tile(min(tm, s), min(tn, f))

    def ep(accs, extras, outs, rows):
        outs[0][rows, :] = (_silu(accs[0]) * accs[1]).astype(BF16)

    (h,) = _fused_matmul(xb, [(w1, layer, 0), (w3, layer, 0)], f, [],
                         [((s, f), BF16, blk, imap)], ep, tm=tm, tn=tn, name="swiglu_up")
    return h


def _routing_tables(gates, idx, tm):
    s, ne = gates.shape
    n_assign = s * TOP_K
    ntiles = n_assign // tm + ne
    e_flat = idx.reshape(n_assign)
    onehot = (e_flat[:, None] == jnp.arange(ne, dtype=jnp.int32)[None, :]).astype(jnp.int32)
    rank = jnp.take_along_axis(jnp.cumsum(onehot, axis=0) - onehot, e_flat[:, None], axis=1)[:, 0]
    counts = jnp.sum(onehot, axis=0)
    padded = ((counts + tm - 1) // tm) * tm
    ends = jnp.cumsum(padded)
    pos = (ends - padded)[e_flat] + rank
    token = jnp.arange(n_assign, dtype=jnp.int32) // TOP_K
    gate = jnp.take_along_axis(gates, idx, axis=1)
    row_token = jnp.zeros((ntiles * tm,), jnp.int32).at[pos].set(token)
    tile_start = jnp.arange(ntiles, dtype=jnp.int32) * tm
    tile_expert = jnp.minimum(jnp.searchsorted(ends, tile_start, side="right"), ne - 1).astype(jnp.int32)
    tile_valid = tile_start < ends[-1]
    first = tile_valid & jnp.concatenate(
        [jnp.ones((1,), bool), tile_expert[1:] != tile_expert[:-1]])
    slot = (jnp.cumsum(first.astype(jnp.int32)) - 1) % 2
    eid = jnp.arange(ne, dtype=jnp.int32)
    later = (counts > 0)[None, :] & (eid[None, :] > eid[:, None])
    next_used = jnp.min(jnp.where(later, eid[None, :], ne), axis=1)
    next_used = jnp.where(next_used == ne, -1, next_used)
    sched = jnp.stack([tile_expert, tile_valid.astype(jnp.int32), first.astype(jnp.int32),
                       slot.astype(jnp.int32), next_used[tile_expert]]).astype(jnp.int32)
    return row_token, gate, pos.reshape(s, TOP_K).astype(jnp.int32), sched, ntiles


GATHER_ISSUE_UNROLL = 8


def _gather_body(tv_ref, tok_ref, tok_next_ref, x_hbm, o_ref, buf, sem, *, tm, ntiles):
    i = pl.program_id(0)
    slot = i % 2

    def row_copy(tile_tok_ref, slot_, r):
        return pltpu.make_async_copy(x_hbm.at[pl.ds(tile_tok_ref[0, 0, r], 1)],
                                     buf.at[slot_, pl.ds(r, 1)], sem.at[slot_])

    def start_tile(tile_tok_ref, slot_):
        def body(r, carry):
            row_copy(tile_tok_ref, slot_, r).start()
            return carry
        lax.fori_loop(0, tm, body, 0, unroll=GATHER_ISSUE_UNROLL)

    @pl.when((i == 0) & (tv_ref[0] > 0))
    def _():
        start_tile(tok_ref, slot)

    @pl.when((i + 1 < ntiles) & (tv_ref[jnp.minimum(i + 1, ntiles - 1)] > 0))
    def _():
        start_tile(tok_next_ref, 1 - slot)

    @pl.when(tv_ref[i] > 0)
    def _():
        def wait_body(r, carry):
            row_copy(tok_ref, slot, r).wait()
            return carry
        lax.fori_loop(0, tm, wait_body, 0, unroll=GATHER_ISSUE_UNROLL)
        o_ref[...] = buf[slot].astype(BF16)

    @pl.when(tv_ref[i] == 0)
    def _():
        o_ref[...] = jnp.zeros_like(o_ref)


def _moe_gather(x, row_token, tile_valid, ntiles, tm):
    s, d = x.shape
    vmem = 2 * _nbytes((tm, d), F32) + 2 * _nbytes((tm, d), BF16) + _nbytes((tm, d), F32)
    tok = row_token.reshape(ntiles, 1, tm)
    grid_spec = pltpu.PrefetchScalarGridSpec(
        num_scalar_prefetch=1,
        grid=(ntiles,),
        in_specs=[pl.BlockSpec((1, 1, tm), lambda i, tv: (i, 0, 0), memory_space=pltpu.SMEM),
                  pl.BlockSpec((1, 1, tm), lambda i, tv: (jnp.minimum(i + 1, ntiles - 1), 0, 0),
                               memory_space=pltpu.SMEM),
                  pl.BlockSpec(memory_space=pl.ANY)],
        out_specs=pl.BlockSpec((tm, d), lambda i, tv: (i, 0)),
        scratch_shapes=[pltpu.VMEM((2, tm, d), F32), pltpu.SemaphoreType.DMA((2,))],
    )
    return pl.pallas_call(
        functools.partial(_gather_body, tm=tm, ntiles=ntiles),
        grid_spec=grid_spec,
        out_shape=jax.ShapeDtypeStruct((ntiles * tm, d), BF16),
        compiler_params=pltpu.CompilerParams(dimension_semantics=("arbitrary",),
                                             vmem_limit_bytes=_vmem_limit(vmem)),
        name="moe_gather",
    )(tile_valid, tok, tok, x)


SCHED_EXPERT, SCHED_VALID, SCHED_FIRST, SCHED_SLOT, SCHED_NEXT = range(5)


def _grouped_body(sched_ref, x_ref, *rest, nw, layer, tcol, epilogue):
    w_hbm, o_ref, wbufs, sem = rest[:nw], rest[nw], rest[nw + 1:2 * nw + 1], rest[2 * nw + 1]
    j, i = pl.program_id(0), pl.program_id(1)
    expert, slot, nxt = sched_ref[SCHED_EXPERT, i], sched_ref[SCHED_SLOT, i], sched_ref[SCHED_NEXT, i]
    first = sched_ref[SCHED_FIRST, i] > 0
    col0 = pl.multiple_of(j * tcol, tcol)

    def copies(e, s):
        return [pltpu.make_async_copy(w.at[layer, e, :, pl.ds(col0, tcol)], wb.at[s], sem.at[s])
                for w, wb in zip(w_hbm, wbufs)]

    @pl.when(i == 0)
    def _():
        for c in copies(expert, slot):
            c.start()

    @pl.when(first & (nxt >= 0))
    def _():
        for c in copies(nxt, 1 - slot):
            c.start()

    @pl.when(first)
    def _():
        for c in copies(expert, slot):
            c.wait()

    @pl.when(sched_ref[SCHED_VALID, i] > 0)
    def _():
        wbs = [wb[slot].astype(BF16) for wb in wbufs]
        sub = min(x_ref.shape[0], MM_SUB_ROWS)
        for r0 in range(0, x_ref.shape[0], sub):
            xv = x_ref[r0:r0 + sub, :]
            o_ref[r0:r0 + sub, :] = epilogue(
                [jnp.dot(xv, wb, preferred_element_type=F32) for wb in wbs])

    @pl.when(sched_ref[SCHED_VALID, i] == 0)
    def _():
        o_ref[...] = jnp.zeros_like(o_ref)


def _grouped_matmul(x, ws, layer, sched, tm, tcol, out_dtype, epilogue, name):
    rows, kdim = x.shape
    n = ws[0].shape[3]
    tcol = min(tcol, n)
    ntiles = rows // tm
    vmem = 2 * _nbytes((tm, kdim), BF16) + 2 * _nbytes((tm, tcol), out_dtype) \
        + len(ws) * (2 * _nbytes((kdim, tcol), F32) + _nbytes((kdim, tcol), BF16)) \
        + (1 + len(ws)) * _nbytes((tm, tcol), F32)
    grid_spec = pltpu.PrefetchScalarGridSpec(
        num_scalar_prefetch=1,
        grid=(n // tcol, ntiles),
        in_specs=[pl.BlockSpec((tm, kdim), lambda j, i, sched: (i, 0))]
        + [pl.BlockSpec(memory_space=pl.ANY)] * len(ws),
        out_specs=pl.BlockSpec((tm, tcol), lambda j, i, sched: (i, j)),
        scratch_shapes=[pltpu.VMEM((2, kdim, tcol), F32) for _ in ws]
        + [pltpu.SemaphoreType.DMA((2,))],
    )
    return pl.pallas_call(
        functools.partial(_grouped_body, nw=len(ws), layer=layer, tcol=tcol, epilogue=epilogue),
        grid_spec=grid_spec,
        out_shape=jax.ShapeDtypeStruct((rows, n), out_dtype),
        compiler_params=pltpu.CompilerParams(dimension_semantics=("arbitrary", "arbitrary"),
                                             vmem_limit_bytes=_vmem_limit(vmem)),
        name=name,
    )(sched, x, *ws)


def _moe_up(xg, w1, w3, layer, sched, tm, *, tf=512):
    return _grouped_matmul(xg, [w1, w3], layer, sched, tm, tf, BF16,
                           lambda p: (_silu(p[0]) * p[1]).astype(BF16), "moe_up")


def _moe_down(h, w2, layer, sched, tm, *, tn=2048):
    return _grouped_matmul(h, [w2], layer, sched, tm, tn, F32, lambda p: p[0], "moe_down")


def _combine_body(pos_ref, pos_next_ref, y_hbm, x_ref, gate_ref, g_ref, b_ref, o32_ref, o16_ref,
                  buf, sem, *, tr, nt, alpha):
    i = pl.program_id(0)
    slot = i % 2

    def row_copy(tile_pos_ref, slot_, r, kk):
        return pltpu.make_async_copy(y_hbm.at[pl.ds(tile_pos_ref[0, kk, r], 1)],
                                     buf.at[slot_, kk, pl.ds(r, 1)], sem.at[slot_])

    def start_tile(tile_pos_ref, slot_):
        def body(r, carry):
            for kk in range(TOP_K):
                row_copy(tile_pos_ref, slot_, r, kk).start()
            return carry
        lax.fori_loop(0, tr, body, 0, unroll=GATHER_ISSUE_UNROLL)

    @pl.when(i == 0)
    def _():
        start_tile(pos_ref, slot)

    @pl.when(i + 1 < nt)
    def _():
        start_tile(pos_next_ref, 1 - slot)

    def wait_body(r, carry):
        for kk in range(TOP_K):
            row_copy(pos_ref, slot, r, kk).wait()
        return carry
    lax.fori_loop(0, tr, wait_body, 0, unroll=GATHER_ISSUE_UNROLL)

    gate = gate_ref[...]
    f = buf[slot, 0] * gate[:, 0:1]
    for kk in range(1, TOP_K):
        f = f + buf[slot, kk] * gate[:, kk:kk + 1]
    out = _ln_rows(alpha * x_ref[...] + f, g_ref[...], b_ref[...])
    o32_ref[...] = out
    o16_ref[...] = out.astype(BF16)


def _moe_combine_ln(y, pos, gate, x, g, b, alpha, *, tr=256):
    s, d = x.shape
    tr = min(tr, s)
    nt = s // tr
    pos_t = pos.reshape(nt, tr, TOP_K).transpose(0, 2, 1)
    row = pl.BlockSpec((tr, d), lambda i: (i, 0))
    vec = pl.BlockSpec((1, d), lambda i: (0, 0))
    vmem = 2 * TOP_K * _nbytes((tr, d), F32) \
        + 2 * (2 * _nbytes((tr, d), F32) + _nbytes((tr, d), BF16)) + 3 * _nbytes((tr, d), F32)
    pos_blk = (1, TOP_K, tr)
    return pl.pallas_call(
        functools.partial(_combine_body, tr=tr, nt=nt, alpha=alpha),
        grid=(nt,),
        in_specs=[pl.BlockSpec(pos_blk, lambda i: (i, 0, 0), memory_space=pltpu.SMEM),
                  pl.BlockSpec(pos_blk, lambda i: (jnp.minimum(i + 1, nt - 1), 0, 0),
                               memory_space=pltpu.SMEM),
                  pl.BlockSpec(memory_space=pl.ANY), row,
                  pl.BlockSpec((tr, TOP_K), lambda i: (i, 0)), vec, vec],
        out_specs=[row, row],
        out_shape=[jax.ShapeDtypeStruct((s, d), F32), jax.ShapeDtypeStruct((s, d), BF16)],
        scratch_shapes=[pltpu.VMEM((2, TOP_K, tr, d), F32), pltpu.SemaphoreType.DMA((2,))],
        compiler_params=pltpu.CompilerParams(dimension_semantics=("arbitrary",),
                                             vmem_limit_bytes=_vmem_limit(vmem)),
        name="moe_combine_ln",
    )(pos_t, pos_t, y, x, gate, g.reshape(1, d), b.reshape(1, d))


def _ple(xb, x, p, w_gate, w_proj, layer, bi, *, tm=MM_ROWS, tn=MM_COLS):
    s, d = x.shape
    pdim = p.shape[-1]
    blk, imap = _tile(min(tm, s), min(tn, d))

    def ep(accs, extras, outs, rows):
        proj = jnp.dot(extras[1][rows, :].astype(BF16), extras[2][...].astype(BF16),
                       preferred_element_type=F32)
        out = extras[0][rows, :] + jax.nn.sigmoid(accs[0]) * proj
        outs[0][rows, :] = out
        outs[1][rows, :] = out.astype(BF16)

    extras = [(x, blk, imap),
              (p, (None, None, blk[0], pdim), lambda i, j: (layer, bi, i, 0)),
              (w_proj, (None, pdim, blk[1]), lambda i, j: (layer, 0, j))]
    outs = [((s, d), F32, blk, imap), ((s, d), BF16, blk, imap)]
    return _fused_matmul(xb, [(w_gate, layer, 0)], d, extras, outs, ep, tm=tm, tn=tn, name="ple")


MOE_TILE_ROWS = 512


def kernel(x, p, ln1_g, ln1_b, ln2_g, ln2_b, a_w_in, a_lb_logits, a_out_g, a_w_o, kv_w, b_w_q, b_sinks, b_w_o, rel_bias, ffn_w1, ffn_w3, ffn_w2, moe_router, moe_w1, moe_w3, moe_w2, ple_w_proj, ple_w_gate):
    batch, s, d = x.shape
    depth = p.shape[0]
    n_a = a_w_in.shape[0]
    kv_dim = kv_w.shape[1] // 2
    n_kv = kv_dim // B_HEAD_DIM
    alpha = (2.0 * depth) ** 0.25
    moe_tm = min(MOE_TILE_ROWS, s)
    bias_band = _rel_bias_band(rel_bias.astype(F32))
    lb_logits = a_lb_logits.astype(F32)

    outs = []
    for bi in range(batch):
        xf = x[bi].astype(F32)
        xb = xf.astype(BF16)
        kvh = None
        for i in range(depth):
            if i < n_a:
                q, v, log_f, g, chunk_decay = _hgrn_in_proj(xb, a_w_in, lb_logits, i)
                hb = _hgrn_core(q, v, log_f, g, a_out_g[i].astype(F32), chunk_decay)
                y = _residual_matmul(hb, a_w_o, i, xf, alpha, name="mixer_out")
            else:
                j = i - n_a
                qh = _head_split_matmul(xb, b_w_q, j, d, name="swa_q")
                hb = _swa_core(qh, kvh, b_sinks[j].astype(F32), bias_band, n_kv)
                y = _residual_matmul(hb, b_w_o, j, xf, alpha, name="mixer_out")
            li = i // 2
            if i % 2 == 0:
                xf, xb = _layer_norm(y, ln1_g[i].astype(F32), ln1_b[i].astype(F32))
                hmid = _swiglu_up(xb, ffn_w1, ffn_w3, li)
                y = _residual_matmul(hmid, ffn_w2, li, xf, alpha, name="ffn_down")
                xf, xb = _layer_norm(y, ln2_g[i].astype(F32), ln2_b[i].astype(F32))
            else:
                xf, xb, gates, idx = _layer_norm(y, ln1_g[i].astype(F32), ln1_b[i].astype(F32),
                                                 moe_router[li].astype(F32))
                row_token, gate, pos, sched, ntiles = _routing_tables(gates, idx, moe_tm)
                xg = _moe_gather(xf, row_token, sched[SCHED_VALID], ntiles, moe_tm)
                hmid = _moe_up(xg, moe_w1, moe_w3, li, sched, moe_tm)
                yg = _moe_down(hmid, moe_w2, li, sched, moe_tm)
                xf, xb = _moe_combine_ln(yg, pos, gate, xf, ln2_g[i].astype(F32),
                                         ln2_b[i].astype(F32), alpha)
            xf, xb = _ple(xb, xf, p, ple_w_gate, ple_w_proj, i, bi)
            if i == n_a - 1:
                kvh = _head_split_matmul(xb, kv_w, None, 2 * kv_dim, name="kv_proj")
        outs.append(xf)
    return jnp.stack(outs, axis=0).astype(x.dtype)
```

```python
import functools

import numpy as np
import jax
import jax.numpy as jnp
from jax import lax
from jax.experimental import pallas as pl
from jax.experimental.pallas import tpu as pltpu

F32 = jnp.float32
BF16 = jnp.bfloat16

V7X_LANES = 128
V7X_SUBLANES = 8
V7X_SCOPED_VMEM_BYTES = 60000 * 1024
COMPILER_SCRATCH_BYTES = 6 * 1024 * 1024

MM_ROWS = 2048
MM_COLS = 256
MM_COLS_BF16_OUT = 512
MM_SUB_ROWS = 128

A_HEAD_DIM = 128
A_CHUNK = 64
A_SUB = 16
A_SAFE_LOG2_DECAY = 100.0
B_HEAD_DIM = 64
B_LOGIT_SCALE = B_HEAD_DIM ** -0.5
assert B_LOGIT_SCALE == 2.0 ** -3
B_BLOCK = 128
WINDOW = 128
REL_BUCKETS = 32
REL_MAX_DISTANCE = 128
TOP_K = 2
LN_EPS = 1e-5
RMS_EPS = 1e-6
LOG2E = 1.4426950408889634


def _vmem_limit(block_bytes):
    return int(min(block_bytes + COMPILER_SCRATCH_BYTES, V7X_SCOPED_VMEM_BYTES))


def _nbytes(shape, dtype):
    return int(np.prod([n for n in shape if n is not None])) * jnp.dtype(dtype).itemsize


def _silu(x):
    return x * jax.nn.sigmoid(x)


def _mm_body(*refs, nw, ne, no, epilogue):
    x_ref = refs[0]
    w_refs = refs[1:1 + nw]
    extra_refs = refs[1 + nw:1 + nw + ne]
    out_refs = refs[1 + nw + ne:1 + nw + ne + no]
    wbs = [w_ref[...].astype(BF16) for w_ref in w_refs]
    tm = x_ref.shape[0]
    sub = min(tm, MM_SUB_ROWS)
    for r0 in range(0, tm, sub):
        rows = slice(r0, r0 + sub)
        xv = x_ref[rows, :]
        prods = [jnp.dot(xv, wb, preferred_element_type=F32) for wb in wbs]
        epilogue(prods, extra_refs, out_refs, rows)


def _fused_matmul(x, ws, n_cols, extras, outs, epilogue, *, tm, tn, name, tk=None, k_block=0):
    m, kdim = x.shape
    tk = tk or kdim
    tm, tn = min(tm, m), min(tn, n_cols)
    assert m % tm == 0 and n_cols % tn == 0 and kdim % tk == 0
    in_specs = [pl.BlockSpec((tm, tk), lambda i, j: (i, k_block))]
    operands = [x]
    vmem = 2 * _nbytes((tm, tk), x.dtype)
    for w, layer, off in ws:
        if layer is None:
            spec = pl.BlockSpec((tk, tn), lambda i, j, off=off: (k_block, j + off))
        else:
            spec = pl.BlockSpec((None, tk, tn), lambda i, j, off=off, layer=layer:
                                (layer, k_block, j + off))
        in_specs.append(spec)
        operands.append(w)
        vmem += 2 * _nbytes((tk, tn), w.dtype) + _nbytes((tk, tn), BF16)
    for arr, bshape, imap in extras:
        in_specs.append(pl.BlockSpec(bshape, imap))
        operands.append(arr)
        vmem += 2 * _nbytes(bshape, arr.dtype)
    out_shapes, out_specs = [], []
    for shape, dtype, bshape, imap in outs:
        out_shapes.append(jax.ShapeDtypeStruct(shape, dtype))
        out_specs.append(pl.BlockSpec(bshape, imap))
        vmem += 2 * _nbytes(bshape, dtype)
    vmem += (2 + len(ws)) * _nbytes((tm, tn), F32)
    body = functools.partial(_mm_body, nw=len(ws), ne=len(extras), no=len(outs), epilogue=epilogue)
    return pl.pallas_call(
        body,
        grid=(m // tm, n_cols // tn),
        in_specs=in_specs,
        out_specs=out_specs,
        out_shape=out_shapes,
        compiler_params=pltpu.CompilerParams(dimension_semantics=("parallel", "parallel"),
                                             vmem_limit_bytes=_vmem_limit(vmem)),
        name=name,
    )(*operands)


def _tile(tm, tn):
    return (tm, tn), (lambda i, j: (i, j))


def _ln_rows(y, g, b):
    mu = jnp.mean(y, axis=-1, keepdims=True)
    yc = y - mu
    var = jnp.mean(yc * yc, axis=-1, keepdims=True)
    return yc * lax.rsqrt(var + LN_EPS) * g + b


def _route_top2(x, w_router):
    logits = jnp.dot(x, w_router, preferred_element_type=F32, precision=lax.Precision.HIGHEST)
    ne = logits.shape[1]
    eid = lax.broadcasted_iota(jnp.int32, logits.shape, 1)
    m1 = jnp.max(logits, axis=-1, keepdims=True)
    i1 = jnp.min(jnp.where(logits == m1, eid, ne), axis=-1, keepdims=True)
    rest = jnp.where(eid == i1, -jnp.inf, logits)
    m2 = jnp.max(rest, axis=-1, keepdims=True)
    i2 = jnp.min(jnp.where(rest == m2, eid, ne), axis=-1, keepdims=True)
    e2 = jnp.exp(m2 - m1)
    w_top1 = 1.0 / (1.0 + e2)
    w_top2 = e2 / (1.0 + e2)
    gates = jnp.where(eid == i1, w_top1, 0.0) + jnp.where(eid == i2, w_top2, 0.0)
    return gates, jnp.where(eid == 0, i1, i2)[:, :TOP_K]


def _ln_body(*refs, route):
    if route:
        y_ref, g_ref, b_ref, wr_ref, o32_ref, o16_ref, gates_ref, idx_ref = refs
    else:
        y_ref, g_ref, b_ref, o32_ref, o16_ref = refs
    out = _ln_rows(y_ref[...], g_ref[...], b_ref[...])
    o32_ref[...] = out
    o16_ref[...] = out.astype(BF16)
    if route:
        gates_ref[...], idx_ref[...] = _route_top2(out, wr_ref[...])


def _layer_norm(y, g, b, w_router=None, *, tr=256):
    s, d = y.shape
    tr = min(tr, s)
    assert s % tr == 0
    row = pl.BlockSpec((tr, d), lambda i: (i, 0))
    vec = pl.BlockSpec((1, d), lambda i: (0, 0))
    in_specs, operands = [row, vec, vec], [y, g.reshape(1, d), b.reshape(1, d)]
    out_specs = [row, row]
    out_shape = [jax.ShapeDtypeStruct((s, d), F32), jax.ShapeDtypeStruct((s, d), BF16)]
    vmem = 2 * (2 * _nbytes((tr, d), F32) + _nbytes((tr, d), BF16)) + 2 * _nbytes((tr, d), F32)
    if w_router is not None:
        ne = w_router.shape[1]
        in_specs.append(pl.BlockSpec((d, ne), lambda i: (0, 0)))
        operands.append(w_router)
        out_specs += [pl.BlockSpec((tr, ne), lambda i: (i, 0)),
                      pl.BlockSpec((tr, TOP_K), lambda i: (i, 0))]
        out_shape += [jax.ShapeDtypeStruct((s, ne), F32), jax.ShapeDtypeStruct((s, TOP_K), jnp.int32)]
        vmem += 2 * _nbytes((d, V7X_LANES), F32) + 4 * _nbytes((tr, d), F32)
    return pl.pallas_call(
        functools.partial(_ln_body, route=w_router is not None),
        grid=(s // tr,),
        in_specs=in_specs,
        out_specs=out_specs,
        out_shape=out_shape,
        compiler_params=pltpu.CompilerParams(dimension_semantics=("parallel",),
                                             vmem_limit_bytes=_vmem_limit(vmem)),
        name="layer_norm_route" if w_router is not None else "layer_norm",
    )(*operands)


def _lower_bound(lbl, layer):
    mx = jnp.max(lbl, axis=0, keepdims=True)
    e = jnp.exp(lbl - mx)
    return jnp.sum(e[:layer + 1], axis=0, keepdims=True) / jnp.sum(e, axis=0, keepdims=True)


def _hgrn_in_proj(xb, w_in, lb_logits, layer, *, tm=MM_ROWS, tn=MM_COLS):
    s, d = xb.shape
    blk, imap = _tile(min(tm, s), min(tn, d))

    def ep_silu(accs, extras, outs, rows):
        outs[0][rows, :] = _silu(accs[0]).astype(BF16)

    def ep_id(accs, extras, outs, rows):
        outs[0][rows, :] = accs[0].astype(BF16)

    def ep_forget(accs, extras, outs, rows):
        lb = _lower_bound(extras[0][...], layer)
        f_raw = accs[0]
        e = jnp.exp(-jnp.abs(f_raw))
        log_sig = jnp.minimum(f_raw, 0.0) - jnp.log(1.0 + e)
        a = jnp.log(lb)
        b = jnp.log1p(-lb) + log_sig
        log_f = jnp.maximum(a, b) + jnp.log(1.0 + jnp.exp(-jnp.abs(a - b)))
        outs[0][rows, :] = log_f
        nrow, ncol = log_f.shape
        outs[1][rows.start // A_CHUNK:rows.stop // A_CHUNK, :] = jnp.sum(
            log_f.reshape(nrow // A_CHUNK, A_CHUNK, ncol), axis=1)

    def call(seg, epilogue, extras, out_dtypes, name, more_outs=(), cols=tn):
        cblk, cimap = _tile(blk[0], min(cols, d))
        outs = [((s, d), dt, cblk, cimap) for dt in out_dtypes] + list(more_outs)
        return _fused_matmul(xb, [(w_in, layer, seg * (d // cblk[1]))], d, extras, outs, epilogue,
                             tm=tm, tn=cols, name=name)

    nslot = lb_logits.shape[0]
    lb_extra = [(lb_logits, (nslot, blk[1]), lambda i, j: (0, j))]
    decay_out = ((s // A_CHUNK, d), F32, (blk[0] // A_CHUNK, blk[1]), imap)
    (q,) = call(0, ep_silu, [], [BF16], "hgrn_q", cols=MM_COLS_BF16_OUT)
    log_f, chunk_decay = call(1, ep_forget, lb_extra, [F32], "hgrn_f", more_outs=[decay_out])
    (v,) = call(2, ep_id, [], [BF16], "hgrn_v", cols=MM_COLS_BF16_OUT)
    (g,) = call(3, ep_silu, [], [BF16], "hgrn_g", cols=MM_COLS_BF16_OUT)
    return q, v, log_f, g, chunk_decay


def _chunk_cumsum(lf_ref, tc):
    x = lf_ref[...]
    row_in_chunk = lax.broadcasted_iota(jnp.int32, x.shape, 0) % A_CHUNK
    shift = 1
    while shift < A_CHUNK:
        x = x + jnp.where(row_in_chunk >= shift, pltpu.roll(x, shift, axis=0), 0.0)
        shift *= 2
    return x


def _hgrn_body(bounded_ref, q_ref, v_ref, lf_ref, g_ref, og_ref, o_ref, st_ref, *, tc):
    @pl.when(pl.program_id(1) == 0)
    def _():
        st_ref[...] = jnp.zeros_like(st_ref)

    out_g = og_ref[...]
    half = A_SUB // 2
    lane = lax.broadcasted_iota(jnp.int32, (half, A_HEAD_DIM), 1)
    hrow = lax.broadcasted_iota(jnp.int32, (half, A_HEAD_DIM), 0)
    nsub = A_CHUNK // A_SUB
    nchunk = tc // A_CHUNK

    def finish(rows, o):
        o = o * lax.rsqrt(jnp.mean(o * o, axis=-1, keepdims=True) + RMS_EPS)
        o_ref[rows, :] = (o * out_g * g_ref[rows, :].astype(F32)).astype(BF16)

    def as_column(row):
        return jnp.transpose(jnp.broadcast_to(row, (V7X_SUBLANES, row.shape[1])))[:, :1]

    def inter_chunk(ci, cum, st_t):
        rows = slice(ci * A_CHUNK, (ci + 1) * A_CHUNK)
        c2 = cum[rows] * LOG2E
        qc = q_ref[rows, :].astype(F32)
        kc = 1.0 - jnp.exp2(lf_ref[rows, :] * LOG2E)
        vc = v_ref[rows, :]
        last = c2[A_CHUNK - 1:A_CHUNK]
        qe = (qc * jnp.exp2(c2)).astype(BF16)
        o_inter = lax.dot_general(qe, st_t.astype(BF16), (((1,), (1,)), ((), ())),
                                  preferred_element_type=F32)
        kd = (kc * jnp.exp2(last - c2)).astype(BF16)
        upd = lax.dot_general(vc, kd, (((0,), (0,)), ((), ())), preferred_element_type=F32)
        return rows, c2, qc, kc, vc, qe, o_inter, st_t * jnp.exp2(last) + upd

    def run_factorised():
        c2 = _chunk_cumsum(lf_ref, tc) * LOG2E
        qe = (q_ref[...].astype(F32) * jnp.exp2(c2)).astype(BF16)
        kb32 = (1.0 - jnp.exp2(lf_ref[...] * LOG2E)) * jnp.exp2(-c2)
        kb = kb32.astype(BF16)
        t_i = lax.broadcasted_iota(jnp.int32, (A_CHUNK, A_CHUNK), 0)
        s_i = lax.broadcasted_iota(jnp.int32, (A_CHUNK, A_CHUNK), 1)
        causal = t_i >= s_i
        st = st_ref[...]
        outs = []
        for ci in range(nchunk):
            rows = slice(ci * A_CHUNK, (ci + 1) * A_CHUNK)
            vc = v_ref[rows, :]
            grow = jnp.exp2(c2[(ci + 1) * A_CHUNK - 1:(ci + 1) * A_CHUNK])
            sc = lax.dot_general(qe[rows], kb[rows], (((1,), (1,)), ((), ())),
                                 preferred_element_type=F32)
            scores = jnp.where(causal, sc, 0.0).astype(BF16)
            outs.append(jnp.dot(jnp.concatenate([qe[rows], scores], axis=1),
                                jnp.concatenate([st.astype(BF16), vc], axis=0),
                                preferred_element_type=F32))
            kd = (kb32[rows] * grow).astype(BF16)
            upd = lax.dot_general(kd, vc, (((0,), (0,)), ((), ())), preferred_element_type=F32)
            st = st * as_column(grow) + upd
        st_ref[...] = st
        finish(slice(0, tc), jnp.concatenate(outs, axis=0))

    def direct_chunk(ci, cum, st):
        rows, c2, qc, kc, vc, qe, o_inter, st = inter_chunk(ci, cum, st)

        p_rows = []
        k_blocks, prev_ref = [], None
        for si in range(nsub):
            lo_r = si * A_SUB
            cs, qs, ks = c2[lo_r:lo_r + A_SUB], qc[lo_r:lo_r + A_SUB], kc[lo_r:lo_r + A_SUB]
            off = None
            if si > 0:
                ref = c2[lo_r - 1:lo_r]
                if prev_ref is not None:
                    rebase = jnp.exp2(ref - prev_ref)
                    k_blocks = [kb * rebase for kb in k_blocks]
                k_blocks.append(kc[lo_r - A_SUB:lo_r] * jnp.exp2(ref - c2[lo_r - A_SUB:lo_r]))
                prev_ref = ref
                k_t = jnp.concatenate(
                    k_blocks + [jnp.zeros((A_CHUNK - lo_r, A_HEAD_DIM), F32)], axis=0)
                q_t = (qs * jnp.exp2(cs - ref)).astype(BF16)
                off = lax.dot_general(q_t, k_t.astype(BF16), (((1,), (1,)), ((), ())),
                                      preferred_element_type=F32)
            c_h = (cs[:half], cs[half:])
            q_h = (qs[:half], qs[half:])
            p_h = [jnp.zeros((half, A_HEAD_DIM), F32), jnp.zeros((half, A_HEAD_DIM), F32)]
            for sj in range(A_SUB):
                for hh in range(sj // half, 2):
                    decay = jnp.exp2(c_h[hh] - cs[sj:sj + 1])
                    colv = jnp.sum(q_h[hh] * (ks[sj:sj + 1] * decay), axis=1, keepdims=True)
                    sel = lane == lo_r + sj
                    if sj // half == hh:
                        sel = sel & (hrow >= sj - hh * half)
                    p_h[hh] = jnp.where(sel, colv, p_h[hh])
            for hh in range(2):
                ph = p_h[hh][:, :A_CHUNK]
                p_rows.append(ph if off is None else ph + off[hh * half:(hh + 1) * half])
        scores = jnp.concatenate(p_rows, axis=0).astype(BF16)
        finish(rows, o_inter + jnp.dot(scores, vc, preferred_element_type=F32))
        return st

    def run_direct():
        cum = _chunk_cumsum(lf_ref, tc)
        st_t = st_ref[...].T
        for ci in range(nchunk):
            st_t = direct_chunk(ci, cum, st_t)
        st_ref[...] = st_t.T

    bounded = bounded_ref[pl.program_id(0), pl.program_id(1)] > 0
    pl.when(bounded)(run_factorised)
    pl.when(jnp.logical_not(bounded))(run_direct)


def _hgrn_core(q, v, log_f, g, out_g, chunk_decay, *, tc=2048):
    s, d = q.shape
    tc = min(tc, s)
    assert s % tc == 0 and tc % A_CHUNK == 0 and d % A_HEAD_DIM == 0
    nhead, nblk = d // A_HEAD_DIM, s // tc
    worst = jnp.min(chunk_decay.reshape(nblk, tc // A_CHUNK, nhead, A_HEAD_DIM), axis=(1, 3))
    bounded = (worst.T * LOG2E >= -A_SAFE_LOG2_DECAY).astype(jnp.int32)
    blk = pl.BlockSpec((tc, A_HEAD_DIM), lambda h, c, flags: (c, h))
    vmem = 2 * (4 * _nbytes((tc, A_HEAD_DIM), BF16) + _nbytes((tc, A_HEAD_DIM), F32)) \
        + 12 * _nbytes((tc, A_HEAD_DIM), F32)
    grid_spec = pltpu.PrefetchScalarGridSpec(
        num_scalar_prefetch=1,
        grid=(nhead, nblk),
        in_specs=[blk, blk, blk, blk,
                  pl.BlockSpec((1, A_HEAD_DIM), lambda h, c, flags: (0, h))],
        out_specs=blk,
        scratch_shapes=[pltpu.VMEM((A_HEAD_DIM, A_HEAD_DIM), F32)],
    )
    return pl.pallas_call(
        functools.partial(_hgrn_body, tc=tc),
        grid_spec=grid_spec,
        out_shape=jax.ShapeDtypeStruct((s, d), BF16),
        compiler_params=pltpu.CompilerParams(dimension_semantics=("parallel", "arbitrary"),
                                             vmem_limit_bytes=_vmem_limit(vmem)),
        name="hgrn_core",
    )(bounded, q, v, log_f, g, out_g.reshape(1, d))


def _residual_matmul(hb, w, layer, resid, alpha, *, name, tm=MM_ROWS, tn=MM_COLS):
    s, d = resid.shape
    kdim = hb.shape[1]
    tk = min(kdim, d)
    npass = kdim // tk
    blk, imap = _tile(min(tm, s), min(tn, d))
    wblk, wimap = _tile(min(tm, s), min(MM_COLS_BF16_OUT, d))

    def ep_partial(accs, extras, outs, rows):
        outs[0][rows, :] = accs[0].astype(BF16)

    partials = []
    for kb in range(npass - 1):
        (part,) = _fused_matmul(hb, [(w, layer, 0)], d, [], [((s, d), BF16, wblk, wimap)],
                                ep_partial, tm=tm, tn=MM_COLS_BF16_OUT, tk=tk, k_block=kb,
                                name=name + "_partial")
        partials.append(part)

    def ep(accs, extras, outs, rows):
        out = alpha * extras[0][rows, :] + accs[0]
        for part_ref in extras[1:]:
            out = out + part_ref[rows, :].astype(F32)
        outs[0][rows, :] = out

    extras = [(resid, blk, imap)] + [(part, blk, imap) for part in partials]
    (y,) = _fused_matmul(hb, [(w, layer, 0)], d, extras, [((s, d), F32, blk, imap)], ep,
                         tm=tm, tn=tn, tk=tk, k_block=npass - 1, name=name)
    return y


def _rel_bucket_band():
    i = np.arange(B_BLOCK)[:, None]
    j = np.arange(2 * B_BLOCK)[None, :]
    d = np.clip(B_BLOCK + i - j, 0, None)
    max_exact = REL_BUCKETS // 2
    large = max_exact + (np.log(np.maximum(d, 1) / max_exact)
                         / np.log(REL_MAX_DISTANCE / max_exact)
                         * (REL_BUCKETS - max_exact)).astype(np.int32)
    large = np.minimum(large, REL_BUCKETS - 1)
    return np.where(d < max_exact, d, large).astype(np.int32)


def _bias_body(rb_ref, bucket_ref, o_ref, *, heads_per_step):
    h0 = pl.program_id(0) * heads_per_step
    bucket = bucket_ref[...]
    qi = lax.broadcasted_iota(jnp.int32, bucket.shape, 0)
    kj = lax.broadcasted_iota(jnp.int32, bucket.shape, 1)
    dist = B_BLOCK + qi - kj
    in_window = (dist >= 0) & (dist < WINDOW)
    for hh in range(heads_per_step):
        acc = jnp.zeros(bucket.shape, F32)
        for b in range(REL_BUCKETS):
            acc = jnp.where(bucket == b, rb_ref[b, h0 + hh], acc)
        o_ref[hh] = jnp.where(in_window, acc, -jnp.inf)


def _rel_bias_band(rel_bias, *, heads_per_step=8):
    nh = rel_bias.shape[1]
    bucket = jnp.asarray(_rel_bucket_band())
    return pl.pallas_call(
        functools.partial(_bias_body, heads_per_step=heads_per_step),
        grid=(nh // heads_per_step,),
        in_specs=[pl.BlockSpec(memory_space=pltpu.SMEM),
                  pl.BlockSpec((B_BLOCK, 2 * B_BLOCK), lambda i: (0, 0))],
        out_specs=pl.BlockSpec((heads_per_step, B_BLOCK, 2 * B_BLOCK), lambda i: (i, 0, 0)),
        out_shape=jax.ShapeDtypeStruct((nh, B_BLOCK, 2 * B_BLOCK), F32),
        compiler_params=pltpu.CompilerParams(dimension_semantics=("parallel",)),
        name="rel_bias_band",
    )(rel_bias, bucket)


def _head_split_matmul(xb, w, layer, n_cols, *, name, tm=MM_ROWS, tn=MM_COLS):
    s, kdim = xb.shape
    tm, tn = min(tm, s), min(tn, n_cols)
    hpt = tn // B_HEAD_DIM

    def ep(accs, extras, outs, rows):
        for c in range(hpt):
            outs[0][c, rows, :] = accs[0][:, c * B_HEAD_DIM:(c + 1) * B_HEAD_DIM].astype(BF16)

    outs = [((n_cols // B_HEAD_DIM, s, B_HEAD_DIM), BF16, (hpt, tm, B_HEAD_DIM),
             lambda i, j: (j, i, 0))]
    (out,) = _fused_matmul(xb, [(w, layer, 0)], n_cols, [], outs, ep, tm=tm, tn=tn, name=name)
    return out


def _attn_body(sink_ref, q_ref, kp_ref, kc_ref, vp_ref, vc_ref, bias_ref, o_ref, *, group, qb):
    h = pl.program_id(0)
    m_blk = pl.program_id(1)
    c = B_BLOCK
    k_all = jnp.concatenate([kp_ref[0], kc_ref[0]], axis=0)
    v_all = jnp.concatenate([vp_ref[0], vc_ref[0]], axis=0)
    kj = lax.broadcasted_iota(jnp.int32, (c, 2 * c), 1)
    for b in range(qb):
        q = q_ref[:, b * c:(b + 1) * c, :].reshape(group * c, B_HEAD_DIM) * B_LOGIT_SCALE
        kb = k_all[b * c:(b + 2) * c]
        vb = v_all[b * c:(b + 2) * c]
        logits = lax.dot_general(q, kb, (((1,), (1,)), ((), ())), preferred_element_type=F32)
        probs = []
        for gi in range(group):
            lg = logits[gi * c:(gi + 1) * c] + bias_ref[gi]
            if b == 0:
                lg = jnp.where((m_blk > 0) | (kj >= c), lg, -jnp.inf)
            sink = sink_ref[h * group + gi]
            m = jnp.maximum(jnp.max(lg, axis=-1, keepdims=True), sink)
            e = jnp.exp(lg - m)
            denom = jnp.sum(e, axis=-1, keepdims=True) + jnp.exp(sink - m)
            probs.append((e / denom).astype(BF16))
        pv = jnp.dot(jnp.concatenate(probs, axis=0), vb, preferred_element_type=F32)
        o_ref[b * c:(b + 1) * c, :] = jnp.concatenate(
            [pv[gi * c:(gi + 1) * c] for gi in range(group)], axis=1).astype(BF16)


def _swa_core(qh, kvh, sinks, bias, n_kv, *, qb=4):
    nq, s, hd = qh.shape
    group = nq // n_kv
    nb = s // B_BLOCK
    qb = min(qb, nb)
    assert nb % qb == 0
    prev = lambda m: jnp.maximum(m * qb - 1, 0)
    prev_blk = (1, B_BLOCK, hd)
    cur_blk = (1, qb * B_BLOCK, hd)
    in_specs = [
        pl.BlockSpec(memory_space=pltpu.SMEM),
        pl.BlockSpec((group, qb * B_BLOCK, hd), lambda h, m: (h, m, 0)),
        pl.BlockSpec(prev_blk, lambda h, m: (h, prev(m), 0)),
        pl.BlockSpec(cur_blk, lambda h, m: (h, m, 0)),
        pl.BlockSpec(prev_blk, lambda h, m: (h + n_kv, prev(m), 0)),
        pl.BlockSpec(cur_blk, lambda h, m: (h + n_kv, m, 0)),
        pl.BlockSpec((group, B_BLOCK, 2 * B_BLOCK), lambda h, m: (h, 0, 0)),
    ]
    vmem = 2 * (_nbytes((group, B_BLOCK, 2 * B_BLOCK), F32)
                + (group + 3) * _nbytes((qb * B_BLOCK, V7X_LANES), BF16)
                + _nbytes((qb * B_BLOCK, group * hd), BF16)) \
        + 6 * qb * _nbytes((group * B_BLOCK, 2 * B_BLOCK), F32)
    return pl.pallas_call(
        functools.partial(_attn_body, group=group, qb=qb),
        grid=(n_kv, nb // qb),
        in_specs=in_specs,
        out_specs=pl.BlockSpec((qb * B_BLOCK, group * hd), lambda h, m: (m, h)),
        out_shape=jax.ShapeDtypeStruct((s, nq * hd), BF16),
        compiler_params=pltpu.CompilerParams(dimension_semantics=("parallel", "parallel"),
                                             vmem_limit_bytes=_vmem_limit(vmem)),
        name="swa_core",
    )(sinks, qh, kvh, kvh, kvh, kvh, bias)


def _swiglu_up(xb, w1, w3, layer, *, tm=MM_ROWS, tn=MM_COLS):
    s, d = xb.shape
    f = w1.shape[2]
    blk, imap = _tile(min(tm, s), min(tn, f))

    def ep(accs, extras, outs, rows):
        outs[0][rows, :] = (_silu(accs[0]) * accs[1]).astype(BF16)

    (h,) = _fused_matmul(xb, [(w1, layer, 0), (w3, layer, 0)], f, [],
                         [((s, f), BF16, blk, imap)], ep, tm=tm, tn=tn, name="swiglu_up")
    return h


def _routing_tables(gates, idx, tm):
    s, ne = gates.shape
    n_assign = s * TOP_K
    ntiles = n_assign // tm + ne
    e_flat = idx.reshape(n_assign)
    onehot = (e_flat[:, None] == jnp.arange(ne, dtype=jnp.int32)[None, :]).astype(jnp.int32)
    rank = jnp.take_along_axis(jnp.cumsum(onehot, axis=0) - onehot, e_flat[:, None], axis=1)[:, 0]
    counts = jnp.sum(onehot, axis=0)
    padded = ((counts + tm - 1) // tm) * tm
    ends = jnp.cumsum(padded)
    pos = (ends - padded)[e_flat] + rank
    token = jnp.arange(n_assign, dtype=jnp.int32) // TOP_K
    gate = jnp.take_along_axis(gates, idx, axis=1)
    row_token = jnp.zeros((ntiles * tm,), jnp.int32).at[pos].set(token)
    tile_start = jnp.arange(ntiles, dtype=jnp.int32) * tm
    tile_expert = jnp.minimum(jnp.searchsorted(ends, tile_start, side="right"), ne - 1).astype(jnp.int32)
    tile_valid = tile_start < ends[-1]
    first = tile_valid & jnp.concatenate(
        [jnp.ones((1,), bool), tile_expert[1:] != tile_expert[:-1]])
    slot = (jnp.cumsum(first.astype(jnp.int32)) - 1) % 2
    eid = jnp.arange(ne, dtype=jnp.int32)
    later = (counts > 0)[None, :] & (eid[None, :] > eid[:, None])
    next_used = jnp.min(jnp.where(later, eid[None, :], ne), axis=1)
    next_used = jnp.where(next_used == ne, -1, next_used)
    sched = jnp.stack([tile_expert, tile_valid.astype(jnp.int32), first.astype(jnp.int32),
                       slot.astype(jnp.int32), next_used[tile_expert]]).astype(jnp.int32)
    return row_token, gate, pos.reshape(s, TOP_K).astype(jnp.int32), sched, ntiles


GATHER_ISSUE_UNROLL = 8


def _gather_body(tv_ref, tok_ref, tok_next_ref, x_hbm, o_ref, buf, sem, *, tm, ntiles):
    i = pl.program_id(0)
    slot = i % 2

    def row_copy(tile_tok_ref, slot_, r):
        return pltpu.make_async_copy(x_hbm.at[pl.ds(tile_tok_ref[0, 0, r], 1)],
                                     buf.at[slot_, pl.ds(r, 1)], sem.at[slot_])

    def start_tile(tile_tok_ref, slot_):
        def body(r, carry):
            row_copy(tile_tok_ref, slot_, r).start()
            return carry
        lax.fori_loop(0, tm, body, 0, unroll=GATHER_ISSUE_UNROLL)

    @pl.when((i == 0) & (tv_ref[0] > 0))
    def _():
        start_tile(tok_ref, slot)

    @pl.when((i + 1 < ntiles) & (tv_ref[jnp.minimum(i + 1, ntiles - 1)] > 0))
    def _():
        start_tile(tok_next_ref, 1 - slot)

    @pl.when(tv_ref[i] > 0)
    def _():
        def wait_body(r, carry):
            row_copy(tok_ref, slot, r).wait()
            return carry
        lax.fori_loop(0, tm, wait_body, 0, unroll=GATHER_ISSUE_UNROLL)
        o_ref[...] = buf[slot].astype(BF16)

    @pl.when(tv_ref[i] == 0)
    def _():
        o_ref[...] = jnp.zeros_like(o_ref)


def _moe_gather(x, row_token, tile_valid, ntiles, tm):
    s, d = x.shape
    vmem = 2 * _nbytes((tm, d), F32) + 2 * _nbytes((tm, d), BF16) + _nbytes((tm, d), F32)
    tok = row_token.reshape(ntiles, 1, tm)
    grid_spec = pltpu.PrefetchScalarGridSpec(
        num_scalar_prefetch=1,
        grid=(ntiles,),
        in_specs=[pl.BlockSpec((1, 1, tm), lambda i, tv: (i, 0, 0), memory_space=pltpu.SMEM),
                  pl.BlockSpec((1, 1, tm), lambda i, tv: (jnp.minimum(i + 1, ntiles - 1), 0, 0),
                               memory_space=pltpu.SMEM),
                  pl.BlockSpec(memory_space=pl.ANY)],
        out_specs=pl.BlockSpec((tm, d), lambda i, tv: (i, 0)),
        scratch_shapes=[pltpu.VMEM((2, tm, d), F32), pltpu.SemaphoreType.DMA((2,))],
    )
    return pl.pallas_call(
        functools.partial(_gather_body, tm=tm, ntiles=ntiles),
        grid_spec=grid_spec,
        out_shape=jax.ShapeDtypeStruct((ntiles * tm, d), BF16),
        compiler_params=pltpu.CompilerParams(dimension_semantics=("arbitrary",),
                                             vmem_limit_bytes=_vmem_limit(vmem)),
        name="moe_gather",
    )(tile_valid, tok, tok, x)


SCHED_EXPERT, SCHED_VALID, SCHED_FIRST, SCHED_SLOT, SCHED_NEXT = range(5)


def _grouped_body(sched_ref, x_ref, *rest, nw, layer, tcol, epilogue):
    w_hbm, o_ref, wbufs, sem = rest[:nw], rest[nw], rest[nw + 1:2 * nw + 1], rest[2 * nw + 1]
    j, i = pl.program_id(0), pl.program_id(1)
    expert, slot, nxt = sched_ref[SCHED_EXPERT, i], sched_ref[SCHED_SLOT, i], sched_ref[SCHED_NEXT, i]
    first = sched_ref[SCHED_FIRST, i] > 0
    col0 = pl.multiple_of(j * tcol, tcol)

    def copies(e, s):
        return [pltpu.make_async_copy(w.at[layer, e, :, pl.ds(col0, tcol)], wb.at[s], sem.at[s])
                for w, wb in zip(w_hbm, wbufs)]

    @pl.when(i == 0)
    def _():
        for c in copies(expert, slot):
            c.start()

    @pl.when(first & (nxt >= 0))
    def _():
        for c in copies(nxt, 1 - slot):
            c.start()

    @pl.when(first)
    def _():
        for c in copies(expert, slot):
            c.wait()

    @pl.when(sched_ref[SCHED_VALID, i] > 0)
    def _():
        wbs = [wb[slot].astype(BF16) for wb in wbufs]
        sub = min(x_ref.shape[0], MM_SUB_ROWS)
        for r0 in range(0, x_ref.shape[0], sub):
            xv = x_ref[r0:r0 + sub, :]
            o_ref[r0:r0 + sub, :] = epilogue(
                [jnp.dot(xv, wb, preferred_element_type=F32) for wb in wbs])

    @pl.when(sched_ref[SCHED_VALID, i] == 0)
    def _():
        o_ref[...] = jnp.zeros_like(o_ref)


def _grouped_matmul(x, ws, layer, sched, tm, tcol, out_dtype, epilogue, name):
    rows, kdim = x.shape
    n = ws[0].shape[3]
    tcol = min(tcol, n)
    ntiles = rows // tm
    vmem = 2 * _nbytes((tm, kdim), BF16) + 2 * _nbytes((tm, tcol), out_dtype) \
        + len(ws) * (2 * _nbytes((kdim, tcol), F32) + _nbytes((kdim, tcol), BF16)) \
        + (1 + len(ws)) * _nbytes((tm, tcol), F32)
    grid_spec = pltpu.PrefetchScalarGridSpec(
        num_scalar_prefetch=1,
        grid=(n // tcol, ntiles),
        in_specs=[pl.BlockSpec((tm, kdim), lambda j, i, sched: (i, 0))]
        + [pl.BlockSpec(memory_space=pl.ANY)] * len(ws),
        out_specs=pl.BlockSpec((tm, tcol), lambda j, i, sched: (i, j)),
        scratch_shapes=[pltpu.VMEM((2, kdim, tcol), F32) for _ in ws]
        + [pltpu.SemaphoreType.DMA((2,))],
    )
    return pl.pallas_call(
        functools.partial(_grouped_body, nw=len(ws), layer=layer, tcol=tcol, epilogue=epilogue),
        grid_spec=grid_spec,
        out_shape=jax.ShapeDtypeStruct((rows, n), out_dtype),
        compiler_params=pltpu.CompilerParams(dimension_semantics=("arbitrary", "arbitrary"),
                                             vmem_limit_bytes=_vmem_limit(vmem)),
        name=name,
    )(sched, x, *ws)


def _moe_up(xg, w1, w3, layer, sched, tm, *, tf=512):
    return _grouped_matmul(xg, [w1, w3], layer, sched, tm, tf, BF16,
                           lambda p: (_silu(p[0]) * p[1]).astype(BF16), "moe_up")


def _moe_down(h, w2, layer, sched, tm, *, tn=2048):
    return _grouped_matmul(h, [w2], layer, sched, tm, tn, F32, lambda p: p[0], "moe_down")


def _combine_body(pos_ref, pos_next_ref, y_hbm, x_ref, gate_ref, g_ref, b_ref, o32_ref, o16_ref,
                  buf, sem, *, tr, nt, alpha):
    i = pl.program_id(0)
    slot = i % 2

    def row_copy(tile_pos_ref, slot_, r, kk):
        return pltpu.make_async_copy(y_hbm.at[pl.ds(tile_pos_ref[0, kk, r], 1)],
                                     buf.at[slot_, kk, pl.ds(r, 1)], sem.at[slot_])

    def start_tile(tile_pos_ref, slot_):
        def body(r, carry):
            for kk in range(TOP_K):
                row_copy(tile_pos_ref, slot_, r, kk).start()
            return carry
        lax.fori_loop(0, tr, body, 0, unroll=GATHER_ISSUE_UNROLL)

    @pl.when(i == 0)
    def _():
        start_tile(pos_ref, slot)

    @pl.when(i + 1 < nt)
    def _():
        start_tile(pos_next_ref, 1 - slot)

    def wait_body(r, carry):
        for kk in range(TOP_K):
            row_copy(pos_ref, slot, r, kk).wait()
        return carry
    lax.fori_loop(0, tr, wait_body, 0, unroll=GATHER_ISSUE_UNROLL)

    gate = gate_ref[...]
    f = buf[slot, 0] * gate[:, 0:1]
    for kk in range(1, TOP_K):
        f = f + buf[slot, kk] * gate[:, kk:kk + 1]
    out = _ln_rows(alpha * x_ref[...] + f, g_ref[...], b_ref[...])
    o32_ref[...] = out
    o16_ref[...] = out.astype(BF16)


def _moe_combine_ln(y, pos, gate, x, g, b, alpha, *, tr=256):
    s, d = x.shape
    tr = min(tr, s)
    nt = s // tr
    pos_t = pos.reshape(nt, tr, TOP_K).transpose(0, 2, 1)
    row = pl.BlockSpec((tr, d), lambda i: (i, 0))
    vec = pl.BlockSpec((1, d), lambda i: (0, 0))
    vmem = 2 * TOP_K * _nbytes((tr, d), F32) \
        + 2 * (2 * _nbytes((tr, d), F32) + _nbytes((tr, d), BF16)) + 3 * _nbytes((tr, d), F32)
    pos_blk = (1, TOP_K, tr)
    return pl.pallas_call(
        functools.partial(_combine_body, tr=tr, nt=nt, alpha=alpha),
        grid=(nt,),
        in_specs=[pl.BlockSpec(pos_blk, lambda i: (i, 0, 0), memory_space=pltpu.SMEM),
                  pl.BlockSpec(pos_blk, lambda i: (jnp.minimum(i + 1, nt - 1), 0, 0),
                               memory_space=pltpu.SMEM),
                  pl.BlockSpec(memory_space=pl.ANY), row,
                  pl.BlockSpec((tr, TOP_K), lambda i: (i, 0)), vec, vec],
        out_specs=[row, row],
        out_shape=[jax.ShapeDtypeStruct((s, d), F32), jax.ShapeDtypeStruct((s, d), BF16)],
        scratch_shapes=[pltpu.VMEM((2, TOP_K, tr, d), F32), pltpu.SemaphoreType.DMA((2,))],
        compiler_params=pltpu.CompilerParams(dimension_semantics=("arbitrary",),
                                             vmem_limit_bytes=_vmem_limit(vmem)),
        name="moe_combine_ln",
    )(pos_t, pos_t, y, x, gate, g.reshape(1, d), b.reshape(1, d))


def _ple(xb, x, p, w_gate, w_proj, layer, bi, *, tm=MM_ROWS, tn=MM_COLS):
    s, d = x.shape
    pdim = p.shape[-1]
    blk, imap = _tile(min(tm, s), min(tn, d))

    def ep(accs, extras, outs, rows):
        proj = jnp.dot(extras[1][rows, :].astype(BF16), extras[2][...].astype(BF16),
                       preferred_element_type=F32)
        out = extras[0][rows, :] + jax.nn.sigmoid(accs[0]) * proj
        outs[0][rows, :] = out
        outs[1][rows, :] = out.astype(BF16)

    extras = [(x, blk, imap),
              (p, (None, None, blk[0], pdim), lambda i, j: (layer, bi, i, 0)),
              (w_proj, (None, pdim, blk[1]), lambda i, j: (layer, 0, j))]
    outs = [((s, d), F32, blk, imap), ((s, d), BF16, blk, imap)]
    return _fused_matmul(xb, [(w_gate, layer, 0)], d, extras, outs, ep, tm=tm, tn=tn, name="ple")


MOE_TILE_ROWS = 512


def kernel(x, p, ln1_g, ln1_b, ln2_g, ln2_b, a_w_in, a_lb_logits, a_out_g, a_w_o, kv_w, b_w_q, b_sinks, b_w_o, rel_bias, ffn_w1, ffn_w3, ffn_w2, moe_router, moe_w1, moe_w3, moe_w2, ple_w_proj, ple_w_gate):
    batch, s, d = x.shape
    depth = p.shape[0]
    n_a = a_w_in.shape[0]
    kv_dim = kv_w.shape[1] // 2
    n_kv = kv_dim // B_HEAD_DIM
    alpha = (2.0 * depth) ** 0.25
    moe_tm = min(MOE_TILE_ROWS, s)
    bias_band = _rel_bias_band(rel_bias.astype(F32))
    lb_logits = a_lb_logits.astype(F32)

    outs = []
    for bi in range(batch):
        xf = x[bi].astype(F32)
        xb = xf.astype(BF16)
        kvh = None
        for i in range(depth):
            if i < n_a:
                q, v, log_f, g, chunk_decay = _hgrn_in_proj(xb, a_w_in, lb_logits, i)
                hb = _hgrn_core(q, v, log_f, g, a_out_g[i].astype(F32), chunk_decay)
                y = _residual_matmul(hb, a_w_o, i, xf, alpha, name="mixer_out")
            else:
                j = i - n_a
                qh = _head_split_matmul(xb, b_w_q, j, d, name="swa_q")
                hb = _swa_core(qh, kvh, b_sinks[j].astype(F32), bias_band, n_kv)
                y = _residual_matmul(hb, b_w_o, j, xf, alpha, name="mixer_out")
            li = i // 2
            if i % 2 == 0:
                xf, xb = _layer_norm(y, ln1_g[i].astype(F32), ln1_b[i].astype(F32))
                hmid = _swiglu_up(xb, ffn_w1, ffn_w3, li)
                y = _residual_matmul(hmid, ffn_w2, li, xf, alpha, name="ffn_down")
                xf, xb = _layer_norm(y, ln2_g[i].astype(F32), ln2_b[i].astype(F32))
            else:
                xf, xb, gates, idx = _layer_norm(y, ln1_g[i].astype(F32), ln1_b[i].astype(F32),
                                                 moe_router[li].astype(F32))
                row_token, gate, pos, sched, ntiles = _routing_tables(gates, idx, moe_tm)
                xg = _moe_gather(xf, row_token, sched[SCHED_VALID], ntiles, moe_tm)
                hmid = _moe_up(xg, moe_w1, moe_w3, li, sched, moe_tm)
                yg = _moe_down(hmid, moe_w2, li, sched, moe_tm)
                xf, xb = _moe_combine_ln(yg, pos, gate, xf, ln2_g[i].astype(F32),
                                         ln2_b[i].astype(F32), alpha)
            xf, xb = _ple(xb, xf, p, ple_w_gate, ple_w_proj, i, bi)
            if i == n_a - 1:
                kvh = _head_split_matmul(xb, kv_w, None, 2 * kv_dim, name="kv_proj")
        outs.append(xf)
    return jnp.stack(outs, axis=0).astype(x.dtype)
```
